```python
import math
import jax
import jax.numpy as jnp
from jax import lax
import numpy as np

D_MODEL = 2048
BATCH = 2
SEQ = 8192
DEPTH = 1

N_HEADS = 16
N_KV_GROUPS = 2
HEADS_PER_GROUP = N_HEADS // N_KV_GROUPS
HEAD_DIM = 128
CMP_BLOCK = 32
CMP_STRIDE = 16
SEL_BLOCK = 64
SEL_TOP_N = 16
WINDOW = 512
Q_BLOCK = 128
N_OVERLAP = (SEL_BLOCK + CMP_BLOCK) // CMP_STRIDE - 1
FORCE_BONUS = 1.0e4
CONV_CH = D_MODEL // 2
CONV_WIDTH = 31
N_EXPERTS = 32
TOP_K = 4
D_FF = D_MODEL
SWIGLU_LIMIT = 7.0
SWIGLU_ALPHA = 1.702
EXPERT_BLOCK = 256
LN_EPS = 1e-5
DEEPNORM_ALPHA = (2 * DEPTH) ** 0.25
DEEPNORM_BETA = (8 * DEPTH) ** -0.25
NEG_INF = -1e30
TINY = 1e-30
Q_DIM = N_HEADS * HEAD_DIM
KV_DIM = N_KV_GROUPS * HEAD_DIM
NSA_GATE_DIM = N_HEADS * 3
GLU_DIM = 2 * CONV_CH
MERGE_DIM = 2 * D_MODEL
SPLITS = (Q_DIM, KV_DIM, KV_DIM, KV_DIM, KV_DIM, KV_DIM, KV_DIM, NSA_GATE_DIM, GLU_DIM, MERGE_DIM)
IN_DIM = sum(SPLITS)

kernel_name = "nsa_conformer_moe_hybrid_block"


def layer_norm(x, g, b):
    xf = x.astype(jnp.float32)
    mu = jnp.mean(xf, axis=-1, keepdims=True)
    var = jnp.mean(jnp.square(xf - mu), axis=-1, keepdims=True)
    return ((xf - mu) * lax.rsqrt(var + LN_EPS) * g + b).astype(x.dtype)


def masked_softmax(s, mask):
    s = jnp.where(mask, s.astype(jnp.float32), NEG_INF)
    m = jnp.max(s, axis=-1, keepdims=True)
    e = jnp.where(mask, jnp.exp(s - m), 0.0)
    return e / jnp.maximum(jnp.sum(e, axis=-1, keepdims=True), TINY)


def alibi_slopes():
    h = jnp.arange(1, N_HEADS + 1, dtype=jnp.float32)
    return jnp.exp2(-8.0 * h / N_HEADS).reshape(N_KV_GROUPS, HEADS_PER_GROUP)


def compress(kv, pe, w1, b1, w2, b2):
    B, S, G, dh = kv.shape
    n_cmp = (S - CMP_BLOCK) // CMP_STRIDE + 1
    idx = np.arange(n_cmp)[:, None] * CMP_STRIDE + np.arange(CMP_BLOCK)[None, :]
    blocks = kv[:, idx] + pe[None, None, :, None, :]
    flat = blocks.transpose(0, 1, 3, 2, 4).reshape(B, n_cmp, G, CMP_BLOCK * dh)
    h = jax.nn.gelu(flat @ w1 + b1)
    return h @ w2 + b2


def gather_blocks(kb, top):
    return jax.vmap(jax.vmap(lambda a, i: a[i]))(kb, top)


def nsa_attention(q, kc, vc, ks, vs, kw, vw, g_br):
    B, S, H, dh = q.shape
    n_chunk = S // Q_BLOCK
    n_cmp = kc.shape[1]
    n_selb = S // SEL_BLOCK
    n_top = min(SEL_TOP_N, n_selb)
    slopes = alibi_slopes()
    cmp_end = jnp.arange(n_cmp, dtype=jnp.int32) * CMP_STRIDE + (CMP_BLOCK - 1)
    lo = (np.arange(n_selb) * SEL_BLOCK - CMP_BLOCK) // CMP_STRIDE + 1
    ovl = lo[:, None] + np.arange(N_OVERLAP)[None, :]
    ovl = np.where((ovl >= 0) & (ovl < n_cmp), ovl, n_cmp)
    scale = 1.0 / math.sqrt(dh)
    qg = (q * scale).reshape(B, n_chunk, Q_BLOCK, N_KV_GROUPS, HEADS_PER_GROUP, dh).transpose(1, 0, 3, 4, 2, 5)
    gg = g_br.reshape(B, n_chunk, Q_BLOCK, N_KV_GROUPS, HEADS_PER_GROUP, 3).transpose(1, 0, 3, 4, 2, 5)
    ks_b = ks.reshape(B, n_selb, SEL_BLOCK, N_KV_GROUPS, dh).transpose(0, 3, 1, 2, 4)
    vs_b = vs.reshape(B, n_selb, SEL_BLOCK, N_KV_GROUPS, dh).transpose(0, 3, 1, 2, 4)
    kw_pad = jnp.pad(kw, ((0, 0), (WINDOW, 0), (0, 0), (0, 0)))
    vw_pad = jnp.pad(vw, ((0, 0), (WINDOW, 0), (0, 0), (0, 0)))
    blk = jnp.arange(n_selb, dtype=jnp.int32)

    def chunk(args):
        qb, gb, c = args
        t = c * Q_BLOCK + jnp.arange(Q_BLOCK, dtype=jnp.int32)
        dist = t[:, None] - cmp_end[None, :]
        s = jnp.einsum('bghqd,bngd->bghqn', qb, kc) - slopes[:, :, None, None] * dist
        p = masked_softmax(s, dist >= 0)
        o_cmp = jnp.einsum('bghqn,bngd->bghqd', p, vc)
        imp = jnp.pad(jnp.sum(p, axis=2), ((0, 0), (0, 0), (0, 0), (0, 1)))
        imp_sel = jnp.sum(imp[..., ovl], axis=-1)
        cur = t // SEL_BLOCK
        valid = blk[None, :] * SEL_BLOCK <= t[:, None]
        forced = (blk[None, :] == 0) | (blk[None, :] == cur[:, None]) | (blk[None, :] == cur[:, None] - 1)
        score = jnp.where(valid, imp_sel + jnp.where(forced, FORCE_BONUS, 0.0), -1.0)
        _, top = lax.top_k(score, n_top)
        k_g = gather_blocks(ks_b, top)
        v_g = gather_blocks(vs_b, top)
        pos = top[..., None] * SEL_BLOCK + jnp.arange(SEL_BLOCK, dtype=jnp.int32)
        d_s = t[None, None, :, None, None] - pos
        s2 = jnp.einsum('bghqd,bgqnkd->bghqnk', qb, k_g) - slopes[None, :, :, None, None, None] * d_s[:, :, None]
        p2 = masked_softmax(s2.reshape(B, N_KV_GROUPS, HEADS_PER_GROUP, Q_BLOCK, n_top * SEL_BLOCK),
                            (d_s >= 0)[:, :, None].reshape(B, N_KV_GROUPS, 1, Q_BLOCK, n_top * SEL_BLOCK))
        o_slc = jnp.einsum('bghqnk,bgqnkd->bghqd',
                           p2.reshape(B, N_KV_GROUPS, HEADS_PER_GROUP, Q_BLOCK, n_top, SEL_BLOCK), v_g)
        k_win = lax.dynamic_slice_in_dim(kw_pad, c * Q_BLOCK, Q_BLOCK + WINDOW, axis=1)
        v_win = lax.dynamic_slice_in_dim(vw_pad, c * Q_BLOCK, Q_BLOCK + WINDOW, axis=1)
        spos = c * Q_BLOCK - WINDOW + jnp.arange(Q_BLOCK + WINDOW, dtype=jnp.int32)
        dw = t[:, None] - spos[None, :]
        mask_w = (dw >= 0) & (dw < WINDOW) & (spos[None, :] >= 0)
        s3 = jnp.einsum('bghqd,bkgd->bghqk', qb, k_win) - slopes[:, :, None, None] * dw
        o_win = jnp.einsum('bghqk,bkgd->bghqd', masked_softmax(s3, mask_w), v_win)
        gs = jax.nn.sigmoid(gb.astype(jnp.float32))
        o = gs[..., 0:1] * o_cmp + gs[..., 1:2] * o_slc + gs[..., 2:3] * o_win
        return o.astype(q.dtype)

    out = lax.map(chunk, (qg, gg, jnp.arange(n_chunk, dtype=jnp.int32)))
    return out.transpose(1, 0, 4, 2, 3, 5).reshape(B, S, H * dh)


def conformer_conv(glu_in, conv_w, conv_b, ln_g, ln_b, w_proj):
    a, b = jnp.split(glu_in, 2, axis=-1)
    h = a * jax.nn.sigmoid(b)
    h = lax.conv_general_dilated(h, conv_w[:, None, :], window_strides=(1,),
                                 padding=[(CONV_WIDTH - 1, 0)],
                                 dimension_numbers=('NWC', 'WIO', 'NWC'),
                                 feature_group_count=CONV_CH) + conv_b
    h = jax.nn.silu(layer_norm(h, ln_g, ln_b))
    return h @ w_proj


def hybrid_mixer(x, w_in, cmp_pe, cmp_w1, cmp_b1, cmp_w2, cmp_b2, w_nsa_proj,
                 conv_w, conv_b, conv_ln_g, conv_ln_b, w_conv_proj, w_out):
    B, S, _ = x.shape
    proj = x @ w_in
    q, kc_r, vc_r, ks, vs, kw, vw, g_nsa, glu_in, g_merge = jnp.split(
        proj, np.cumsum(SPLITS)[:-1].tolist(), axis=-1)
    kvs = lambda t: t.reshape(B, S, N_KV_GROUPS, HEAD_DIM)
    kc = compress(kvs(kc_r), cmp_pe[0], cmp_w1[0], cmp_b1[0], cmp_w2[0], cmp_b2[0])
    vc = compress(kvs(vc_r), cmp_pe[1], cmp_w1[1], cmp_b1[1], cmp_w2[1], cmp_b2[1])
    o_nsa = nsa_attention(q.reshape(B, S, N_HEADS, HEAD_DIM), kc, vc, kvs(ks), kvs(vs),
                          kvs(kw), kvs(vw), g_nsa.reshape(B, S, N_HEADS, 3))
    y_a = o_nsa @ w_nsa_proj
    y_b = conformer_conv(glu_in, conv_w, conv_b, conv_ln_g, conv_ln_b, w_conv_proj)
    g_a, g_b = jnp.split(jax.nn.sigmoid(g_merge), 2, axis=-1)
    return (g_a * y_a + g_b * y_b) @ w_out


def moe(x, router_w, router_b, w_gate, b_gate, w_up, b_up, w_down, b_down):
    B, S, D = x.shape
    T = B * S
    xt = x.reshape(T, D)
    logits = (xt @ router_w + router_b).astype(jnp.float32)
    top_v, top_i = lax.top_k(logits, TOP_K)
    gate = jax.nn.softmax(top_v, axis=-1)
    A = T * TOP_K
    flat_e = top_i.reshape(A)
    flat_tok = jnp.arange(A, dtype=jnp.int32) // TOP_K
    flat_w = gate.reshape(A)
    order = jnp.argsort(flat_e)
    sorted_e = flat_e[order]
    counts = jnp.zeros((N_EXPERTS,), jnp.int32).at[flat_e].add(1)
    padded = (counts + EXPERT_BLOCK - 1) // EXPERT_BLOCK * EXPERT_BLOCK
    pad_end = jnp.cumsum(padded)
    pad_start = pad_end - padded
    start = jnp.cumsum(counts) - counts
    dest = pad_start[sorted_e] + (jnp.arange(A, dtype=jnp.int32) - start[sorted_e])
    P = -(-A // EXPERT_BLOCK) * EXPERT_BLOCK + N_EXPERTS * EXPERT_BLOCK
    n_blk = P // EXPERT_BLOCK
    buf_tok = jnp.full((P,), T, jnp.int32).at[dest].set(flat_tok[order])
    buf_w = jnp.zeros((P,), jnp.float32).at[dest].set(flat_w[order])
    blk_start = jnp.arange(n_blk, dtype=jnp.int32) * EXPERT_BLOCK
    blk_e = jnp.minimum(jnp.sum(blk_start[:, None] >= pad_end[None, :], axis=1), N_EXPERTS - 1)
    x_pad = jnp.concatenate([xt, jnp.zeros((1, D), xt.dtype)], axis=0)

    def expert_block(args):
        tok, e = args
        xb = x_pad[tok]
        g = jnp.minimum(xb @ w_gate[e] + b_gate[e], SWIGLU_LIMIT)
        u = jnp.clip(xb @ w_up[e] + b_up[e], -SWIGLU_LIMIT, SWIGLU_LIMIT)
        h = g * jax.nn.sigmoid(SWIGLU_ALPHA * g) * (u + 1.0)
        return h @ w_down[e] + b_down[e]

    out = lax.map(expert_block, (buf_tok.reshape(n_blk, EXPERT_BLOCK), blk_e)).reshape(P, D)
    out = (out * buf_w[:, None]).astype(x.dtype)
    y = jnp.zeros((T + 1, D), x.dtype).at[buf_tok].add(out)[:T]
    return y.reshape(B, S, D)


def setup_inputs(seed: int = 0) -> dict:
    key = jax.random.key(seed)
    k = jax.random.split(key, 32)
    L = DEPTH
    nrm = lambda kk, shape, sc: jax.random.normal(kk, shape, jnp.float32) * sc
    return {
        "x": nrm(k[0], (BATCH, SEQ, D_MODEL), 1.0),
        "w_in": nrm(k[1], (L, D_MODEL, IN_DIM), D_MODEL ** -0.5),
        "cmp_pe": nrm(k[2], (L, 2, CMP_BLOCK, HEAD_DIM), 0.1),
        "cmp_w1": nrm(k[3], (L, 2, CMP_BLOCK * HEAD_DIM, HEAD_DIM), (CMP_BLOCK * HEAD_DIM) ** -0.5),
        "cmp_b1": nrm(k[4], (L, 2, HEAD_DIM), 0.02),
        "cmp_w2": nrm(k[5], (L, 2, HEAD_DIM, HEAD_DIM), HEAD_DIM ** -0.5),
        "cmp_b2": nrm(k[6], (L, 2, HEAD_DIM), 0.02),
        "w_nsa_proj": nrm(k[7], (L, Q_DIM, D_MODEL), Q_DIM ** -0.5),
        "conv_w": nrm(k[8], (L, CONV_WIDTH, CONV_CH), CONV_WIDTH ** -0.5),
        "conv_b": nrm(k[9], (L, CONV_CH), 0.02),
        "conv_ln_g": 1.0 + nrm(k[10], (L, CONV_CH), 0.05),
        "conv_ln_b": nrm(k[11], (L, CONV_CH), 0.02),
        "w_conv_proj": nrm(k[12], (L, CONV_CH, D_MODEL), CONV_CH ** -0.5),
        "w_out": nrm(k[13], (L, D_MODEL, D_MODEL), DEEPNORM_BETA * D_MODEL ** -0.5),
        "ln1_g": 1.0 + nrm(k[14], (L, D_MODEL), 0.05),
        "ln1_b": nrm(k[15], (L, D_MODEL), 0.02),
        "router_w": nrm(k[16], (L, D_MODEL, N_EXPERTS), D_MODEL ** -0.5),
        "router_b": nrm(k[17], (L, N_EXPERTS), 0.01),
        "w_gate": nrm(k[18], (L, N_EXPERTS, D_MODEL, D_FF), D_MODEL ** -0.5),
        "b_gate": nrm(k[19], (L, N_EXPERTS, D_FF), 0.02),
        "w_up": nrm(k[20], (L, N_EXPERTS, D_MODEL, D_FF), D_MODEL ** -0.5),
        "b_up": nrm(k[21], (L, N_EXPERTS, D_FF), 0.02),
        "w_down": nrm(k[22], (L, N_EXPERTS, D_FF, D_MODEL), DEEPNORM_BETA * D_FF ** -0.5),
        "b_down": nrm(k[23], (L, N_EXPERTS, D_MODEL), 0.02),
        "ln2_g": 1.0 + nrm(k[24], (L, D_MODEL), 0.05),
        "ln2_b": nrm(k[25], (L, D_MODEL), 0.02),
    }


def reference(x, w_in, cmp_pe, cmp_w1, cmp_b1, cmp_w2, cmp_b2, w_nsa_proj,
              conv_w, conv_b, conv_ln_g, conv_ln_b, w_conv_proj, w_out, ln1_g, ln1_b,
              router_w, router_b, w_gate, b_gate, w_up, b_up, w_down, b_down, ln2_g, ln2_b):
    for l in range(DEPTH):
        m = hybrid_mixer(x, w_in[l], cmp_pe[l], cmp_w1[l], cmp_b1[l], cmp_w2[l], cmp_b2[l],
                         w_nsa_proj[l], conv_w[l], conv_b[l], conv_ln_g[l], conv_ln_b[l],
                         w_conv_proj[l], w_out[l])
        x = layer_norm(DEEPNORM_ALPHA * x + m, ln1_g[l], ln1_b[l])
        f = moe(x, router_w[l], router_b[l], w_gate[l], b_gate[l], w_up[l], b_up[l],
                w_down[l], b_down[l])
        x = layer_norm(DEEPNORM_ALPHA * x + f, ln2_g[l], ln2_b[l])
    return x
```

```python
import functools
import math

import numpy as np
import jax
import jax.numpy as jnp
from jax import lax
from jax.experimental import pallas as pl
from jax.experimental.pallas import tpu as pltpu

D_MODEL = 2048
N_HEADS = 16
N_KV_GROUPS = 2
HEADS_PER_GROUP = N_HEADS // N_KV_GROUPS
HEAD_DIM = 128
CMP_BLOCK = 32
CMP_STRIDE = 16
SEL_BLOCK = 64
SEL_SHIFT = 6
SEL_TOP_N = 16
WINDOW = 512
Q_BLOCK = 128
N_OVERLAP = (SEL_BLOCK + CMP_BLOCK) // CMP_STRIDE - 1
FORCE_BONUS = 1.0e4
CONV_CH = D_MODEL // 2
CONV_WIDTH = 31
N_EXPERTS = 32
TOP_K = 4
D_FF = D_MODEL
SWIGLU_LIMIT = 7.0
SWIGLU_ALPHA = 1.702
LN_EPS = 1e-5
DEPTH = 1
DEEPNORM_ALPHA = (2 * DEPTH) ** 0.25
NEG_INF = -1e30
TINY = 1e-30

Q_DIM = N_HEADS * HEAD_DIM
KV_DIM = N_KV_GROUPS * HEAD_DIM
NSA_GATE_DIM = N_HEADS * 3

LANES = 128
SUBLANES = 8
VMEM_LIMIT = 56 * 1024 * 1024

COL_Q = 0
COL_GLU = COL_Q + Q_DIM
COL_MERGE = COL_GLU + 2 * CONV_CH
COL_KC = COL_MERGE + 2 * D_MODEL
COL_VC = COL_KC + KV_DIM
COL_KS = COL_VC + KV_DIM
COL_VS = COL_KS + KV_DIM
COL_KW = COL_VS + KV_DIM
COL_VW = COL_KW + KV_DIM
COL_GATE = COL_VW + KV_DIM
PROJ_W = COL_GATE + N_KV_GROUPS * LANES

BF16 = jnp.bfloat16
F32 = jnp.float32

MOE_BLOCK = 256
MOE_FF_TILE = 512

_ALIBI = np.exp2(-8.0 * np.arange(1, N_HEADS + 1, dtype=np.float32) / N_HEADS).astype(np.float32)
_ALIBI = _ALIBI.reshape(N_KV_GROUPS, HEADS_PER_GROUP)


def _cparams(sem, vmem=VMEM_LIMIT):
    return pltpu.CompilerParams(dimension_semantics=sem, vmem_limit_bytes=vmem)


def _inproj_kernel(x_ref, w_ref, o_ref, xb_ref):
    @pl.when(pl.program_id(1) == 0)
    def _():
        xb_ref[...] = x_ref[...].astype(BF16)

    o_ref[...] = jnp.dot(xb_ref[...], w_ref[...],
                         preferred_element_type=F32).astype(o_ref.dtype)


def _inproj(x2d, w_p):
    T, D = x2d.shape
    N = w_p.shape[1]
    tm = min(1024, T)
    tn = 768
    assert T % tm == 0 and N % tn == 0
    return pl.pallas_call(
        _inproj_kernel,
        out_shape=jax.ShapeDtypeStruct((T, N), BF16),
        grid=(T // tm, N // tn),
        in_specs=[pl.BlockSpec((tm, D), lambda i, j: (i, 0)),
                  pl.BlockSpec((D, tn), lambda i, j: (0, j))],
        out_specs=pl.BlockSpec((tm, tn), lambda i, j: (i, j)),
        scratch_shapes=[pltpu.VMEM((tm, D), BF16)],
        compiler_params=_cparams(("arbitrary", "arbitrary")),
        name="inproj",
    )(x2d, w_p)


def _gelu_tanh(x):
    c = math.sqrt(2.0 / math.pi)
    return 0.5 * x * (1.0 + jnp.tanh(c * (x + 0.044715 * (x * x * x))))


def _compress_kernel(x_ref, pe_ref, w1_ref, b1_ref, w2_ref, b2_ref, o_ref):
    half = CMP_STRIDE * HEAD_DIM
    x = x_ref[0, 0].astype(F32)
    pe = pe_ref[0]
    xt = (x + pe[:, :half]).astype(BF16)
    xb = (x + pe[:, half:]).astype(BF16)
    a = jnp.dot(xt, w1_ref[0, :half, :], preferred_element_type=F32)
    bm = jnp.dot(xb, w1_ref[0, half:, :], preferred_element_type=F32)
    n16 = a.shape[0]
    pre = a + pltpu.roll(bm, n16 - 1, 0) + b1_ref[0]
    h = _gelu_tanh(pre)
    o = jnp.dot(h.astype(BF16), w2_ref[0], preferred_element_type=F32) + b2_ref[0]
    o_ref[0, 0] = o.astype(o_ref.dtype)


def _compress(xs, pe, w1, b1, w2, b2):
    n_slot, B, n16, wide = xs.shape
    kv = lambda s, b: (s // N_KV_GROUPS, 0, 0)
    return pl.pallas_call(
        _compress_kernel,
        out_shape=jax.ShapeDtypeStruct((n_slot, B, n16, HEAD_DIM), BF16),
        grid=(n_slot, B),
        in_specs=[pl.BlockSpec((1, 1, n16, wide), lambda s, b: (s, b, 0, 0)),
                  pl.BlockSpec((1, 1, 2 * wide), kv),
                  pl.BlockSpec((1, 2 * wide, HEAD_DIM), kv),
                  pl.BlockSpec((1, 1, HEAD_DIM), kv),
                  pl.BlockSpec((1, HEAD_DIM, HEAD_DIM), kv),
                  pl.BlockSpec((1, 1, HEAD_DIM), kv)],
        out_specs=pl.BlockSpec((1, 1, n16, HEAD_DIM), lambda s, b: (s, b, 0, 0)),
        compiler_params=_cparams(("arbitrary", "arbitrary")),
        name="compress",
    )(xs, pe, w1, b1, w2, b2)


def _stack_heads(q):
    return jnp.concatenate(
        [q[:, h * HEAD_DIM:(h + 1) * HEAD_DIM] for h in range(HEADS_PER_GROUP)], axis=0)


def _unstack_heads(o):
    return jnp.concatenate(
        [o[h * Q_BLOCK:(h + 1) * Q_BLOCK, :] for h in range(HEADS_PER_GROUP)], axis=1)


def _slope(g, h):
    return jnp.where(g == 0, float(_ALIBI[0, h]), float(_ALIBI[1, h])).astype(F32)


def _dot_nt(a, b):
    return lax.dot_general(a, b, (((1,), (1,)), ((), ())), preferred_element_type=F32)


def _nsa_cmp_kernel(q_ref, kc_ref, vc_ref, ocmp_ref, sel_ref, flag_ref, *, n_selb):
    g = pl.program_id(1)
    c = pl.program_id(2)
    ncp = kc_ref.shape[2]
    nsp = sel_ref.shape[2]
    q2 = _stack_heads(q_ref[...])
    s_all = _dot_nt(q2, kc_ref[0, 0])
    t = c * Q_BLOCK + lax.broadcasted_iota(jnp.int32, (Q_BLOCK, 1), 0)
    cmp_end = lax.broadcasted_iota(jnp.int32, (1, ncp), 1) * CMP_STRIDE + (CMP_BLOCK - 1)
    dist_i = t - cmp_end
    mask = dist_i >= 0
    dist = dist_i.astype(F32)
    imp = jnp.zeros((Q_BLOCK, ncp), F32)
    ps = []
    for h in range(HEADS_PER_GROUP):
        s = s_all[h * Q_BLOCK:(h + 1) * Q_BLOCK] - _slope(g, h) * dist
        s = jnp.where(mask, s, NEG_INF)
        m = jnp.max(s, axis=-1, keepdims=True)
        e = jnp.where(mask, jnp.exp(s - m), 0.0)
        p = e / jnp.maximum(jnp.sum(e, axis=-1, keepdims=True), TINY)
        imp = imp + p
        ps.append(p.astype(BF16))
    o = jnp.dot(jnp.concatenate(ps, axis=0), vc_ref[0, 0], preferred_element_type=F32)
    ocmp_ref[...] = _unstack_heads(o).astype(ocmp_ref.dtype)

    n_i = lax.broadcasted_iota(jnp.int32, (ncp, nsp), 0)
    j_i = lax.broadcasted_iota(jnp.int32, (ncp, nsp), 1)
    lo = j_i * (SEL_BLOCK // CMP_STRIDE) - (CMP_BLOCK // CMP_STRIDE) + 1
    ovl = ((n_i >= lo) & (n_i < lo + N_OVERLAP)).astype(BF16)
    i_hi = imp.astype(BF16)
    r1 = imp - i_hi.astype(F32)
    i_mid = r1.astype(BF16)
    i_lo = (r1 - i_mid.astype(F32)).astype(BF16)
    imp_sel = (jnp.dot(i_hi, ovl, preferred_element_type=F32)
               + jnp.dot(i_mid, ovl, preferred_element_type=F32)
               + jnp.dot(i_lo, ovl, preferred_element_type=F32))

    blk = lax.broadcasted_iota(jnp.int32, (1, nsp), 1)
    cur = jnp.right_shift(t, SEL_SHIFT)
    valid = blk * SEL_BLOCK <= t
    forced = (blk == 0) | (blk == cur) | (blk == cur - 1)
    score = jnp.where(valid, imp_sel + jnp.where(forced, FORCE_BONUS, 0.0), -1.0)
    score = jnp.where(blk < n_selb, score, -3.0)
    sel = jnp.zeros((Q_BLOCK, nsp), F32)
    for _ in range(min(SEL_TOP_N, n_selb)):
        mx = jnp.max(score, axis=-1, keepdims=True)
        first = jnp.min(jnp.where(score == mx, blk, nsp), axis=-1, keepdims=True)
        pick = blk == first
        sel = jnp.where(pick, 1.0, sel)
        score = jnp.where(pick, -2.0, score)
    sel_ref[0] = sel.astype(sel_ref.dtype)
    flag_ref[0, 0, 0] = jnp.max(sel, axis=0, keepdims=True)


def _nsa_cmp(proj, kcvc, B, S):
    T = B * S
    C = S // Q_BLOCK
    G = N_KV_GROUPS
    ncp = kcvc.shape[2]
    n_selb = S // SEL_BLOCK
    nsp = max(LANES, n_selb)
    gw = HEADS_PER_GROUP * HEAD_DIM
    kern = functools.partial(_nsa_cmp_kernel, n_selb=n_selb)
    return pl.pallas_call(
        kern,
        out_shape=(jax.ShapeDtypeStruct((T, Q_DIM), BF16),
                   jax.ShapeDtypeStruct((G, T, nsp), BF16),
                   jax.ShapeDtypeStruct((B, G, C, 1, nsp), F32)),
        grid=(B, G, C),
        in_specs=[pl.BlockSpec((Q_BLOCK, gw), lambda b, g, c: (b * C + c, g)),
                  pl.BlockSpec((1, 1, ncp, HEAD_DIM), lambda b, g, c: (g, b, 0, 0)),
                  pl.BlockSpec((1, 1, ncp, HEAD_DIM), lambda b, g, c: (G + g, b, 0, 0))],
        out_specs=(pl.BlockSpec((Q_BLOCK, gw), lambda b, g, c: (b * C + c, g)),
                   pl.BlockSpec((1, Q_BLOCK, nsp), lambda b, g, c: (g, b * C + c, 0)),
                   pl.BlockSpec((1, 1, 1, 1, nsp), lambda b, g, c: (b, g, c, 0, 0))),
        compiler_params=_cparams(("arbitrary", "arbitrary", "arbitrary")),
        name="nsa_cmp",
    )(proj, kcvc, kcvc)


SLC_GROUP = 4


def _softmax_tile(state, q2, k_tile, v_tile, mask, dist, g):
    m_old, l_old, acc_old = state
    s_all = _dot_nt(q2, k_tile)
    ps, ms, ls = [], [], []
    for h in range(HEADS_PER_GROUP):
        rows = slice(h * Q_BLOCK, (h + 1) * Q_BLOCK)
        s = s_all[rows] - _slope(g, h) * dist
        s = jnp.where(mask, s, NEG_INF)
        m_new = jnp.maximum(m_old[rows], jnp.max(s, axis=-1, keepdims=True))
        p = jnp.where(mask, jnp.exp(s - m_new), 0.0)
        alpha = jnp.exp(m_old[rows] - m_new)
        ls.append(alpha * l_old[rows] + jnp.sum(p, axis=-1, keepdims=True))
        ms.append(m_new)
        ps.append(p.astype(BF16))
    m_new = jnp.concatenate(ms, axis=0)
    alpha = jnp.exp(m_old - m_new)
    pv = jnp.dot(jnp.concatenate(ps, axis=0), v_tile, preferred_element_type=F32)
    return m_new, jnp.concatenate(ls, axis=0), alpha * acc_old + pv


def _nsa_slc_kernel(nblk_ref, blist_ref, q_ref, ks_ref, vs_ref, kw_ref, vw_ref,
                    sel_ref, gate_ref, ocmp_ref, o_ref):
    b = pl.program_id(0)
    g = pl.program_id(1)
    c = pl.program_id(2)
    n_chunk = pl.num_programs(2)
    nsp = sel_ref.shape[2]
    row = (b * N_KV_GROUPS + g) * n_chunk + c
    n_act = nblk_ref[row]
    q2 = _stack_heads(q_ref[...])
    t = c * Q_BLOCK + lax.broadcasted_iota(jnp.int32, (Q_BLOCK, 1), 0)
    rows = HEADS_PER_GROUP * Q_BLOCK
    init = (jnp.full((rows, 1), NEG_INF, F32), jnp.zeros((rows, 1), F32),
            jnp.zeros((rows, HEAD_DIM), F32))

    wide = SLC_GROUP * SEL_BLOCK
    lane = lax.broadcasted_iota(jnp.int32, (1, wide), 1)
    slot = jnp.right_shift(lane, SEL_SHIFT)
    sel = sel_ref[0]

    def slc_body(i, state):
        js = []
        for k in range(SLC_GROUP):
            pos = i * SLC_GROUP + k
            j = blist_ref[row * nsp + jnp.minimum(pos, nsp - 1)]
            js.append(jnp.where(pos < n_act, j, 0))
        k_tile = jnp.concatenate(
            [ks_ref[pl.ds(pl.multiple_of(j * SEL_BLOCK, SEL_BLOCK), SEL_BLOCK), :] for j in js], axis=0)
        v_tile = jnp.concatenate(
            [vs_ref[pl.ds(pl.multiple_of(j * SEL_BLOCK, SEL_BLOCK), SEL_BLOCK), :] for j in js], axis=0)
        jvec = jnp.zeros((1, wide), jnp.int32)
        for k in range(SLC_GROUP):
            jvec = jnp.where(slot == k, js[k], jvec)
        kpos = jvec * SEL_BLOCK + (lane - slot * SEL_BLOCK)
        slot_ok = (i * SLC_GROUP + slot) < n_act
        onehot = (lax.broadcasted_iota(jnp.int32, (nsp, wide), 0) == jvec).astype(BF16)
        chosen = jnp.dot(sel, onehot, preferred_element_type=F32) > 0.5
        mask = chosen & (kpos <= t) & slot_ok
        dist = (t - kpos).astype(F32)
        return _softmax_tile(state, q2, k_tile, v_tile, mask, dist, g)

    n_steps = (n_act + SLC_GROUP - 1) // SLC_GROUP
    _, l_s, acc_s = lax.fori_loop(0, n_steps, slc_body, init)
    o_slc = acc_s / jnp.maximum(l_s, TINY)

    n_wt = WINDOW // Q_BLOCK + 1
    kl = lax.broadcasted_iota(jnp.int32, (1, Q_BLOCK), 1)

    def win_body(w, state):
        start = pl.multiple_of((c - (n_wt - 1) + w) * Q_BLOCK, Q_BLOCK)
        k_tile = kw_ref[pl.ds(start, Q_BLOCK), :]
        v_tile = vw_ref[pl.ds(start, Q_BLOCK), :]
        dw = t - (start + kl)
        mask = (dw >= 0) & (dw < WINDOW)
        return _softmax_tile(state, q2, k_tile, v_tile, mask, dw.astype(F32), g)

    _, l_w, acc_w = lax.fori_loop(jnp.maximum(n_wt - 1 - c, 0), n_wt, win_body, init)
    o_win = acc_w / jnp.maximum(l_w, TINY)

    gs = jax.nn.sigmoid(gate_ref[...].astype(F32))
    o_cmp = ocmp_ref[...].astype(F32)
    outs = []
    for h in range(HEADS_PER_GROUP):
        hr = slice(h * Q_BLOCK, (h + 1) * Q_BLOCK)
        hc = slice(h * HEAD_DIM, (h + 1) * HEAD_DIM)
        outs.append(gs[:, 3 * h:3 * h + 1] * o_cmp[:, hc]
                    + gs[:, 3 * h + 1:3 * h + 2] * o_slc[hr]
                    + gs[:, 3 * h + 2:3 * h + 3] * o_win[hr])
    o_ref[...] = jnp.concatenate(outs, axis=1).astype(o_ref.dtype)


def _nsa_slc(proj, sel, nblk, blist, ocmp, B, S):
    T = B * S
    C = S // Q_BLOCK
    G = N_KV_GROUPS
    nsp = sel.shape[2]
    gw = HEADS_PER_GROUP * HEAD_DIM
    kvspec = lambda col: pl.BlockSpec(
        (S, HEAD_DIM), lambda b, g, c, *_: (b, col // HEAD_DIM + g))
    grid_spec = pltpu.PrefetchScalarGridSpec(
        num_scalar_prefetch=2,
        grid=(B, G, C),
        in_specs=[pl.BlockSpec((Q_BLOCK, gw), lambda b, g, c, *_: (b * C + c, g)),
                  kvspec(COL_KS), kvspec(COL_VS), kvspec(COL_KW), kvspec(COL_VW),
                  pl.BlockSpec((1, Q_BLOCK, nsp), lambda b, g, c, *_: (g, b * C + c, 0)),
                  pl.BlockSpec((Q_BLOCK, LANES), lambda b, g, c, *_: (b * C + c, COL_GATE // LANES + g)),
                  pl.BlockSpec((Q_BLOCK, gw), lambda b, g, c, *_: (b * C + c, g))],
        out_specs=pl.BlockSpec((Q_BLOCK, gw), lambda b, g, c, *_: (b * C + c, g)),
    )
    return pl.pallas_call(
        _nsa_slc_kernel,
        out_shape=jax.ShapeDtypeStruct((T, Q_DIM), BF16),
        grid_spec=grid_spec,
        compiler_params=_cparams(("arbitrary", "arbitrary", "arbitrary")),
        name="nsa_slc",
    )(nblk, blist, proj, proj, proj, proj, proj, sel, proj, ocmp)


CONV_TS = 512
CONV_HALO = 32
CONV_RC = 64
CONV_CC = 256


def _conv_kernel(a_ref, b_ref, w_ref, cb_ref, g_ref, beta_ref, o_ref, hbuf, ybuf):
    si = pl.program_id(1)

    @pl.when(si == 0)
    def _():
        hbuf[pl.ds(0, CONV_HALO), :] = jnp.zeros((CONV_HALO, CONV_CH), F32)

    @pl.when(si > 0)
    def _():
        hbuf[pl.ds(0, CONV_HALO), :] = hbuf[pl.ds(CONV_TS, CONV_HALO), :]

    a = a_ref[...].astype(F32)
    bb = b_ref[...].astype(F32)
    hbuf[pl.ds(CONV_HALO, CONV_TS), :] = a * jax.nn.sigmoid(bb)

    off = CONV_HALO - (CONV_WIDTH - 1)
    for cc in range(CONV_CH // CONV_CC):
        cols = slice(cc * CONV_CC, (cc + 1) * CONV_CC)
        wts = w_ref[:, cols]

        def row_body(r, carry, cols=cols, wts=wts):
            r0 = pl.multiple_of(r * CONV_RC, CONV_RC)
            win = hbuf[pl.ds(r0, CONV_RC + CONV_HALO), cols]
            acc = jnp.zeros((CONV_RC, CONV_CC), F32)
            for k in range(CONV_WIDTH):
                acc = acc + win[off + k:off + k + CONV_RC, :] * wts[k:k + 1, :]
            ybuf[pl.ds(r0, CONV_RC), cols] = acc
            return carry

        lax.fori_loop(0, CONV_TS // CONV_RC, row_body, 0)

    y = ybuf[...] + cb_ref[...]
    mu = jnp.mean(y, axis=-1, keepdims=True)
    yc = y - mu
    var = jnp.mean(yc * yc, axis=-1, keepdims=True)
    z = yc * lax.rsqrt(var + LN_EPS) * g_ref[...] + beta_ref[...]
    o_ref[...] = (z * jax.nn.sigmoid(z)).astype(o_ref.dtype)


def _conv(proj, conv_w, conv_b, ln_g, ln_b, B, S):
    T = B * S
    ts = CONV_TS
    assert S % ts == 0
    nS = S // ts
    ca = COL_GLU // CONV_CH
    vec = lambda: pl.BlockSpec((1, CONV_CH), lambda b, s: (0, 0))
    return pl.pallas_call(
        _conv_kernel,
        out_shape=jax.ShapeDtypeStruct((T, CONV_CH), BF16),
        grid=(B, nS),
        in_specs=[pl.BlockSpec((ts, CONV_CH), lambda b, s: (b * nS + s, ca)),
                  pl.BlockSpec((ts, CONV_CH), lambda b, s: (b * nS + s, ca + 1)),
                  pl.BlockSpec((CONV_WIDTH, CONV_CH), lambda b, s: (0, 0)),
                  vec(), vec(), vec()],
        out_specs=pl.BlockSpec((ts, CONV_CH), lambda b, s: (b * nS + s, 0)),
        scratch_shapes=[pltpu.VMEM((CONV_HALO + ts, CONV_CH), F32),
                        pltpu.VMEM((ts, CONV_CH), F32)],
        compiler_params=_cparams(("arbitrary", "arbitrary")),
        name="conformer_conv",
    )(proj, proj, conv_w, conv_b.reshape(1, -1), ln_g.reshape(1, -1), ln_b.reshape(1, -1))


def _merge_kernel(o_ref, h_ref, wa_ref, wb_ref, ga_ref, gb_ref, out_ref):
    ya = jnp.dot(o_ref[...], wa_ref[...], preferred_element_type=F32)
    yb = jnp.dot(h_ref[...], wb_ref[...], preferred_element_type=F32)
    ga = jax.nn.sigmoid(ga_ref[...].astype(F32))
    gb = jax.nn.sigmoid(gb_ref[...].astype(F32))
    out_ref[...] = (ga * ya + gb * yb).astype(out_ref.dtype)


def _merge(o_nsa, h_conv, wa, wb, proj):
    T = o_nsa.shape[0]
    tm = min(512, T)
    tn = 1024
    nN = D_MODEL // tn
    ga0 = COL_MERGE // tn
    return pl.pallas_call(
        _merge_kernel,
        out_shape=jax.ShapeDtypeStruct((T, D_MODEL), BF16),
        grid=(T // tm, nN),
        in_specs=[pl.BlockSpec((tm, Q_DIM), lambda i, j: (i, 0)),
                  pl.BlockSpec((tm, CONV_CH), lambda i, j: (i, 0)),
                  pl.BlockSpec((Q_DIM, tn), lambda i, j: (0, j)),
                  pl.BlockSpec((CONV_CH, tn), lambda i, j: (0, j)),
                  pl.BlockSpec((tm, tn), lambda i, j: (i, ga0 + j)),
                  pl.BlockSpec((tm, tn), lambda i, j: (i, ga0 + nN + j))],
        out_specs=pl.BlockSpec((tm, tn), lambda i, j: (i, j)),
        compiler_params=_cparams(("arbitrary", "arbitrary")),
        name="merge",
    )(o_nsa, h_conv, wa, wb, proj, proj)


def _layer_norm_rows(y, g, b):
    mu = jnp.mean(y, axis=-1, keepdims=True)
    yc = y - mu
    var = jnp.mean(yc * yc, axis=-1, keepdims=True)
    return yc * lax.rsqrt(var + LN_EPS) * g + b


def _split2(v):
    hi = v.astype(BF16)
    return hi, (v - hi.astype(F32)).astype(BF16)


def _outproj_kernel(mix_ref, w_ref, x_ref, g_ref, b_ref, rw_hi_ref, rw_lo_ref, rb_ref,
                    x1_ref, x1b_ref, ti_ref, tg_ref):
    m = jnp.dot(mix_ref[...], w_ref[...], preferred_element_type=F32)
    x1 = _layer_norm_rows(DEEPNORM_ALPHA * x_ref[...] + m, g_ref[...], b_ref[...])
    x1_ref[...] = x1
    x1b_ref[...] = x1.astype(BF16)
    x_hi, x_lo = _split2(x1)
    logits = (jnp.dot(x_hi, rw_hi_ref[...], preferred_element_type=F32)
              + jnp.dot(x_hi, rw_lo_ref[...], preferred_element_type=F32)
              + jnp.dot(x_lo, rw_hi_ref[...], preferred_element_type=F32)) + rb_ref[...]
    lane = lax.broadcasted_iota(jnp.int32, logits.shape, 1)
    logits = jnp.where(lane < N_EXPERTS, logits, -jnp.inf)
    ti = jnp.zeros(logits.shape, jnp.int32)
    tv = jnp.zeros(logits.shape, F32)
    top0 = None
    den = jnp.zeros((logits.shape[0], 1), F32)
    for k in range(TOP_K):
        mx = jnp.max(logits, axis=-1, keepdims=True)
        idx = jnp.min(jnp.where(logits == mx, lane, LANES), axis=-1, keepdims=True)
        if top0 is None:
            top0 = mx
        e = jnp.exp(mx - top0)
        den = den + e
        ti = jnp.where(lane == k, idx, ti)
        tv = jnp.where(lane == k, e, tv)
        logits = jnp.where(lane == idx, -jnp.inf, logits)
    ti_ref[...] = ti
    tg_ref[...] = tv / den


def _outproj(mix, w_out, x2d, g, b, rw_hi, rw_lo, rb):
    T = mix.shape[0]
    tm = min(256, T)
    full = lambda shape: pl.BlockSpec(shape, lambda i: (0, 0))
    rowb = lambda w: pl.BlockSpec((tm, w), lambda i: (i, 0))
    return pl.pallas_call(
        _outproj_kernel,
        out_shape=(jax.ShapeDtypeStruct((T, D_MODEL), F32),
                   jax.ShapeDtypeStruct((T, D_MODEL), BF16),
                   jax.ShapeDtypeStruct((T, LANES), jnp.int32),
                   jax.ShapeDtypeStruct((T, LANES), F32)),
        grid=(T // tm,),
        in_specs=[rowb(D_MODEL), full((D_MODEL, D_MODEL)), rowb(D_MODEL),
                  full((1, D_MODEL)), full((1, D_MODEL)),
                  full((D_MODEL, LANES)), full((D_MODEL, LANES)), full((1, LANES))],
        out_specs=(rowb(D_MODEL), rowb(D_MODEL), rowb(LANES), rowb(LANES)),
        compiler_params=_cparams(("arbitrary",)),
        name="outproj_ln_router",
    )(mix, w_out, x2d, g, b, rw_hi, rw_lo, rb)


def _moe_up_kernel(be_ref, nused_ref, x_ref, wg_ref, wu_ref, bg_ref, bu_ref, h_ref,
                   wgb_ref, wub_ref):
    r = pl.program_id(1)
    prev = be_ref[jnp.maximum(r - 1, 0)]
    fresh = (r == 0) | (be_ref[r] != prev)

    @pl.when(fresh)
    def _():
        wgb_ref[...] = wg_ref[0].astype(BF16)
        wub_ref[...] = wu_ref[0].astype(BF16)

    @pl.when(r < nused_ref[0])
    def _():
        x = x_ref[...]
        gt = jnp.dot(x, wgb_ref[...], preferred_element_type=F32) + bg_ref[0]
        up = jnp.dot(x, wub_ref[...], preferred_element_type=F32) + bu_ref[0]
        gt = jnp.minimum(gt, SWIGLU_LIMIT)
        up = jnp.clip(up, -SWIGLU_LIMIT, SWIGLU_LIMIT)
        h = gt * jax.nn.sigmoid(SWIGLU_ALPHA * gt) * (up + 1.0)
        h_ref[...] = h.astype(h_ref.dtype)


def _moe_up(blk_e, n_used, xs, w_gate, w_up, b_gate, b_up):
    P = xs.shape[0]
    bm, tf = MOE_BLOCK, MOE_FF_TILE
    nblk = P // bm
    grid_spec = pltpu.PrefetchScalarGridSpec(
        num_scalar_prefetch=2,
        grid=(D_FF // tf, nblk),
        in_specs=[pl.BlockSpec((bm, D_MODEL), lambda f, r, be, nu: (r, 0)),
                  pl.BlockSpec((1, D_MODEL, tf), lambda f, r, be, nu: (be[r], 0, f)),
                  pl.BlockSpec((1, D_MODEL, tf), lambda f, r, be, nu: (be[r], 0, f)),
                  pl.BlockSpec((1, 1, tf), lambda f, r, be, nu: (be[r], 0, f)),
                  pl.BlockSpec((1, 1, tf), lambda f, r, be, nu: (be[r], 0, f))],
        out_specs=pl.BlockSpec((bm, tf), lambda f, r, be, nu: (r, f)),
        scratch_shapes=[pltpu.VMEM((D_MODEL, tf), BF16), pltpu.VMEM((D_MODEL, tf), BF16)],
    )
    return pl.pallas_call(
        _moe_up_kernel,
        out_shape=jax.ShapeDtypeStruct((P, D_FF), BF16),
        grid_spec=grid_spec,
        compiler_params=_cparams(("arbitrary", "arbitrary")),
        name="moe_up",
    )(blk_e, n_used, xs, w_gate, w_up, b_gate.reshape(N_EXPERTS, 1, D_FF),
      b_up.reshape(N_EXPERTS, 1, D_FF))


def _moe_down_kernel(be_ref, nused_ref, h_ref, wd_ref, bd_ref, rw_ref, o_ref, wdb_ref):
    r = pl.program_id(1)
    prev = be_ref[jnp.maximum(r - 1, 0)]
    fresh = (r == 0) | (be_ref[r] != prev)

    @pl.when(fresh)
    def _():
        wdb_ref[...] = wd_ref[0].astype(BF16)

    @pl.when(r < nused_ref[0])
    def _():
        y = jnp.dot(h_ref[...], wdb_ref[...], preferred_element_type=F32) + bd_ref[0]
        o_ref[...] = (y * rw_ref[...]).astype(o_ref.dtype)


def _moe_down(blk_e, n_used, h, w_down, b_down, row_w):
    P = h.shape[0]
    bm, tn = MOE_BLOCK, MOE_FF_TILE
    nblk = P // bm
    grid_spec = pltpu.PrefetchScalarGridSpec(
        num_scalar_prefetch=2,
        grid=(D_MODEL // tn, nblk),
        in_specs=[pl.BlockSpec((bm, D_FF), lambda n, r, be, nu: (r, 0)),
                  pl.BlockSpec((1, D_FF, tn), lambda n, r, be, nu: (be[r], 0, n)),
                  pl.BlockSpec((1, 1, tn), lambda n, r, be, nu: (be[r], 0, n)),
                  pl.BlockSpec((bm, 1), lambda n, r, be, nu: (r, 0))],
        out_specs=pl.BlockSpec((bm, tn), lambda n, r, be, nu: (r, n)),
        scratch_shapes=[pltpu.VMEM((D_FF, tn), BF16)],
    )
    return pl.pallas_call(
        _moe_down_kernel,
        out_shape=jax.ShapeDtypeStruct((P, D_MODEL), BF16),
        grid_spec=grid_spec,
        compiler_params=_cparams(("arbitrary", "arbitrary")),
        name="moe_down",
    )(blk_e, n_used, h, w_down, b_down.reshape(N_EXPERTS, 1, D_MODEL), row_w)


def _final_kernel(x1_ref, y_ref, g_ref, b_ref, o_ref):
    f = y_ref[:, :D_MODEL].astype(F32)
    for k in range(1, TOP_K):
        f = f + y_ref[:, k * D_MODEL:(k + 1) * D_MODEL].astype(F32)
    o_ref[...] = _layer_norm_rows(DEEPNORM_ALPHA * x1_ref[...] + f, g_ref[...], b_ref[...])


def _final(x1, y4, g, b):
    T = x1.shape[0]
    tm = min(256, T)
    return pl.pallas_call(
        _final_kernel,
        out_shape=jax.ShapeDtypeStruct((T, D_MODEL), F32),
        grid=(T // tm,),
        in_specs=[pl.BlockSpec((tm, D_MODEL), lambda i: (i, 0)),
                  pl.BlockSpec((tm, TOP_K * D_MODEL), lambda i: (i, 0)),
                  pl.BlockSpec((1, D_MODEL), lambda i: (0, 0)),
                  pl.BlockSpec((1, D_MODEL), lambda i: (0, 0))],
        out_specs=pl.BlockSpec((tm, D_MODEL), lambda i: (i, 0)),
        compiler_params=_cparams(("arbitrary",)),
        name="combine_ln",
    )(x1, y4, g, b)


def _prep_w_in(w_in):
    splits = np.cumsum([Q_DIM] + [KV_DIM] * 6 + [NSA_GATE_DIM, 2 * CONV_CH, 2 * D_MODEL])
    q = w_in[:, :splits[0]] * (1.0 / math.sqrt(HEAD_DIM))
    kv = w_in[:, splits[0]:splits[6]]
    gate = w_in[:, splits[6]:splits[7]]
    glu = w_in[:, splits[7]:splits[8]]
    merge = w_in[:, splits[8]:splits[9]]
    per_g = HEADS_PER_GROUP * 3
    gate_tiles = [jnp.pad(gate[:, g * per_g:(g + 1) * per_g], ((0, 0), (0, LANES - per_g)))
                  for g in range(N_KV_GROUPS)]
    return jnp.concatenate([q, glu, merge, kv] + gate_tiles, axis=1).astype(BF16)


def _mixer(x2d, B, S, w_in, cmp_pe, cmp_w1, cmp_b1, cmp_w2, cmp_b2, w_nsa_proj,
           conv_w, conv_b, conv_ln_g, conv_ln_b, w_conv_proj):
    T = B * S
    G = N_KV_GROUPS
    proj = _inproj(x2d, _prep_w_in(w_in))

    n16 = S // CMP_STRIDE
    slots = [proj[:, COL_KC + s * HEAD_DIM: COL_KC + (s + 1) * HEAD_DIM]
             .reshape(B, n16, CMP_STRIDE * HEAD_DIM) for s in range(2 * G)]
    xs = jnp.stack(slots, axis=0)
    kcvc = _compress(xs, cmp_pe.reshape(2, 1, CMP_BLOCK * HEAD_DIM),
                     cmp_w1.astype(BF16), cmp_b1.reshape(2, 1, HEAD_DIM),
                     cmp_w2.astype(BF16), cmp_b2.reshape(2, 1, HEAD_DIM))

    ocmp, sel, flags = _nsa_cmp(proj, kcvc, B, S)
    nsp = sel.shape[2]
    fl = flags.reshape(-1, nsp) > 0.5
    nblk = jnp.sum(fl, axis=1).astype(jnp.int32)
    blist = jnp.argsort(jnp.logical_not(fl), axis=1, stable=True).astype(jnp.int32)
    o_nsa = _nsa_slc(proj, sel, nblk, blist.reshape(-1), ocmp, B, S)

    h_conv = _conv(proj, conv_w, conv_b, conv_ln_g, conv_ln_b, B, S)
    return _merge(o_nsa, h_conv, w_nsa_proj.astype(BF16), w_conv_proj.astype(BF16), proj)


def _moe(x1, x1b, top_i, top_g, w_gate, b_gate, w_up, b_up, w_down, b_down):
    T = x1.shape[0]
    A = T * TOP_K
    bm = MOE_BLOCK
    flat_e = top_i[:, :TOP_K].reshape(A)
    flat_w = top_g[:, :TOP_K].reshape(A)
    onehot = (flat_e[:, None] == jnp.arange(N_EXPERTS, dtype=jnp.int32)[None, :]).astype(jnp.int32)
    csum = jnp.cumsum(onehot, axis=0)
    rank = jnp.sum(onehot * csum, axis=1) - 1
    counts = csum[-1]
    padded = (counts + bm - 1) // bm * bm
    pad_end = jnp.cumsum(padded)
    pad_start = pad_end - padded
    dest = pad_start[flat_e] + rank
    P = -(-A // bm) * bm + N_EXPERTS * bm
    nblk = P // bm
    flat_tok = jnp.arange(A, dtype=jnp.int32) // TOP_K
    buf_tok = jnp.zeros((P,), jnp.int32).at[dest].set(flat_tok)
    buf_w = jnp.zeros((P,), F32).at[dest].set(flat_w)
    blk_start = jnp.arange(nblk, dtype=jnp.int32) * bm
    blk_e = jnp.minimum(jnp.sum(blk_start[:, None] >= pad_end[None, :], axis=1),
                        N_EXPERTS - 1).astype(jnp.int32)
    n_used = (pad_end[-1] // bm).astype(jnp.int32).reshape(1)
    xs = x1b[buf_tok]
    h = _moe_up(blk_e, n_used, xs, w_gate, w_up, b_gate, b_up)
    out = _moe_down(blk_e, n_used, h, w_down, b_down, buf_w.reshape(P, 1))
    return out[dest].reshape(T, TOP_K * D_MODEL)


def kernel(x, w_in, cmp_pe, cmp_w1, cmp_b1, cmp_w2, cmp_b2, w_nsa_proj, conv_w, conv_b, conv_ln_g, conv_ln_b, w_conv_proj, w_out, ln1_g, ln1_b, router_w, router_b, w_gate, b_gate, w_up, b_up, w_down, b_down, ln2_g, ln2_b):
    B, S, D = x.shape
    T = B * S
    x2d = x.reshape(T, D)
    for l in range(DEPTH):
        mix = _mixer(x2d, B, S, w_in[l], cmp_pe[l], cmp_w1[l], cmp_b1[l], cmp_w2[l], cmp_b2[l],
                     w_nsa_proj[l], conv_w[l], conv_b[l], conv_ln_g[l], conv_ln_b[l],
                     w_conv_proj[l])
        rw = jnp.pad(router_w[l], ((0, 0), (0, LANES - N_EXPERTS)))
        rw_hi = rw.astype(BF16)
        rw_lo = (rw - rw_hi.astype(F32)).astype(BF16)
        rb = jnp.pad(router_b[l], (0, LANES - N_EXPERTS)).reshape(1, LANES)
        x1, x1b, top_i, top_g = _outproj(
            mix, w_out[l].astype(BF16), x2d, ln1_g[l].reshape(1, D), ln1_b[l].reshape(1, D),
            rw_hi, rw_lo, rb)
        y4 = _moe(x1, x1b, top_i, top_g, w_gate[l], b_gate[l], w_up[l], b_up[l],
                  w_down[l], b_down[l])
        x2d = _final(x1, y4, ln2_g[l].reshape(1, D), ln2_b[l].reshape(1, D))
    return x2d.reshape(B, S, D)
```

```python
import functools
import math

import numpy as np
import jax
import jax.numpy as jnp
from jax import lax
from jax.experimental import pallas as pl
from jax.experimental.pallas import tpu as pltpu

D_MODEL = 2048
N_HEADS = 16
N_KV_GROUPS = 2
HEADS_PER_GROUP = N_HEADS // N_KV_GROUPS
HEAD_DIM = 128
CMP_BLOCK = 32
CMP_STRIDE = 16
SEL_BLOCK = 64
SEL_SHIFT = 6
SEL_TOP_N = 16
WINDOW = 512
Q_BLOCK = 128
N_OVERLAP = (SEL_BLOCK + CMP_BLOCK) // CMP_STRIDE - 1
FORCE_BONUS = 1.0e4
CONV_CH = D_MODEL // 2
CONV_WIDTH = 31
N_EXPERTS = 32
TOP_K = 4
D_FF = D_MODEL
SWIGLU_LIMIT = 7.0
SWIGLU_ALPHA = 1.702
LN_EPS = 1e-5
DEPTH = 1
DEEPNORM_ALPHA = (2 * DEPTH) ** 0.25
NEG_INF = -1e30
TINY = 1e-30

Q_DIM = N_HEADS * HEAD_DIM
KV_DIM = N_KV_GROUPS * HEAD_DIM
NSA_GATE_DIM = N_HEADS * 3

LANES = 128
SUBLANES = 8
VMEM_LIMIT = 56 * 1024 * 1024

COL_Q = 0
COL_GLU = COL_Q + Q_DIM
COL_MERGE = COL_GLU + 2 * CONV_CH
COL_KC = COL_MERGE + 2 * D_MODEL
COL_VC = COL_KC + KV_DIM
COL_KS = COL_VC + KV_DIM
COL_VS = COL_KS + KV_DIM
COL_KW = COL_VS + KV_DIM
COL_VW = COL_KW + KV_DIM
COL_GATE = COL_VW + KV_DIM
PROJ_W = COL_GATE + N_KV_GROUPS * LANES

BF16 = jnp.bfloat16
F32 = jnp.float32

MOE_BLOCK = 256
MOE_UP_TILE = 1024

_ALIBI = np.exp2(-8.0 * np.arange(1, N_HEADS + 1, dtype=np.float32) / N_HEADS).astype(np.float32)
_ALIBI = _ALIBI.reshape(N_KV_GROUPS, HEADS_PER_GROUP)


def _cparams(sem, vmem=VMEM_LIMIT):
    return pltpu.CompilerParams(dimension_semantics=sem, vmem_limit_bytes=vmem)


def _inproj_kernel(x_ref, w_ref, o_ref, xb_ref):
    @pl.when(pl.program_id(1) == 0)
    def _():
        xb_ref[...] = x_ref[...].astype(BF16)

    o_ref[...] = jnp.dot(xb_ref[...], w_ref[...],
                         preferred_element_type=F32).astype(o_ref.dtype)


def _inproj(x2d, w_p):
    T, D = x2d.shape
    N = w_p.shape[1]
    tm = min(1024, T)
    tn = 768
    assert T % tm == 0 and N % tn == 0
    return pl.pallas_call(
        _inproj_kernel,
        out_shape=jax.ShapeDtypeStruct((T, N), BF16),
        grid=(T // tm, N // tn),
        in_specs=[pl.BlockSpec((tm, D), lambda i, j: (i, 0)),
                  pl.BlockSpec((D, tn), lambda i, j: (0, j))],
        out_specs=pl.BlockSpec((tm, tn), lambda i, j: (i, j)),
        scratch_shapes=[pltpu.VMEM((tm, D), BF16)],
        compiler_params=_cparams(("arbitrary", "arbitrary")),
        name="inproj",
    )(x2d, w_p)


def _gelu_tanh(x):
    c = math.sqrt(2.0 / math.pi)
    return 0.5 * x * (1.0 + jnp.tanh(c * (x + 0.044715 * (x * x * x))))


def _compress_kernel(x_ref, pe_ref, w1_ref, b1_ref, w2_ref, b2_ref, o_ref, xf_ref):
    S = x_ref.shape[0]
    n16 = S // CMP_STRIDE
    xf_ref[...] = x_ref[...].astype(F32)
    top = jnp.zeros((n16, HEAD_DIM), F32)
    bot = jnp.zeros((n16, HEAD_DIM), F32)
    for j in range(CMP_STRIDE):
        xj = xf_ref[pl.ds(j, n16, stride=CMP_STRIDE), :]
        lo, hi = j * HEAD_DIM, (CMP_STRIDE + j) * HEAD_DIM
        top = top + jnp.dot((xj + pe_ref[0, :, lo:lo + HEAD_DIM]).astype(BF16),
                            w1_ref[0, lo:lo + HEAD_DIM, :], preferred_element_type=F32)
        bot = bot + jnp.dot((xj + pe_ref[0, :, hi:hi + HEAD_DIM]).astype(BF16),
                            w1_ref[0, hi:hi + HEAD_DIM, :], preferred_element_type=F32)
    pre = top + pltpu.roll(bot, n16 - 1, 0) + b1_ref[0]
    h = _gelu_tanh(pre)
    o = jnp.dot(h.astype(BF16), w2_ref[0], preferred_element_type=F32) + b2_ref[0]
    o_ref[0, 0] = o.astype(o_ref.dtype)


def _compress(proj, pe, w1, b1, w2, b2, B, S):
    n_slot = 2 * N_KV_GROUPS
    n16 = S // CMP_STRIDE
    wide = CMP_BLOCK * HEAD_DIM
    kv = lambda s, b: (s // N_KV_GROUPS, 0, 0)
    return pl.pallas_call(
        _compress_kernel,
        out_shape=jax.ShapeDtypeStruct((n_slot, B, n16, HEAD_DIM), BF16),
        grid=(n_slot, B),
        in_specs=[pl.BlockSpec((S, HEAD_DIM), lambda s, b: (b, COL_KC // HEAD_DIM + s)),
                  pl.BlockSpec((1, 1, wide), kv),
                  pl.BlockSpec((1, wide, HEAD_DIM), kv),
                  pl.BlockSpec((1, 1, HEAD_DIM), kv),
                  pl.BlockSpec((1, HEAD_DIM, HEAD_DIM), kv),
                  pl.BlockSpec((1, 1, HEAD_DIM), kv)],
        out_specs=pl.BlockSpec((1, 1, n16, HEAD_DIM), lambda s, b: (s, b, 0, 0)),
        scratch_shapes=[pltpu.VMEM((S, HEAD_DIM), F32)],
        compiler_params=_cparams(("arbitrary", "arbitrary")),
        name="compress",
    )(proj, pe, w1, b1, w2, b2)


LOG2E = 1.4426950408889634
KAUG = 2 * HEAD_DIM
ROW_ALIBI = 0
ROW_SLOT = 16
SLC_GROUP = 4
GROUP_W = HEADS_PER_GROUP * Q_BLOCK


def _alibi_rows():
    s2 = jnp.asarray(_ALIBI * np.float32(LOG2E), F32)
    hi = s2.astype(BF16)
    r1 = s2 - hi.astype(F32)
    mid = r1.astype(BF16)
    lo = (r1 - mid.astype(F32)).astype(BF16)
    trip = jnp.stack([hi, mid, lo, hi, mid, lo], axis=1)
    rows = jnp.pad(trip, ((0, 0), (0, 16 - 6), (0, 0)))
    return jnp.repeat(rows, Q_BLOCK, axis=2)


def _dot_ta(a, b):
    return lax.dot_general(a, b, (((0,), (0,)), ((), ())), preferred_element_type=F32)


def _fill_qaug(qaug, qT_ref, alibi_ref):
    qaug[pl.ds(0, HEAD_DIM), :] = qT_ref[0, 0]
    qaug[pl.ds(HEAD_DIM, 16), :] = alibi_ref[0]
    qaug[pl.ds(HEAD_DIM + 16, HEAD_DIM - 16), :] = jnp.zeros((HEAD_DIM - 16, GROUP_W), BF16)


def _pos_cols(a_val, r_val, lane, slot_lane=None):
    base = jnp.where((lane >= ROW_ALIBI + 3) & (lane < ROW_ALIBI + 6), r_val, 0.0)
    if slot_lane is not None:
        base = jnp.where(lane == slot_lane, 1.0, base)
    return jnp.where(lane < ROW_ALIBI + 3, a_val, base)


def _nsa_cmp_kernel(qT_ref, kc_ref, vc_ref, alibi_ref, ocmpT_ref, selb_ref, qaug, *, n_selb):
    c = pl.program_id(2)
    ncp = kc_ref.shape[2]
    nsp = selb_ref.shape[2]
    _fill_qaug(qaug, qT_ref, alibi_ref)

    n_i = lax.broadcasted_iota(jnp.int32, (ncp, LANES), 0)
    lane = lax.broadcasted_iota(jnp.int32, (ncp, LANES), 1)
    end_rel = n_i * CMP_STRIDE + (CMP_BLOCK - 1) - c * Q_BLOCK
    a_val = jnp.left_shift(jnp.right_shift(end_rel, SEL_SHIFT), SEL_SHIFT).astype(F32)
    r_val = (end_rel & (SEL_BLOCK - 1)).astype(F32)
    pc = _pos_cols(a_val, r_val, lane).astype(BF16)
    s = jnp.dot(jnp.concatenate([kc_ref[0, 0], pc], axis=1), qaug[...],
                preferred_element_type=F32)

    tq = lax.broadcasted_iota(jnp.int32, (1, Q_BLOCK), 1)
    t_row = c * Q_BLOCK + tq
    n_col = lax.broadcasted_iota(jnp.int32, (ncp, Q_BLOCK), 0)
    mask_bias = jnp.where(n_col * CMP_STRIDE + (CMP_BLOCK - 1) <= t_row, 0.0, NEG_INF)
    has_valid = t_row >= CMP_BLOCK - 1

    imp = jnp.zeros((ncp, Q_BLOCK), F32)
    es, rls = [], []
    for h in range(HEADS_PER_GROUP):
        sh = s[:, h * Q_BLOCK:(h + 1) * Q_BLOCK] + mask_bias
        m = jnp.max(sh, axis=0, keepdims=True)
        e = jnp.exp2(sh - m)
        l = jnp.sum(e, axis=0, keepdims=True)
        rl = jnp.where(has_valid, 1.0 / jnp.maximum(l, TINY), 0.0)
        imp = imp + e * rl
        es.append(e.astype(BF16))
        rls.append(rl)
    oT = _dot_ta(vc_ref[0, 0], jnp.concatenate(es, axis=1))
    ocmpT_ref[0, 0] = oT * jnp.concatenate(rls, axis=1)

    j_i = lax.broadcasted_iota(jnp.int32, (nsp, ncp), 0)
    n_j = lax.broadcasted_iota(jnp.int32, (nsp, ncp), 1)
    lo = j_i * (SEL_BLOCK // CMP_STRIDE) - (CMP_BLOCK // CMP_STRIDE) + 1
    ovl = ((n_j >= lo) & (n_j < lo + N_OVERLAP)).astype(BF16)
    i_hi = imp.astype(BF16)
    r1 = imp - i_hi.astype(F32)
    i_mid = r1.astype(BF16)
    i_lo = (r1 - i_mid.astype(F32)).astype(BF16)
    imp_sel = (jnp.dot(ovl, i_hi, preferred_element_type=F32)
               + jnp.dot(ovl, i_mid, preferred_element_type=F32)
               + jnp.dot(ovl, i_lo, preferred_element_type=F32))

    blk = lax.broadcasted_iota(jnp.int32, (nsp, Q_BLOCK), 0)
    cur = jnp.right_shift(t_row, SEL_SHIFT)
    valid = blk * SEL_BLOCK <= t_row
    forced = (blk == 0) | (blk == cur) | (blk == cur - 1)
    score = jnp.where(valid, imp_sel + jnp.where(forced, FORCE_BONUS, 0.0), -1.0)
    score = jnp.where(blk < n_selb, score, -3.0)
    selb = jnp.full((nsp, Q_BLOCK), NEG_INF, F32)
    for _ in range(min(SEL_TOP_N, n_selb)):
        mx = jnp.max(score, axis=0, keepdims=True)
        first = jnp.min(jnp.where(score == mx, blk, nsp), axis=0, keepdims=True)
        pick = blk == first
        selb = jnp.where(pick, 0.0, selb)
        score = jnp.where(pick, -2.0, score)
    selb_ref[0, 0] = selb


def _nsa_cmp(qT, kcvc, alibi, B, S):
    C = S // Q_BLOCK
    G = N_KV_GROUPS
    ncp = kcvc.shape[2]
    n_selb = S // SEL_BLOCK
    nsp = max(LANES, n_selb)
    kern = functools.partial(_nsa_cmp_kernel, n_selb=n_selb)
    return pl.pallas_call(
        kern,
        out_shape=(jax.ShapeDtypeStruct((B * C, G, HEAD_DIM, GROUP_W), F32),
                   jax.ShapeDtypeStruct((B * C, G, nsp, Q_BLOCK), F32)),
        grid=(B, G, C),
        in_specs=[pl.BlockSpec((1, 1, HEAD_DIM, GROUP_W), lambda b, g, c: (b * C + c, g, 0, 0)),
                  pl.BlockSpec((1, 1, ncp, HEAD_DIM), lambda b, g, c: (g, b, 0, 0)),
                  pl.BlockSpec((1, 1, ncp, HEAD_DIM), lambda b, g, c: (G + g, b, 0, 0)),
                  pl.BlockSpec((1, 16, GROUP_W), lambda b, g, c: (g, 0, 0))],
        out_specs=(pl.BlockSpec((1, 1, HEAD_DIM, GROUP_W), lambda b, g, c: (b * C + c, g, 0, 0)),
                   pl.BlockSpec((1, 1, nsp, Q_BLOCK), lambda b, g, c: (b * C + c, g, 0, 0))),
        scratch_shapes=[pltpu.VMEM((KAUG, GROUP_W), BF16)],
        compiler_params=_cparams(("arbitrary", "arbitrary", "arbitrary")),
        name="nsa_cmp",
    )(qT, kcvc, kcvc, alibi)


def _softmax_tile(st, qaug, k_tile, pc, v_tile, bias, first):
    m_ref, l_ref, acc_ref = st
    kaug = jnp.concatenate([k_tile, pc.astype(BF16)], axis=1)
    s = jnp.dot(kaug, qaug[...], preferred_element_type=F32)
    if bias is not None:
        s = jnp.concatenate([s[:, h * Q_BLOCK:(h + 1) * Q_BLOCK] + bias
                             for h in range(HEADS_PER_GROUP)], axis=1)
    mx = jnp.max(s, axis=0, keepdims=True)
    if first:
        m_new = mx
    else:
        m_old = m_ref[pl.ds(0, 1), :]
        m_new = jnp.maximum(m_old, mx)
    p = jnp.exp2(s - m_new)
    ps = jnp.sum(p, axis=0, keepdims=True)
    pv = _dot_ta(v_tile, p.astype(BF16))
    if first:
        l_ref[pl.ds(0, 1), :] = ps
        acc_ref[...] = pv
    else:
        alpha = jnp.exp2(m_old - m_new)
        l_ref[pl.ds(0, 1), :] = alpha * l_ref[pl.ds(0, 1), :] + ps
        acc_ref[...] = alpha * acc_ref[...] + pv
    m_ref[pl.ds(0, 1), :] = m_new


def _nsa_slc_kernel(nblk_ref, blist_ref, qT_ref, ks_ref, vs_ref, kw_ref, vw_ref,
                    selb_ref, gate_ref, ocmpT_ref, alibi_ref, o_ref,
                    qaug, m_ref, l_ref, acc_ref):
    b = pl.program_id(0)
    g = pl.program_id(1)
    c = pl.program_id(2)
    n_chunk = pl.num_programs(2)
    nsp = selb_ref.shape[2]
    row = (b * N_KV_GROUPS + g) * n_chunk + c
    n_act = nblk_ref[row]
    st = (m_ref, l_ref, acc_ref)
    _fill_qaug(qaug, qT_ref, alibi_ref)

    lane = lax.broadcasted_iota(jnp.int32, (SEL_BLOCK, LANES), 1)
    r_val = lax.broadcasted_iota(jnp.int32, (SEL_BLOCK, LANES), 0).astype(F32)
    ki = lax.broadcasted_iota(jnp.int32, (Q_BLOCK, Q_BLOCK), 0)
    qi = lax.broadcasted_iota(jnp.int32, (Q_BLOCK, Q_BLOCK), 1)
    causal_bias = jnp.where(ki <= qi, 0.0, NEG_INF)
    tail_bias = jnp.where(ki > qi, 0.0, NEG_INF)

    def pair_cols(first_block_rel):
        return jnp.concatenate(
            [_pos_cols(float(SEL_BLOCK * (first_block_rel + i)), r_val, lane) for i in range(2)], axis=0)

    def finish():
        return acc_ref[...] / jnp.maximum(l_ref[pl.ds(0, 1), :], TINY)

    n_wt = WINDOW // Q_BLOCK
    base = pl.multiple_of(c * Q_BLOCK, Q_BLOCK)
    _softmax_tile(st, qaug, kw_ref[pl.ds(base, Q_BLOCK), :], pair_cols(0),
                  vw_ref[pl.ds(base, Q_BLOCK), :], causal_bias, True)
    for w in range(n_wt):
        @pl.when(c - n_wt + w >= 0)
        def _(w=w):
            start = pl.multiple_of((c - n_wt + w) * Q_BLOCK, Q_BLOCK)
            _softmax_tile(st, qaug, kw_ref[pl.ds(start, Q_BLOCK), :], pair_cols(2 * (w - n_wt)),
                          vw_ref[pl.ds(start, Q_BLOCK), :], tail_bias if w == 0 else None, False)
    o_winT = finish()

    _softmax_tile(st, qaug, ks_ref[pl.ds(base, Q_BLOCK), :], pair_cols(0),
                  vs_ref[pl.ds(base, Q_BLOCK), :], causal_bias, True)
    sub = lax.broadcasted_iota(jnp.int32, (16, GROUP_W), 0)

    def slc_body(i, carry):
        ks_t, vs_t, pcs = [], [], []
        slot_rows = jnp.zeros((16, GROUP_W), F32)
        for k in range(SLC_GROUP):
            pos = i * SLC_GROUP + k
            live = pos < n_act
            j = jnp.where(live, blist_ref[row * nsp + jnp.minimum(pos, nsp - 1)], 0)
            off = pl.multiple_of(j * SEL_BLOCK, SEL_BLOCK)
            ks_t.append(ks_ref[pl.ds(off, SEL_BLOCK), :])
            vs_t.append(vs_ref[pl.ds(off, SEL_BLOCK), :])
            a_val = ((j - 2 * c) * SEL_BLOCK).astype(F32)
            pcs.append(_pos_cols(a_val, r_val, lane, slot_lane=ROW_SLOT + k))
            rowk = selb_ref[0, 0, pl.ds(j, 1), :]
            rowk = jnp.where(live, rowk, NEG_INF)
            slot_rows = jnp.where(sub == k, jnp.tile(rowk, (1, HEADS_PER_GROUP)), slot_rows)
        qaug[pl.ds(HEAD_DIM + ROW_SLOT, 16), :] = slot_rows.astype(BF16)
        _softmax_tile(st, qaug, jnp.concatenate(ks_t, axis=0), jnp.concatenate(pcs, axis=0),
                      jnp.concatenate(vs_t, axis=0), None, False)
        return carry

    lax.fori_loop(0, (n_act + SLC_GROUP - 1) // SLC_GROUP, slc_body, 0)
    o_slcT = finish()

    gT = jax.nn.sigmoid(gate_ref[...].astype(F32).T)
    o_cmpT = ocmpT_ref[0, 0]
    outs = []
    for h in range(HEADS_PER_GROUP):
        hs = slice(h * Q_BLOCK, (h + 1) * Q_BLOCK)
        oT = (gT[3 * h:3 * h + 1, :] * o_cmpT[:, hs] + gT[3 * h + 1:3 * h + 2, :] * o_slcT[:, hs]
              + gT[3 * h + 2:3 * h + 3, :] * o_winT[:, hs])
        outs.append(oT.T)
    o_ref[...] = jnp.concatenate(outs, axis=1).astype(o_ref.dtype)


def _nsa_slc(proj, qT, selb, nblk, blist, ocmpT, alibi, B, S):
    T = B * S
    C = S // Q_BLOCK
    G = N_KV_GROUPS
    nsp = selb.shape[2]
    gw = HEADS_PER_GROUP * HEAD_DIM
    kvspec = lambda col: pl.BlockSpec(
        (S, HEAD_DIM), lambda b, g, c, *_: (b, col // HEAD_DIM + g))
    tspec = lambda rows: pl.BlockSpec((1, 1, rows, GROUP_W), lambda b, g, c, *_: (b * C + c, g, 0, 0))
    grid_spec = pltpu.PrefetchScalarGridSpec(
        num_scalar_prefetch=2,
        grid=(B, G, C),
        in_specs=[tspec(HEAD_DIM),
                  kvspec(COL_KS), kvspec(COL_VS), kvspec(COL_KW), kvspec(COL_VW),
                  pl.BlockSpec((1, 1, nsp, Q_BLOCK), lambda b, g, c, *_: (b * C + c, g, 0, 0)),
                  pl.BlockSpec((Q_BLOCK, LANES), lambda b, g, c, *_: (b * C + c, COL_GATE // LANES + g)),
                  tspec(HEAD_DIM),
                  pl.BlockSpec((1, 16, GROUP_W), lambda b, g, c, *_: (g, 0, 0))],
        out_specs=pl.BlockSpec((Q_BLOCK, gw), lambda b, g, c, *_: (b * C + c, g)),
        scratch_shapes=[pltpu.VMEM((KAUG, GROUP_W), BF16),
                        pltpu.VMEM((SUBLANES, GROUP_W), F32),
                        pltpu.VMEM((SUBLANES, GROUP_W), F32),
                        pltpu.VMEM((HEAD_DIM, GROUP_W), F32)],
    )
    return pl.pallas_call(
        _nsa_slc_kernel,
        out_shape=jax.ShapeDtypeStruct((T, Q_DIM), BF16),
        grid_spec=grid_spec,
        compiler_params=_cparams(("arbitrary", "arbitrary", "arbitrary")),
        name="nsa_slc",
    )(nblk, blist, qT, proj, proj, proj, proj, selb, proj, ocmpT, alibi)


CONV_TS = 512
CONV_HALO = 32
CONV_RC = 64
CONV_CC = 256


def _conv_kernel(a_ref, b_ref, w_ref, cb_ref, g_ref, beta_ref, o_ref, hbuf, ybuf):
    si = pl.program_id(1)

    @pl.when(si == 0)
    def _():
        hbuf[pl.ds(0, CONV_HALO), :] = jnp.zeros((CONV_HALO, CONV_CH), F32)

    @pl.when(si > 0)
    def _():
        hbuf[pl.ds(0, CONV_HALO), :] = hbuf[pl.ds(CONV_TS, CONV_HALO), :]

    a = a_ref[...].astype(F32)
    bb = b_ref[...].astype(F32)
    hbuf[pl.ds(CONV_HALO, CONV_TS), :] = a * jax.nn.sigmoid(bb)

    off = CONV_HALO - (CONV_WIDTH - 1)
    for cc in range(CONV_CH // CONV_CC):
        cols = slice(cc * CONV_CC, (cc + 1) * CONV_CC)
        wts = w_ref[:, cols]

        def row_body(r, carry, cols=cols, wts=wts):
            r0 = pl.multiple_of(r * CONV_RC, CONV_RC)
            win = hbuf[pl.ds(r0, CONV_RC + CONV_HALO), cols]
            acc = jnp.zeros((CONV_RC, CONV_CC), F32)
            for k in range(CONV_WIDTH):
                acc = acc + win[off + k:off + k + CONV_RC, :] * wts[k:k + 1, :]
            ybuf[pl.ds(r0, CONV_RC), cols] = acc
            return carry

        lax.fori_loop(0, CONV_TS // CONV_RC, row_body, 0)

    y = ybuf[...] + cb_ref[...]
    mu = jnp.mean(y, axis=-1, keepdims=True)
    yc = y - mu
    var = jnp.mean(yc * yc, axis=-1, keepdims=True)
    z = yc * lax.rsqrt(var + LN_EPS) * g_ref[...] + beta_ref[...]
    o_ref[...] = (z * jax.nn.sigmoid(z)).astype(o_ref.dtype)


def _conv(proj, conv_w, conv_b, ln_g, ln_b, B, S):
    T = B * S
    ts = CONV_TS
    assert S % ts == 0
    nS = S // ts
    ca = COL_GLU // CONV_CH
    vec = lambda: pl.BlockSpec((1, CONV_CH), lambda b, s: (0, 0))
    return pl.pallas_call(
        _conv_kernel,
        out_shape=jax.ShapeDtypeStruct((T, CONV_CH), BF16),
        grid=(B, nS),
        in_specs=[pl.BlockSpec((ts, CONV_CH), lambda b, s: (b * nS + s, ca)),
                  pl.BlockSpec((ts, CONV_CH), lambda b, s: (b * nS + s, ca + 1)),
                  pl.BlockSpec((CONV_WIDTH, CONV_CH), lambda b, s: (0, 0)),
                  vec(), vec(), vec()],
        out_specs=pl.BlockSpec((ts, CONV_CH), lambda b, s: (b * nS + s, 0)),
        scratch_shapes=[pltpu.VMEM((CONV_HALO + ts, CONV_CH), F32),
                        pltpu.VMEM((ts, CONV_CH), F32)],
        compiler_params=_cparams(("arbitrary", "arbitrary")),
        name="conformer_conv",
    )(proj, proj, conv_w, conv_b.reshape(1, -1), ln_g.reshape(1, -1), ln_b.reshape(1, -1))


def _merge_kernel(o_ref, h_ref, wa_ref, wb_ref, ga_ref, gb_ref, out_ref):
    ya = jnp.dot(o_ref[...], wa_ref[...], preferred_element_type=F32)
    yb = jnp.dot(h_ref[...], wb_ref[...], preferred_element_type=F32)
    ga = jax.nn.sigmoid(ga_ref[...].astype(F32))
    gb = jax.nn.sigmoid(gb_ref[...].astype(F32))
    out_ref[...] = (ga * ya + gb * yb).astype(out_ref.dtype)


def _merge(o_nsa, h_conv, wa, wb, proj):
    T = o_nsa.shape[0]
    tm = min(512, T)
    tn = 1024
    nN = D_MODEL // tn
    ga0 = COL_MERGE // tn
    return pl.pallas_call(
        _merge_kernel,
        out_shape=jax.ShapeDtypeStruct((T, D_MODEL), BF16),
        grid=(T // tm, nN),
        in_specs=[pl.BlockSpec((tm, Q_DIM), lambda i, j: (i, 0)),
                  pl.BlockSpec((tm, CONV_CH), lambda i, j: (i, 0)),
                  pl.BlockSpec((Q_DIM, tn), lambda i, j: (0, j)),
                  pl.BlockSpec((CONV_CH, tn), lambda i, j: (0, j)),
                  pl.BlockSpec((tm, tn), lambda i, j: (i, ga0 + j)),
                  pl.BlockSpec((tm, tn), lambda i, j: (i, ga0 + nN + j))],
        out_specs=pl.BlockSpec((tm, tn), lambda i, j: (i, j)),
        compiler_params=_cparams(("arbitrary", "arbitrary")),
        name="merge",
    )(o_nsa, h_conv, wa, wb, proj, proj)


def _layer_norm_rows(y, g, b):
    mu = jnp.mean(y, axis=-1, keepdims=True)
    yc = y - mu
    var = jnp.mean(yc * yc, axis=-1, keepdims=True)
    return yc * lax.rsqrt(var + LN_EPS) * g + b


def _split2(v):
    hi = v.astype(BF16)
    return hi, (v - hi.astype(F32)).astype(BF16)


def _outproj_kernel(mix_ref, w_ref, x_ref, g_ref, b_ref, rw_hi_ref, rw_lo_ref, rb_ref,
                    x1_ref, x1b_ref, ti_ref, tg_ref):
    m = jnp.dot(mix_ref[...], w_ref[...], preferred_element_type=F32)
    x1 = _layer_norm_rows(DEEPNORM_ALPHA * x_ref[...] + m, g_ref[...], b_ref[...])
    x1_ref[...] = x1
    x1b_ref[...] = x1.astype(BF16)
    x_hi, x_lo = _split2(x1)
    logits = (jnp.dot(x_hi, rw_hi_ref[...], preferred_element_type=F32)
              + jnp.dot(x_hi, rw_lo_ref[...], preferred_element_type=F32)
              + jnp.dot(x_lo, rw_hi_ref[...], preferred_element_type=F32)) + rb_ref[...]
    lane = lax.broadcasted_iota(jnp.int32, logits.shape, 1)
    logits = jnp.where(lane < N_EXPERTS, logits, -jnp.inf)
    ti = jnp.zeros(logits.shape, jnp.int32)
    tv = jnp.zeros(logits.shape, F32)
    top0 = None
    den = jnp.zeros((logits.shape[0], 1), F32)
    for k in range(TOP_K):
        mx = jnp.max(logits, axis=-1, keepdims=True)
        idx = jnp.min(jnp.where(logits == mx, lane, LANES), axis=-1, keepdims=True)
        if top0 is None:
            top0 = mx
        e = jnp.exp(mx - top0)
        den = den + e
        ti = jnp.where(lane == k, idx, ti)
        tv = jnp.where(lane == k, e, tv)
        logits = jnp.where(lane == idx, -jnp.inf, logits)
    ti_ref[...] = ti
    tg_ref[...] = tv / den


def _outproj(mix, w_out, x2d, g, b, rw_hi, rw_lo, rb):
    T = mix.shape[0]
    tm = min(256, T)
    full = lambda shape: pl.BlockSpec(shape, lambda i: (0, 0))
    rowb = lambda w: pl.BlockSpec((tm, w), lambda i: (i, 0))
    return pl.pallas_call(
        _outproj_kernel,
        out_shape=(jax.ShapeDtypeStruct((T, D_MODEL), F32),
                   jax.ShapeDtypeStruct((T, D_MODEL), BF16),
                   jax.ShapeDtypeStruct((T, LANES), jnp.int32),
                   jax.ShapeDtypeStruct((T, LANES), F32)),
        grid=(T // tm,),
        in_specs=[rowb(D_MODEL), full((D_MODEL, D_MODEL)), rowb(D_MODEL),
                  full((1, D_MODEL)), full((1, D_MODEL)),
                  full((D_MODEL, LANES)), full((D_MODEL, LANES)), full((1, LANES))],
        out_specs=(rowb(D_MODEL), rowb(D_MODEL), rowb(LANES), rowb(LANES)),
        compiler_params=_cparams(("arbitrary",)),
        name="outproj_ln_router",
    )(mix, w_out, x2d, g, b, rw_hi, rw_lo, rb)


def _moe_up_kernel(be_ref, nused_ref, x_ref, wg_ref, wu_ref, bg_ref, bu_ref, h_ref,
                   wgb_ref, wub_ref):
    r = pl.program_id(1)
    prev = be_ref[jnp.maximum(r - 1, 0)]
    fresh = (r == 0) | (be_ref[r] != prev)

    @pl.when(fresh)
    def _():
        wgb_ref[...] = wg_ref[0].astype(BF16)
        wub_ref[...] = wu_ref[0].astype(BF16)

    @pl.when(r < nused_ref[0])
    def _():
        x = x_ref[...]
        gt = jnp.dot(x, wgb_ref[...], preferred_element_type=F32) + bg_ref[0]
        up = jnp.dot(x, wub_ref[...], preferred_element_type=F32) + bu_ref[0]
        gt = jnp.minimum(gt, SWIGLU_LIMIT)
        up = jnp.clip(up, -SWIGLU_LIMIT, SWIGLU_LIMIT)
        h = gt * jax.nn.sigmoid(SWIGLU_ALPHA * gt) * (up + 1.0)
        h_ref[...] = h.astype(h_ref.dtype)


def _moe_row_maps(nblk):
    rd = lambda r, nu: jnp.minimum(r, nu[0] - 1)
    wr = lambda r, nu: jnp.where(r < nu[0], r, nblk)
    return rd, wr


def _moe_up(blk_e, n_used, xs, w_gate, w_up, b_gate, b_up):
    P = xs.shape[0]
    bm, tf = MOE_BLOCK, MOE_UP_TILE
    nblk = P // bm
    rd, wr = _moe_row_maps(nblk)
    wspec = pl.BlockSpec((1, D_MODEL, tf), lambda f, r, be, nu: (be[r], 0, f))
    bspec = pl.BlockSpec((1, 1, tf), lambda f, r, be, nu: (be[r], 0, f))
    grid_spec = pltpu.PrefetchScalarGridSpec(
        num_scalar_prefetch=2,
        grid=(D_FF // tf, nblk),
        in_specs=[pl.BlockSpec((bm, D_MODEL), lambda f, r, be, nu: (rd(r, nu), 0)),
                  wspec, wspec, bspec, bspec],
        out_specs=pl.BlockSpec((bm, tf), lambda f, r, be, nu: (wr(r, nu), f)),
        scratch_shapes=[pltpu.VMEM((D_MODEL, tf), BF16), pltpu.VMEM((D_MODEL, tf), BF16)],
    )
    return pl.pallas_call(
        _moe_up_kernel,
        out_shape=jax.ShapeDtypeStruct((P + bm, D_FF), BF16),
        grid_spec=grid_spec,
        compiler_params=_cparams(("arbitrary", "arbitrary")),
        name="moe_up",
    )(blk_e, n_used, xs, w_gate, w_up, b_gate.reshape(N_EXPERTS, 1, D_FF),
      b_up.reshape(N_EXPERTS, 1, D_FF))


def _moe_down_kernel(be_ref, nused_ref, h_ref, wd_ref, bd_ref, o_ref, wdb_ref):
    r = pl.program_id(0)
    prev = be_ref[jnp.maximum(r - 1, 0)]
    fresh = (r == 0) | (be_ref[r] != prev)

    @pl.when(fresh)
    def _():
        wdb_ref[...] = wd_ref[0].astype(BF16)

    @pl.when(r < nused_ref[0])
    def _():
        y = jnp.dot(h_ref[...], wdb_ref[...], preferred_element_type=F32) + bd_ref[0]
        o_ref[...] = y.astype(o_ref.dtype)


def _moe_down(blk_e, n_used, h, w_down, b_down):
    bm = MOE_BLOCK
    nblk = h.shape[0] // bm - 1
    rd, wr = _moe_row_maps(nblk)
    grid_spec = pltpu.PrefetchScalarGridSpec(
        num_scalar_prefetch=2,
        grid=(nblk,),
        in_specs=[pl.BlockSpec((bm, D_FF), lambda r, be, nu: (rd(r, nu), 0)),
                  pl.BlockSpec((1, D_FF, D_MODEL), lambda r, be, nu: (be[r], 0, 0)),
                  pl.BlockSpec((1, 1, D_MODEL), lambda r, be, nu: (be[r], 0, 0))],
        out_specs=pl.BlockSpec((bm, D_MODEL), lambda r, be, nu: (wr(r, nu), 0)),
        scratch_shapes=[pltpu.VMEM((D_FF, D_MODEL), BF16)],
    )
    return pl.pallas_call(
        _moe_down_kernel,
        out_shape=jax.ShapeDtypeStruct(((nblk + 1) * bm, D_MODEL), BF16),
        grid_spec=grid_spec,
        compiler_params=_cparams(("arbitrary",)),
        name="moe_down",
    )(blk_e, n_used, h, w_down, b_down.reshape(N_EXPERTS, 1, D_MODEL))


def _final_kernel(x1_ref, y_ref, tg_ref, g_ref, b_ref, o_ref):
    tg = tg_ref[...]
    f = tg[:, 0:1] * y_ref[:, :D_MODEL].astype(F32)
    for k in range(1, TOP_K):
        f = f + tg[:, k:k + 1] * y_ref[:, k * D_MODEL:(k + 1) * D_MODEL].astype(F32)
    o_ref[...] = _layer_norm_rows(DEEPNORM_ALPHA * x1_ref[...] + f, g_ref[...], b_ref[...])


def _final(x1, y4, tg, g, b):
    T = x1.shape[0]
    tm = min(256, T)
    return pl.pallas_call(
        _final_kernel,
        out_shape=jax.ShapeDtypeStruct((T, D_MODEL), F32),
        grid=(T // tm,),
        in_specs=[pl.BlockSpec((tm, D_MODEL), lambda i: (i, 0)),
                  pl.BlockSpec((tm, TOP_K * D_MODEL), lambda i: (i, 0)),
                  pl.BlockSpec((tm, LANES), lambda i: (i, 0)),
                  pl.BlockSpec((1, D_MODEL), lambda i: (0, 0)),
                  pl.BlockSpec((1, D_MODEL), lambda i: (0, 0))],
        out_specs=pl.BlockSpec((tm, D_MODEL), lambda i: (i, 0)),
        compiler_params=_cparams(("arbitrary",)),
        name="combine_ln",
    )(x1, y4, tg, g, b)


def _prep_w_in(w_in):
    splits = np.cumsum([Q_DIM] + [KV_DIM] * 6 + [NSA_GATE_DIM, 2 * CONV_CH, 2 * D_MODEL])
    q = w_in[:, :splits[0]] * (LOG2E / math.sqrt(HEAD_DIM))
    kv = w_in[:, splits[0]:splits[6]]
    gate = w_in[:, splits[6]:splits[7]]
    glu = w_in[:, splits[7]:splits[8]]
    merge = w_in[:, splits[8]:splits[9]]
    per_g = HEADS_PER_GROUP * 3
    gate_tiles = [jnp.pad(gate[:, g * per_g:(g + 1) * per_g], ((0, 0), (0, LANES - per_g)))
                  for g in range(N_KV_GROUPS)]
    return jnp.concatenate([q, glu, merge, kv] + gate_tiles, axis=1).astype(BF16)


def _mixer(x2d, B, S, w_in, cmp_pe, cmp_w1, cmp_b1, cmp_w2, cmp_b2, w_nsa_proj,
           conv_w, conv_b, conv_ln_g, conv_ln_b, w_conv_proj):
    T = B * S
    G = N_KV_GROUPS
    proj = _inproj(x2d, _prep_w_in(w_in))

    kcvc = _compress(proj, cmp_pe.reshape(2, 1, CMP_BLOCK * HEAD_DIM),
                     cmp_w1.astype(BF16), cmp_b1.reshape(2, 1, HEAD_DIM),
                     cmp_w2.astype(BF16), cmp_b2.reshape(2, 1, HEAD_DIM), B, S)

    C = S // Q_BLOCK
    qT = proj[:, COL_Q:COL_Q + Q_DIM].reshape(B * C, Q_BLOCK, G, HEADS_PER_GROUP, HEAD_DIM)
    qT = qT.transpose(0, 2, 4, 3, 1).reshape(B * C, G, HEAD_DIM, GROUP_W)
    alibi = _alibi_rows().astype(BF16)
    ocmpT, selb = _nsa_cmp(qT, kcvc, alibi, B, S)
    nsp = selb.shape[2]
    fl = jnp.max(selb, axis=3) > 0.5 * NEG_INF
    fl = fl.reshape(B, C, G, nsp).transpose(0, 2, 1, 3)
    past = (jnp.arange(nsp, dtype=jnp.int32)[None, :]
            < (Q_BLOCK // SEL_BLOCK) * jnp.arange(C, dtype=jnp.int32)[:, None])
    fl = (fl & past[None, None]).reshape(B * G * C, nsp)
    nblk = jnp.sum(fl, axis=1).astype(jnp.int32)
    blist = jnp.argsort(jnp.logical_not(fl), axis=1, stable=True).astype(jnp.int32)
    o_nsa = _nsa_slc(proj, qT, selb, nblk, blist.reshape(-1), ocmpT, alibi, B, S)

    h_conv = _conv(proj, conv_w, conv_b, conv_ln_g, conv_ln_b, B, S)
    return _merge(o_nsa, h_conv, w_nsa_proj.astype(BF16), w_conv_proj.astype(BF16), proj)


def _moe(x1b, top_i, w_gate, b_gate, w_up, b_up, w_down, b_down):
    T = x1b.shape[0]
    A = T * TOP_K
    bm = MOE_BLOCK
    flat_e = top_i[:, :TOP_K].reshape(A)
    onehot = (flat_e[:, None] == jnp.arange(N_EXPERTS, dtype=jnp.int32)[None, :]).astype(jnp.int32)
    csum = jnp.cumsum(onehot, axis=0)
    rank = jnp.sum(onehot * csum, axis=1) - 1
    counts = csum[-1]
    padded = (counts + bm - 1) // bm * bm
    pad_end = jnp.cumsum(padded)
    pad_start = pad_end - padded
    dest = pad_start[flat_e] + rank
    P = -(-A // bm) * bm + N_EXPERTS * bm
    nblk = P // bm
    flat_tok = jnp.arange(A, dtype=jnp.int32) // TOP_K
    buf_tok = jnp.zeros((P,), jnp.int32).at[dest].set(flat_tok)
    blk_start = jnp.arange(nblk, dtype=jnp.int32) * bm
    blk_e = jnp.minimum(jnp.sum(blk_start[:, None] >= pad_end[None, :], axis=1),
                        N_EXPERTS - 1).astype(jnp.int32)
    n_used = (pad_end[-1] // bm).astype(jnp.int32).reshape(1)
    xs = x1b[buf_tok]
    h = _moe_up(blk_e, n_used, xs, w_gate, w_up, b_gate, b_up)
    out = _moe_down(blk_e, n_used, h, w_down, b_down)
    return out[dest].reshape(T, TOP_K * D_MODEL)


def kernel(x, w_in, cmp_pe, cmp_w1, cmp_b1, cmp_w2, cmp_b2, w_nsa_proj, conv_w, conv_b, conv_ln_g, conv_ln_b, w_conv_proj, w_out, ln1_g, ln1_b, router_w, router_b, w_gate, b_gate, w_up, b_up, w_down, b_down, ln2_g, ln2_b):
    B, S, D = x.shape
    T = B * S
    x2d = x.reshape(T, D)
    for l in range(DEPTH):
        mix = _mixer(x2d, B, S, w_in[l], cmp_pe[l], cmp_w1[l], cmp_b1[l], cmp_w2[l], cmp_b2[l],
                     w_nsa_proj[l], conv_w[l], conv_b[l], conv_ln_g[l], conv_ln_b[l],
                     w_conv_proj[l])
        rw = jnp.pad(router_w[l], ((0, 0), (0, LANES - N_EXPERTS)))
        rw_hi = rw.astype(BF16)
        rw_lo = (rw - rw_hi.astype(F32)).astype(BF16)
        rb = jnp.pad(router_b[l], (0, LANES - N_EXPERTS)).reshape(1, LANES)
        x1, x1b, top_i, top_g = _outproj(
            mix, w_out[l].astype(BF16), x2d, ln1_g[l].reshape(1, D), ln1_b[l].reshape(1, D),
            rw_hi, rw_lo, rb)
        y4 = _moe(x1b, top_i, w_gate[l], b_gate[l], w_up[l], b_up[l], w_down[l], b_down[l])
        x2d = _final(x1, y4, top_g, ln2_g[l].reshape(1, D), ln2_b[l].reshape(1, D))
    return x2d.reshape(B, S, D)
```

```python
import functools
import math

import numpy as np
import jax
import jax.numpy as jnp
from jax import lax
from jax.experimental import pallas as pl
from jax.experimental.pallas import tpu as pltpu

D_MODEL = 2048
N_HEADS = 16
N_KV_GROUPS = 2
HEADS_PER_GROUP = N_HEADS // N_KV_GROUPS
HEAD_DIM = 128
CMP_BLOCK = 32
CMP_STRIDE = 16
SEL_BLOCK = 64
SEL_SHIFT = 6
SEL_TOP_N = 16
WINDOW = 512
Q_BLOCK = 128
N_OVERLAP = (SEL_BLOCK + CMP_BLOCK) // CMP_STRIDE - 1
FORCE_BONUS = 1.0e4
CONV_CH = D_MODEL // 2
CONV_WIDTH = 31
N_EXPERTS = 32
TOP_K = 4
D_FF = D_MODEL
SWIGLU_LIMIT = 7.0
SWIGLU_ALPHA = 1.702
LN_EPS = 1e-5
DEPTH = 1
DEEPNORM_ALPHA = (2 * DEPTH) ** 0.25
NEG_INF = -1e30
TINY = 1e-30

Q_DIM = N_HEADS * HEAD_DIM
KV_DIM = N_KV_GROUPS * HEAD_DIM
NSA_GATE_DIM = N_HEADS * 3

LANES = 128
SUBLANES = 8
VMEM_LIMIT = 56 * 1024 * 1024

COL_Q = 0
COL_GLU = COL_Q + Q_DIM
COL_MERGE = COL_GLU + 2 * CONV_CH
COL_KC = COL_MERGE + 2 * D_MODEL
COL_VC = COL_KC + KV_DIM
COL_KS = COL_VC + KV_DIM
COL_VS = COL_KS + KV_DIM
COL_KW = COL_VS + KV_DIM
COL_VW = COL_KW + KV_DIM
COL_GATE = COL_VW + KV_DIM
PROJ_W = COL_GATE + N_KV_GROUPS * LANES

BF16 = jnp.bfloat16
F32 = jnp.float32

MOE_BLOCK = 256
MOE_UP_TILE = 1024

_ALIBI = np.exp2(-8.0 * np.arange(1, N_HEADS + 1, dtype=np.float32) / N_HEADS).astype(np.float32)
_ALIBI = _ALIBI.reshape(N_KV_GROUPS, HEADS_PER_GROUP)


def _cparams(sem, vmem=VMEM_LIMIT):
    return pltpu.CompilerParams(dimension_semantics=sem, vmem_limit_bytes=vmem)


def _inproj_kernel(x_ref, w_ref, o_ref, xb_ref):
    @pl.when(pl.program_id(1) == 0)
    def _():
        xb_ref[...] = x_ref[...].astype(BF16)

    o_ref[...] = jnp.dot(xb_ref[...], w_ref[...],
                         preferred_element_type=F32).astype(o_ref.dtype)


def _inproj(x2d, w_p):
    T, D = x2d.shape
    N = w_p.shape[1]
    tm = min(1024, T)
    tn = 768
    assert T % tm == 0 and N % tn == 0
    return pl.pallas_call(
        _inproj_kernel,
        out_shape=jax.ShapeDtypeStruct((T, N), BF16),
        grid=(T // tm, N // tn),
        in_specs=[pl.BlockSpec((tm, D), lambda i, j: (i, 0)),
                  pl.BlockSpec((D, tn), lambda i, j: (0, j))],
        out_specs=pl.BlockSpec((tm, tn), lambda i, j: (i, j)),
        scratch_shapes=[pltpu.VMEM((tm, D), BF16)],
        compiler_params=_cparams(("arbitrary", "arbitrary")),
        name="inproj",
    )(x2d, w_p)


def _gelu_tanh(x):
    c = math.sqrt(2.0 / math.pi)
    return 0.5 * x * (1.0 + jnp.tanh(c * (x + 0.044715 * (x * x * x))))


def _compress_kernel(x_ref, pe_ref, w1_ref, b1_ref, w2_ref, b2_ref, o_ref, xf_ref):
    S = x_ref.shape[0]
    n16 = S // CMP_STRIDE
    xf_ref[...] = x_ref[...].astype(F32)
    top = jnp.zeros((n16, HEAD_DIM), F32)
    bot = jnp.zeros((n16, HEAD_DIM), F32)
    for j in range(CMP_STRIDE):
        xj = xf_ref[pl.ds(j, n16, stride=CMP_STRIDE), :]
        lo, hi = j * HEAD_DIM, (CMP_STRIDE + j) * HEAD_DIM
        top = top + jnp.dot((xj + pe_ref[0, :, lo:lo + HEAD_DIM]).astype(BF16),
                            w1_ref[0, lo:lo + HEAD_DIM, :], preferred_element_type=F32)
        bot = bot + jnp.dot((xj + pe_ref[0, :, hi:hi + HEAD_DIM]).astype(BF16),
                            w1_ref[0, hi:hi + HEAD_DIM, :], preferred_element_type=F32)
    pre = top + pltpu.roll(bot, n16 - 1, 0) + b1_ref[0]
    h = _gelu_tanh(pre)
    o = jnp.dot(h.astype(BF16), w2_ref[0], preferred_element_type=F32) + b2_ref[0]
    o_ref[0, 0] = o.astype(o_ref.dtype)


def _compress(proj, pe, w1, b1, w2, b2, B, S):
    n_slot = 2 * N_KV_GROUPS
    n16 = S // CMP_STRIDE
    wide = CMP_BLOCK * HEAD_DIM
    kv = lambda s, b: (s // N_KV_GROUPS, 0, 0)
    return pl.pallas_call(
        _compress_kernel,
        out_shape=jax.ShapeDtypeStruct((n_slot, B, n16, HEAD_DIM), BF16),
        grid=(n_slot, B),
        in_specs=[pl.BlockSpec((S, HEAD_DIM), lambda s, b: (b, COL_KC // HEAD_DIM + s)),
                  pl.BlockSpec((1, 1, wide), kv),
                  pl.BlockSpec((1, wide, HEAD_DIM), kv),
                  pl.BlockSpec((1, 1, HEAD_DIM), kv),
                  pl.BlockSpec((1, HEAD_DIM, HEAD_DIM), kv),
                  pl.BlockSpec((1, 1, HEAD_DIM), kv)],
        out_specs=pl.BlockSpec((1, 1, n16, HEAD_DIM), lambda s, b: (s, b, 0, 0)),
        scratch_shapes=[pltpu.VMEM((S, HEAD_DIM), F32)],
        compiler_params=_cparams(("arbitrary", "arbitrary")),
        name="compress",
    )(proj, pe, w1, b1, w2, b2)


LOG2E = 1.4426950408889634
KAUG = 2 * HEAD_DIM
ROW_ALIBI = 0
ROW_SLOT = 16
SLC_GROUP = 8
GROUP_W = HEADS_PER_GROUP * Q_BLOCK
WIN_TILES = WINDOW // Q_BLOCK
WIN_KEYS = WINDOW + Q_BLOCK


def _window_pos_cols():
    rel = np.arange(WIN_KEYS) - WINDOW
    a = (rel // SEL_BLOCK) * SEL_BLOCK
    r = rel - a
    cols = np.zeros((WIN_KEYS, LANES), np.float32)
    cols[:, ROW_ALIBI:ROW_ALIBI + 3] = a[:, None]
    cols[:, ROW_ALIBI + 3:ROW_ALIBI + 6] = r[:, None]
    return jnp.asarray(cols, BF16)


def _alibi_rows():
    s2 = jnp.asarray(_ALIBI * np.float32(LOG2E), F32)
    hi = s2.astype(BF16)
    r1 = s2 - hi.astype(F32)
    mid = r1.astype(BF16)
    lo = (r1 - mid.astype(F32)).astype(BF16)
    trip = jnp.stack([hi, mid, lo, hi, mid, lo], axis=1)
    rows = jnp.pad(trip, ((0, 0), (0, 16 - 6), (0, 0)))
    return jnp.repeat(rows, Q_BLOCK, axis=2)


def _dot_ta(a, b):
    return lax.dot_general(a, b, (((0,), (0,)), ((), ())), preferred_element_type=F32)


def _fill_qaug(qaug, qT_ref, alibi_ref):
    qaug[pl.ds(0, HEAD_DIM), :] = qT_ref[0, 0]
    qaug[pl.ds(HEAD_DIM, 16), :] = alibi_ref[0]
    qaug[pl.ds(HEAD_DIM + 16, HEAD_DIM - 16), :] = jnp.zeros((HEAD_DIM - 16, GROUP_W), BF16)


def _pos_cols(a_val, r_val, lane, slot_lane=None):
    base = jnp.where((lane >= ROW_ALIBI + 3) & (lane < ROW_ALIBI + 6), r_val, 0.0)
    if slot_lane is not None:
        base = jnp.where(lane == slot_lane, 1.0, base)
    return jnp.where(lane < ROW_ALIBI + 3, a_val, base)


def _nsa_cmp_kernel(qT_ref, kc_ref, vc_ref, alibi_ref, ocmpT_ref, selb_ref, qaug, *, n_selb):
    c = pl.program_id(2)
    ncp = kc_ref.shape[2]
    nsp = selb_ref.shape[2]
    _fill_qaug(qaug, qT_ref, alibi_ref)

    n_i = lax.broadcasted_iota(jnp.int32, (ncp, LANES), 0)
    lane = lax.broadcasted_iota(jnp.int32, (ncp, LANES), 1)
    end_rel = n_i * CMP_STRIDE + (CMP_BLOCK - 1) - c * Q_BLOCK
    a_val = jnp.left_shift(jnp.right_shift(end_rel, SEL_SHIFT), SEL_SHIFT).astype(F32)
    r_val = (end_rel & (SEL_BLOCK - 1)).astype(F32)
    pc = _pos_cols(a_val, r_val, lane).astype(BF16)
    s = jnp.dot(jnp.concatenate([kc_ref[0, 0], pc], axis=1), qaug[...],
                preferred_element_type=F32)

    tq = lax.broadcasted_iota(jnp.int32, (1, Q_BLOCK), 1)
    t_row = c * Q_BLOCK + tq
    n_col = lax.broadcasted_iota(jnp.int32, (ncp, Q_BLOCK), 0)
    mask_bias = jnp.where(n_col * CMP_STRIDE + (CMP_BLOCK - 1) <= t_row, 0.0, NEG_INF)
    has_valid = t_row >= CMP_BLOCK - 1

    imp = jnp.zeros((ncp, Q_BLOCK), F32)
    es, rls = [], []
    for h in range(HEADS_PER_GROUP):
        sh = s[:, h * Q_BLOCK:(h + 1) * Q_BLOCK] + mask_bias
        m = jnp.max(sh, axis=0, keepdims=True)
        e = jnp.exp2(sh - m)
        l = jnp.sum(e, axis=0, keepdims=True)
        rl = jnp.where(has_valid, 1.0 / jnp.maximum(l, TINY), 0.0)
        imp = imp + e * rl
        es.append(e.astype(BF16))
        rls.append(rl)
    oT = _dot_ta(vc_ref[0, 0], jnp.concatenate(es, axis=1))
    ocmpT_ref[0, 0] = oT * jnp.concatenate(rls, axis=1)

    j_i = lax.broadcasted_iota(jnp.int32, (nsp, ncp), 0)
    n_j = lax.broadcasted_iota(jnp.int32, (nsp, ncp), 1)
    lo = j_i * (SEL_BLOCK // CMP_STRIDE) - (CMP_BLOCK // CMP_STRIDE) + 1
    ovl = ((n_j >= lo) & (n_j < lo + N_OVERLAP)).astype(BF16)
    i_hi = imp.astype(BF16)
    r1 = imp - i_hi.astype(F32)
    i_mid = r1.astype(BF16)
    i_lo = (r1 - i_mid.astype(F32)).astype(BF16)
    imp_sel = (jnp.dot(ovl, i_hi, preferred_element_type=F32)
               + jnp.dot(ovl, i_mid, preferred_element_type=F32)
               + jnp.dot(ovl, i_lo, preferred_element_type=F32))

    blk = lax.broadcasted_iota(jnp.int32, (nsp, Q_BLOCK), 0)
    cur = jnp.right_shift(t_row, SEL_SHIFT)
    valid = blk * SEL_BLOCK <= t_row
    forced = (blk == 0) | (blk == cur) | (blk == cur - 1)
    score = jnp.where(valid, imp_sel + jnp.where(forced, FORCE_BONUS, 0.0), -1.0)
    score = jnp.where(blk < n_selb, score, -3.0)
    selb = jnp.full((nsp, Q_BLOCK), NEG_INF, F32)
    for _ in range(min(SEL_TOP_N, n_selb)):
        mx = jnp.max(score, axis=0, keepdims=True)
        first = jnp.min(jnp.where(score == mx, blk, nsp), axis=0, keepdims=True)
        pick = blk == first
        selb = jnp.where(pick, 0.0, selb)
        score = jnp.where(pick, -2.0, score)
    selb_ref[0, 0] = selb


def _nsa_cmp(qT, kcvc, alibi, B, S):
    C = S // Q_BLOCK
    G = N_KV_GROUPS
    ncp = kcvc.shape[2]
    n_selb = S // SEL_BLOCK
    nsp = max(LANES, n_selb)
    kern = functools.partial(_nsa_cmp_kernel, n_selb=n_selb)
    return pl.pallas_call(
        kern,
        out_shape=(jax.ShapeDtypeStruct((B * C, G, HEAD_DIM, GROUP_W), F32),
                   jax.ShapeDtypeStruct((B * C, G, nsp, Q_BLOCK), F32)),
        grid=(B, G, C),
        in_specs=[pl.BlockSpec((1, 1, HEAD_DIM, GROUP_W), lambda b, g, c: (b * C + c, g, 0, 0)),
                  pl.BlockSpec((1, 1, ncp, HEAD_DIM), lambda b, g, c: (g, b, 0, 0)),
                  pl.BlockSpec((1, 1, ncp, HEAD_DIM), lambda b, g, c: (G + g, b, 0, 0)),
                  pl.BlockSpec((1, 16, GROUP_W), lambda b, g, c: (g, 0, 0))],
        out_specs=(pl.BlockSpec((1, 1, HEAD_DIM, GROUP_W), lambda b, g, c: (b * C + c, g, 0, 0)),
                   pl.BlockSpec((1, 1, nsp, Q_BLOCK), lambda b, g, c: (b * C + c, g, 0, 0))),
        scratch_shapes=[pltpu.VMEM((KAUG, GROUP_W), BF16)],
        compiler_params=_cparams(("arbitrary", "arbitrary", "arbitrary")),
        name="nsa_cmp",
    )(qT, kcvc, kcvc, alibi)


def _softmax_tile(st, qaug, k_tile, pc, v_tile, bias, first):
    m_ref, l_ref, acc_ref = st
    kaug = jnp.concatenate([k_tile, pc.astype(BF16)], axis=1)
    s = jnp.dot(kaug, qaug[...], preferred_element_type=F32)
    if bias is not None:
        s = jnp.concatenate([s[:, h * Q_BLOCK:(h + 1) * Q_BLOCK] + bias
                             for h in range(HEADS_PER_GROUP)], axis=1)
    mx = jnp.max(s, axis=0, keepdims=True)
    if first:
        m_new = mx
    else:
        m_old = m_ref[pl.ds(0, 1), :]
        m_new = jnp.maximum(m_old, mx)
    p = jnp.exp2(s - m_new)
    ps = jnp.sum(p, axis=0, keepdims=True)
    pv = _dot_ta(v_tile, p.astype(BF16))
    if first:
        l_ref[pl.ds(0, 1), :] = ps
        acc_ref[...] = pv
    else:
        alpha = jnp.exp2(m_old - m_new)
        l_ref[pl.ds(0, 1), :] = alpha * l_ref[pl.ds(0, 1), :] + ps
        acc_ref[...] = alpha * acc_ref[...] + pv
    m_ref[pl.ds(0, 1), :] = m_new


def _bias_rows(s, bias):
    return jnp.concatenate([s[:, h * Q_BLOCK:(h + 1) * Q_BLOCK] + bias
                            for h in range(HEADS_PER_GROUP)], axis=1)


def _nsa_slc_kernel(nblk_ref, blist_ref, qT_ref, ks_ref, vs_ref, kw_ref, vw_ref,
                    selb_ref, gate_ref, ocmpT_ref, alibi_ref, wpc_ref, o_ref,
                    qaug, m_ref, l_ref, acc_ref, owin_ref):
    b = pl.program_id(0)
    g = pl.program_id(1)
    c = pl.program_id(2)
    n_chunk = pl.num_programs(2)
    nsp = selb_ref.shape[2]
    row = (b * N_KV_GROUPS + g) * n_chunk + c
    n_act = nblk_ref[row]
    st = (m_ref, l_ref, acc_ref)
    _fill_qaug(qaug, qT_ref, alibi_ref)

    lane = lax.broadcasted_iota(jnp.int32, (SEL_BLOCK, LANES), 1)
    r_val = lax.broadcasted_iota(jnp.int32, (SEL_BLOCK, LANES), 0).astype(F32)
    ki = lax.broadcasted_iota(jnp.int32, (Q_BLOCK, Q_BLOCK), 0)
    qi = lax.broadcasted_iota(jnp.int32, (Q_BLOCK, Q_BLOCK), 1)
    causal_bias = jnp.where(ki <= qi, 0.0, NEG_INF)
    tail_bias = jnp.where(ki > qi, 0.0, NEG_INF)

    def pair_cols(first_block_rel):
        return jnp.concatenate(
            [_pos_cols(float(SEL_BLOCK * (first_block_rel + i)), r_val, lane) for i in range(2)], axis=0)

    def finish():
        return acc_ref[...] / jnp.maximum(l_ref[pl.ds(0, 1), :], TINY)

    n_wt = WIN_TILES
    base = pl.multiple_of(c * Q_BLOCK, Q_BLOCK)

    @pl.when(c >= n_wt)
    def _():
        start = pl.multiple_of((c - n_wt) * Q_BLOCK, Q_BLOCK)
        kaug = jnp.concatenate([kw_ref[pl.ds(start, WIN_KEYS), :], wpc_ref[...]], axis=1)
        s = jnp.dot(kaug, qaug[...], preferred_element_type=F32)
        s = jnp.concatenate([_bias_rows(s[:Q_BLOCK], tail_bias), s[Q_BLOCK:WINDOW],
                             _bias_rows(s[WINDOW:], causal_bias)], axis=0)
        p = jnp.exp2(s - jnp.max(s, axis=0, keepdims=True))
        l = jnp.sum(p, axis=0, keepdims=True)
        pv = _dot_ta(vw_ref[pl.ds(start, WIN_KEYS), :], p.astype(BF16))
        owin_ref[...] = pv / jnp.maximum(l, TINY)

    @pl.when(c < n_wt)
    def _():
        _softmax_tile(st, qaug, kw_ref[pl.ds(base, Q_BLOCK), :], pair_cols(0),
                      vw_ref[pl.ds(base, Q_BLOCK), :], causal_bias, True)
        for w in range(1, n_wt):
            @pl.when(c - n_wt + w >= 0)
            def _(w=w):
                start = pl.multiple_of((c - n_wt + w) * Q_BLOCK, Q_BLOCK)
                _softmax_tile(st, qaug, kw_ref[pl.ds(start, Q_BLOCK), :],
                              pair_cols(2 * (w - n_wt)), vw_ref[pl.ds(start, Q_BLOCK), :],
                              None, False)
        owin_ref[...] = finish()

    _softmax_tile(st, qaug, ks_ref[pl.ds(base, Q_BLOCK), :], pair_cols(0),
                  vs_ref[pl.ds(base, Q_BLOCK), :], causal_bias, True)
    sub = lax.broadcasted_iota(jnp.int32, (16, GROUP_W), 0)

    def slc_body(i, carry):
        ks_t, vs_t, pcs = [], [], []
        slot_rows = jnp.zeros((16, GROUP_W), F32)
        for k in range(SLC_GROUP):
            pos = i * SLC_GROUP + k
            live = pos < n_act
            j = jnp.where(live, blist_ref[row * nsp + jnp.minimum(pos, nsp - 1)], 0)
            off = pl.multiple_of(j * SEL_BLOCK, SEL_BLOCK)
            ks_t.append(ks_ref[pl.ds(off, SEL_BLOCK), :])
            vs_t.append(vs_ref[pl.ds(off, SEL_BLOCK), :])
            a_val = ((j - 2 * c) * SEL_BLOCK).astype(F32)
            pcs.append(_pos_cols(a_val, r_val, lane, slot_lane=ROW_SLOT + k))
            rowk = selb_ref[0, 0, pl.ds(j, 1), :]
            rowk = jnp.where(live, rowk, NEG_INF)
            slot_rows = jnp.where(sub == k, jnp.tile(rowk, (1, HEADS_PER_GROUP)), slot_rows)
        qaug[pl.ds(HEAD_DIM + ROW_SLOT, 16), :] = slot_rows.astype(BF16)
        _softmax_tile(st, qaug, jnp.concatenate(ks_t, axis=0), jnp.concatenate(pcs, axis=0),
                      jnp.concatenate(vs_t, axis=0), None, False)
        return carry

    lax.fori_loop(0, (n_act + SLC_GROUP - 1) // SLC_GROUP, slc_body, 0)
    o_slcT = finish()

    gT = jax.nn.sigmoid(gate_ref[...].astype(F32).T)
    o_cmpT = ocmpT_ref[0, 0]
    o_winT = owin_ref[...]
    outs = []
    for h in range(HEADS_PER_GROUP):
        hs = slice(h * Q_BLOCK, (h + 1) * Q_BLOCK)
        oT = (gT[3 * h:3 * h + 1, :] * o_cmpT[:, hs] + gT[3 * h + 1:3 * h + 2, :] * o_slcT[:, hs]
              + gT[3 * h + 2:3 * h + 3, :] * o_winT[:, hs])
        outs.append(oT.T)
    o_ref[...] = jnp.concatenate(outs, axis=1).astype(o_ref.dtype)


def _nsa_slc(proj, qT, selb, nblk, blist, ocmpT, alibi, B, S):
    T = B * S
    C = S // Q_BLOCK
    G = N_KV_GROUPS
    nsp = selb.shape[2]
    gw = HEADS_PER_GROUP * HEAD_DIM
    kvspec = lambda col: pl.BlockSpec(
        (S, HEAD_DIM), lambda b, g, c, *_: (b, col // HEAD_DIM + g))
    tspec = lambda rows: pl.BlockSpec((1, 1, rows, GROUP_W), lambda b, g, c, *_: (b * C + c, g, 0, 0))
    grid_spec = pltpu.PrefetchScalarGridSpec(
        num_scalar_prefetch=2,
        grid=(B, G, C),
        in_specs=[tspec(HEAD_DIM),
                  kvspec(COL_KS), kvspec(COL_VS), kvspec(COL_KW), kvspec(COL_VW),
                  pl.BlockSpec((1, 1, nsp, Q_BLOCK), lambda b, g, c, *_: (b * C + c, g, 0, 0)),
                  pl.BlockSpec((Q_BLOCK, LANES), lambda b, g, c, *_: (b * C + c, COL_GATE // LANES + g)),
                  tspec(HEAD_DIM),
                  pl.BlockSpec((1, 16, GROUP_W), lambda b, g, c, *_: (g, 0, 0)),
                  pl.BlockSpec((WIN_KEYS, LANES), lambda b, g, c, *_: (0, 0))],
        out_specs=pl.BlockSpec((Q_BLOCK, gw), lambda b, g, c, *_: (b * C + c, g)),
        scratch_shapes=[pltpu.VMEM((KAUG, GROUP_W), BF16),
                        pltpu.VMEM((SUBLANES, GROUP_W), F32),
                        pltpu.VMEM((SUBLANES, GROUP_W), F32),
                        pltpu.VMEM((HEAD_DIM, GROUP_W), F32),
                        pltpu.VMEM((HEAD_DIM, GROUP_W), F32)],
    )
    return pl.pallas_call(
        _nsa_slc_kernel,
        out_shape=jax.ShapeDtypeStruct((T, Q_DIM), BF16),
        grid_spec=grid_spec,
        compiler_params=_cparams(("arbitrary", "arbitrary", "arbitrary")),
        name="nsa_slc",
    )(nblk, blist, qT, proj, proj, proj, proj, selb, proj, ocmpT, alibi, _window_pos_cols())


CONV_TS = 512
CONV_HALO = 32
CONV_RC = 64
CONV_CC = 256


def _conv_kernel(a_ref, b_ref, w_ref, cb_ref, g_ref, beta_ref, o_ref, hbuf, ybuf, wrep):
    si = pl.program_id(1)

    @pl.when(si == 0)
    def _():
        hbuf[pl.ds(0, CONV_HALO), :] = jnp.zeros((CONV_HALO, CONV_CH), F32)

    @pl.when(si > 0)
    def _():
        hbuf[pl.ds(0, CONV_HALO), :] = hbuf[pl.ds(CONV_TS, CONV_HALO), :]

    a = a_ref[...].astype(F32)
    bb = b_ref[...].astype(F32)
    hbuf[pl.ds(CONV_HALO, CONV_TS), :] = a * jax.nn.sigmoid(bb)

    for k in range(CONV_WIDTH):
        wrep[pl.ds(k * SUBLANES, SUBLANES), :] = jnp.broadcast_to(w_ref[k:k + 1, :],
                                                                  (SUBLANES, CONV_CH))

    off = CONV_HALO - (CONV_WIDTH - 1)
    groups = CONV_RC // SUBLANES
    for cc in range(CONV_CH // CONV_CC):
        cols = slice(cc * CONV_CC, (cc + 1) * CONV_CC)

        def row_body(r, carry, cols=cols):
            r0 = pl.multiple_of(r * CONV_RC, CONV_RC)
            win = hbuf[pl.ds(r0, CONV_RC + CONV_HALO), cols]
            acc = jnp.zeros((groups, SUBLANES, CONV_CC), F32)
            for res in range(SUBLANES):
                taps = [k for k in range(CONV_WIDTH) if (off + k) % SUBLANES == res]
                span = max(off + k for k in taps) - res + CONV_RC
                shifted = win[res:res + span, :]
                for k in taps:
                    a0 = off + k - res
                    rows = shifted[a0:a0 + CONV_RC, :].reshape(groups, SUBLANES, CONV_CC)
                    acc = acc + rows * wrep[pl.ds(k * SUBLANES, SUBLANES), cols][None]
            ybuf[pl.ds(r0, CONV_RC), cols] = acc.reshape(CONV_RC, CONV_CC)
            return carry

        lax.fori_loop(0, CONV_TS // CONV_RC, row_body, 0)

    y = ybuf[...] + cb_ref[...]
    mu = jnp.mean(y, axis=-1, keepdims=True)
    yc = y - mu
    var = jnp.mean(yc * yc, axis=-1, keepdims=True)
    z = yc * lax.rsqrt(var + LN_EPS) * g_ref[...] + beta_ref[...]
    o_ref[...] = (z * jax.nn.sigmoid(z)).astype(o_ref.dtype)


def _conv(proj, conv_w, conv_b, ln_g, ln_b, B, S):
    T = B * S
    ts = CONV_TS
    assert S % ts == 0
    nS = S // ts
    ca = COL_GLU // CONV_CH
    vec = lambda: pl.BlockSpec((1, CONV_CH), lambda b, s: (0, 0))
    return pl.pallas_call(
        _conv_kernel,
        out_shape=jax.ShapeDtypeStruct((T, CONV_CH), BF16),
        grid=(B, nS),
        in_specs=[pl.BlockSpec((ts, CONV_CH), lambda b, s: (b * nS + s, ca)),
                  pl.BlockSpec((ts, CONV_CH), lambda b, s: (b * nS + s, ca + 1)),
                  pl.BlockSpec((CONV_WIDTH, CONV_CH), lambda b, s: (0, 0)),
                  vec(), vec(), vec()],
        out_specs=pl.BlockSpec((ts, CONV_CH), lambda b, s: (b * nS + s, 0)),
        scratch_shapes=[pltpu.VMEM((CONV_HALO + ts, CONV_CH), F32),
                        pltpu.VMEM((ts, CONV_CH), F32),
                        pltpu.VMEM((CONV_WIDTH * SUBLANES, CONV_CH), F32)],
        compiler_params=_cparams(("arbitrary", "arbitrary")),
        name="conformer_conv",
    )(proj, proj, conv_w, conv_b.reshape(1, -1), ln_g.reshape(1, -1), ln_b.reshape(1, -1))


def _merge_kernel(o_ref, h_ref, wa_ref, wb_ref, ga_ref, gb_ref, out_ref):
    ya = jnp.dot(o_ref[...], wa_ref[...], preferred_element_type=F32)
    yb = jnp.dot(h_ref[...], wb_ref[...], preferred_element_type=F32)
    ga = jax.nn.sigmoid(ga_ref[...].astype(F32))
    gb = jax.nn.sigmoid(gb_ref[...].astype(F32))
    out_ref[...] = (ga * ya + gb * yb).astype(out_ref.dtype)


def _merge(o_nsa, h_conv, wa, wb, proj):
    T = o_nsa.shape[0]
    tm = min(512, T)
    tn = 1024
    nN = D_MODEL // tn
    ga0 = COL_MERGE // tn
    return pl.pallas_call(
        _merge_kernel,
        out_shape=jax.ShapeDtypeStruct((T, D_MODEL), BF16),
        grid=(T // tm, nN),
        in_specs=[pl.BlockSpec((tm, Q_DIM), lambda i, j: (i, 0)),
                  pl.BlockSpec((tm, CONV_CH), lambda i, j: (i, 0)),
                  pl.BlockSpec((Q_DIM, tn), lambda i, j: (0, j)),
                  pl.BlockSpec((CONV_CH, tn), lambda i, j: (0, j)),
                  pl.BlockSpec((tm, tn), lambda i, j: (i, ga0 + j)),
                  pl.BlockSpec((tm, tn), lambda i, j: (i, ga0 + nN + j))],
        out_specs=pl.BlockSpec((tm, tn), lambda i, j: (i, j)),
        compiler_params=_cparams(("arbitrary", "arbitrary")),
        name="merge",
    )(o_nsa, h_conv, wa, wb, proj, proj)


def _layer_norm_rows(y, g, b):
    mu = jnp.mean(y, axis=-1, keepdims=True)
    yc = y - mu
    var = jnp.mean(yc * yc, axis=-1, keepdims=True)
    return yc * lax.rsqrt(var + LN_EPS) * g + b


def _split2(v):
    hi = v.astype(BF16)
    return hi, (v - hi.astype(F32)).astype(BF16)


def _outproj_kernel(mix_ref, w_ref, x_ref, g_ref, b_ref, rw_hi_ref, rw_lo_ref, rb_ref,
                    x1_ref, x1b_ref, ti_ref, tg_ref):
    m = jnp.dot(mix_ref[...], w_ref[...], preferred_element_type=F32)
    x1 = _layer_norm_rows(DEEPNORM_ALPHA * x_ref[...] + m, g_ref[...], b_ref[...])
    x1_ref[...] = x1
    x1b_ref[...] = x1.astype(BF16)
    x_hi, x_lo = _split2(x1)
    logits = (jnp.dot(x_hi, rw_hi_ref[...], preferred_element_type=F32)
              + jnp.dot(x_hi, rw_lo_ref[...], preferred_element_type=F32)
              + jnp.dot(x_lo, rw_hi_ref[...], preferred_element_type=F32)) + rb_ref[...]
    lane = lax.broadcasted_iota(jnp.int32, logits.shape, 1)
    logits = jnp.where(lane < N_EXPERTS, logits, -jnp.inf)
    ti = jnp.zeros(logits.shape, jnp.int32)
    tv = jnp.zeros(logits.shape, F32)
    top0 = None
    den = jnp.zeros((logits.shape[0], 1), F32)
    for k in range(TOP_K):
        mx = jnp.max(logits, axis=-1, keepdims=True)
        idx = jnp.min(jnp.where(logits == mx, lane, LANES), axis=-1, keepdims=True)
        if top0 is None:
            top0 = mx
        e = jnp.exp(mx - top0)
        den = den + e
        ti = jnp.where(lane == k, idx, ti)
        tv = jnp.where(lane == k, e, tv)
        logits = jnp.where(lane == idx, -jnp.inf, logits)
    ti_ref[...] = ti
    tg_ref[...] = tv / den


def _outproj(mix, w_out, x2d, g, b, rw_hi, rw_lo, rb):
    T = mix.shape[0]
    tm = min(256, T)
    full = lambda shape: pl.BlockSpec(shape, lambda i: (0, 0))
    rowb = lambda w: pl.BlockSpec((tm, w), lambda i: (i, 0))
    return pl.pallas_call(
        _outproj_kernel,
        out_shape=(jax.ShapeDtypeStruct((T, D_MODEL), F32),
                   jax.ShapeDtypeStruct((T, D_MODEL), BF16),
                   jax.ShapeDtypeStruct((T, LANES), jnp.int32),
                   jax.ShapeDtypeStruct((T, LANES), F32)),
        grid=(T // tm,),
        in_specs=[rowb(D_MODEL), full((D_MODEL, D_MODEL)), rowb(D_MODEL),
                  full((1, D_MODEL)), full((1, D_MODEL)),
                  full((D_MODEL, LANES)), full((D_MODEL, LANES)), full((1, LANES))],
        out_specs=(rowb(D_MODEL), rowb(D_MODEL), rowb(LANES), rowb(LANES)),
        compiler_params=_cparams(("arbitrary",)),
        name="outproj_ln_router",
    )(mix, w_out, x2d, g, b, rw_hi, rw_lo, rb)


def _moe_up_kernel(be_ref, nused_ref, x_ref, wg_ref, wu_ref, bg_ref, bu_ref, h_ref,
                   wgb_ref, wub_ref):
    r = pl.program_id(1)
    prev = be_ref[jnp.maximum(r - 1, 0)]
    fresh = (r == 0) | (be_ref[r] != prev)

    @pl.when(fresh)
    def _():
        wgb_ref[...] = wg_ref[0].astype(BF16)
        wub_ref[...] = wu_ref[0].astype(BF16)

    @pl.when(r < nused_ref[0])
    def _():
        x = x_ref[...]
        gt = jnp.dot(x, wgb_ref[...], preferred_element_type=F32) + bg_ref[0]
        up = jnp.dot(x, wub_ref[...], preferred_element_type=F32) + bu_ref[0]
        gt = jnp.minimum(gt, SWIGLU_LIMIT)
        up = jnp.clip(up, -SWIGLU_LIMIT, SWIGLU_LIMIT)
        h = gt * jax.nn.sigmoid(SWIGLU_ALPHA * gt) * (up + 1.0)
        h_ref[...] = h.astype(h_ref.dtype)


def _moe_row_maps(nblk):
    rd = lambda r, nu: jnp.minimum(r, nu[0] - 1)
    wr = lambda r, nu: jnp.where(r < nu[0], r, nblk)
    return rd, wr


def _moe_up(blk_e, n_used, xs, w_gate, w_up, b_gate, b_up):
    P = xs.shape[0]
    bm, tf = MOE_BLOCK, MOE_UP_TILE
    nblk = P // bm
    rd, wr = _moe_row_maps(nblk)
    wspec = pl.BlockSpec((1, D_MODEL, tf), lambda f, r, be, nu: (be[r], 0, f))
    bspec = pl.BlockSpec((1, 1, tf), lambda f, r, be, nu: (be[r], 0, f))
    grid_spec = pltpu.PrefetchScalarGridSpec(
        num_scalar_prefetch=2,
        grid=(D_FF // tf, nblk),
        in_specs=[pl.BlockSpec((bm, D_MODEL), lambda f, r, be, nu: (rd(r, nu), 0)),
                  wspec, wspec, bspec, bspec],
        out_specs=pl.BlockSpec((bm, tf), lambda f, r, be, nu: (wr(r, nu), f)),
        scratch_shapes=[pltpu.VMEM((D_MODEL, tf), BF16), pltpu.VMEM((D_MODEL, tf), BF16)],
    )
    return pl.pallas_call(
        _moe_up_kernel,
        out_shape=jax.ShapeDtypeStruct((P + bm, D_FF), BF16),
        grid_spec=grid_spec,
        compiler_params=_cparams(("arbitrary", "arbitrary")),
        name="moe_up",
    )(blk_e, n_used, xs, w_gate, w_up, b_gate.reshape(N_EXPERTS, 1, D_FF),
      b_up.reshape(N_EXPERTS, 1, D_FF))


def _moe_down_kernel(be_ref, nused_ref, h_ref, wd_ref, bd_ref, o_ref, wdb_ref):
    r = pl.program_id(0)
    prev = be_ref[jnp.maximum(r - 1, 0)]
    fresh = (r == 0) | (be_ref[r] != prev)

    @pl.when(fresh)
    def _():
        wdb_ref[...] = wd_ref[0].astype(BF16)

    @pl.when(r < nused_ref[0])
    def _():
        y = jnp.dot(h_ref[...], wdb_ref[...], preferred_element_type=F32) + bd_ref[0]
        o_ref[...] = y.astype(o_ref.dtype)


def _moe_down(blk_e, n_used, h, w_down, b_down):
    bm = MOE_BLOCK
    nblk = h.shape[0] // bm - 1
    rd, wr = _moe_row_maps(nblk)
    grid_spec = pltpu.PrefetchScalarGridSpec(
        num_scalar_prefetch=2,
        grid=(nblk,),
        in_specs=[pl.BlockSpec((bm, D_FF), lambda r, be, nu: (rd(r, nu), 0)),
                  pl.BlockSpec((1, D_FF, D_MODEL), lambda r, be, nu: (be[r], 0, 0)),
                  pl.BlockSpec((1, 1, D_MODEL), lambda r, be, nu: (be[r], 0, 0))],
        out_specs=pl.BlockSpec((bm, D_MODEL), lambda r, be, nu: (wr(r, nu), 0)),
        scratch_shapes=[pltpu.VMEM((D_FF, D_MODEL), BF16)],
    )
    return pl.pallas_call(
        _moe_down_kernel,
        out_shape=jax.ShapeDtypeStruct(((nblk + 1) * bm, D_MODEL), BF16),
        grid_spec=grid_spec,
        compiler_params=_cparams(("arbitrary",)),
        name="moe_down",
    )(blk_e, n_used, h, w_down, b_down.reshape(N_EXPERTS, 1, D_MODEL))


def _final_kernel(x1_ref, *refs):
    y_refs, (tg_ref, g_ref, b_ref, o_ref) = refs[:TOP_K], refs[TOP_K:]
    tg = tg_ref[...]
    f = tg[:, 0:1] * y_refs[0][...].astype(F32)
    for k in range(1, TOP_K):
        f = f + tg[:, k:k + 1] * y_refs[k][...].astype(F32)
    o_ref[...] = _layer_norm_rows(DEEPNORM_ALPHA * x1_ref[...] + f, g_ref[...], b_ref[...])


def _final(x1, ys, tg, g, b):
    T = x1.shape[0]
    tm = min(256, T)
    rows = pl.BlockSpec((tm, D_MODEL), lambda i: (i, 0))
    vec = pl.BlockSpec((1, D_MODEL), lambda i: (0, 0))
    return pl.pallas_call(
        _final_kernel,
        out_shape=jax.ShapeDtypeStruct((T, D_MODEL), F32),
        grid=(T // tm,),
        in_specs=[rows] + [rows] * TOP_K + [pl.BlockSpec((tm, LANES), lambda i: (i, 0)), vec, vec],
        out_specs=rows,
        compiler_params=_cparams(("arbitrary",)),
        name="combine_ln",
    )(x1, *ys, tg, g, b)


def _prep_w_in(w_in):
    splits = np.cumsum([Q_DIM] + [KV_DIM] * 6 + [NSA_GATE_DIM, 2 * CONV_CH, 2 * D_MODEL])
    q = w_in[:, :splits[0]] * (LOG2E / math.sqrt(HEAD_DIM))
    kv = w_in[:, splits[0]:splits[6]]
    gate = w_in[:, splits[6]:splits[7]]
    glu = w_in[:, splits[7]:splits[8]]
    merge = w_in[:, splits[8]:splits[9]]
    per_g = HEADS_PER_GROUP * 3
    gate_tiles = [jnp.pad(gate[:, g * per_g:(g + 1) * per_g], ((0, 0), (0, LANES - per_g)))
                  for g in range(N_KV_GROUPS)]
    return jnp.concatenate([q, glu, merge, kv] + gate_tiles, axis=1).astype(BF16)


def _mixer(x2d, B, S, w_in, cmp_pe, cmp_w1, cmp_b1, cmp_w2, cmp_b2, w_nsa_proj,
           conv_w, conv_b, conv_ln_g, conv_ln_b, w_conv_proj):
    T = B * S
    G = N_KV_GROUPS
    proj = _inproj(x2d, _prep_w_in(w_in))

    kcvc = _compress(proj, cmp_pe.reshape(2, 1, CMP_BLOCK * HEAD_DIM),
                     cmp_w1.astype(BF16), cmp_b1.reshape(2, 1, HEAD_DIM),
                     cmp_w2.astype(BF16), cmp_b2.reshape(2, 1, HEAD_DIM), B, S)

    C = S // Q_BLOCK
    qT = proj[:, COL_Q:COL_Q + Q_DIM].reshape(B * C, Q_BLOCK, G, HEADS_PER_GROUP, HEAD_DIM)
    qT = qT.transpose(0, 2, 4, 3, 1).reshape(B * C, G, HEAD_DIM, GROUP_W)
    alibi = _alibi_rows().astype(BF16)
    ocmpT, selb = _nsa_cmp(qT, kcvc, alibi, B, S)
    nsp = selb.shape[2]
    fl = jnp.max(selb, axis=3) > 0.5 * NEG_INF
    fl = fl.reshape(B, C, G, nsp).transpose(0, 2, 1, 3)
    past = (jnp.arange(nsp, dtype=jnp.int32)[None, :]
            < (Q_BLOCK // SEL_BLOCK) * jnp.arange(C, dtype=jnp.int32)[:, None])
    fl = (fl & past[None, None]).reshape(B * G * C, nsp)
    nblk = jnp.sum(fl, axis=1).astype(jnp.int32)
    blist = jnp.argsort(jnp.logical_not(fl), axis=1, stable=True).astype(jnp.int32)
    o_nsa = _nsa_slc(proj, qT, selb, nblk, blist.reshape(-1), ocmpT, alibi, B, S)

    h_conv = _conv(proj, conv_w, conv_b, conv_ln_g, conv_ln_b, B, S)
    return _merge(o_nsa, h_conv, w_nsa_proj.astype(BF16), w_conv_proj.astype(BF16), proj)


def _moe(x1b, top_i, w_gate, b_gate, w_up, b_up, w_down, b_down):
    T = x1b.shape[0]
    A = T * TOP_K
    bm = MOE_BLOCK
    flat_e = top_i[:, :TOP_K].reshape(A)
    onehot = (flat_e[:, None] == jnp.arange(N_EXPERTS, dtype=jnp.int32)[None, :]).astype(jnp.int32)
    csum = jnp.cumsum(onehot, axis=0)
    rank = jnp.sum(onehot * csum, axis=1) - 1
    counts = csum[-1]
    padded = (counts + bm - 1) // bm * bm
    pad_end = jnp.cumsum(padded)
    pad_start = pad_end - padded
    dest = pad_start[flat_e] + rank
    P = -(-A // bm) * bm + N_EXPERTS * bm
    nblk = P // bm
    flat_tok = jnp.arange(A, dtype=jnp.int32) // TOP_K
    buf_tok = jnp.zeros((P,), jnp.int32).at[dest].set(flat_tok)
    blk_start = jnp.arange(nblk, dtype=jnp.int32) * bm
    blk_e = jnp.minimum(jnp.sum(blk_start[:, None] >= pad_end[None, :], axis=1),
                        N_EXPERTS - 1).astype(jnp.int32)
    n_used = (pad_end[-1] // bm).astype(jnp.int32).reshape(1)
    xs = x1b[buf_tok]
    h = _moe_up(blk_e, n_used, xs, w_gate, w_up, b_gate, b_up)
    out = _moe_down(blk_e, n_used, h, w_down, b_down)
    dest_k = dest.reshape(T, TOP_K)
    return [out[dest_k[:, k]] for k in range(TOP_K)]


def kernel(x, w_in, cmp_pe, cmp_w1, cmp_b1, cmp_w2, cmp_b2, w_nsa_proj, conv_w, conv_b, conv_ln_g, conv_ln_b, w_conv_proj, w_out, ln1_g, ln1_b, router_w, router_b, w_gate, b_gate, w_up, b_up, w_down, b_down, ln2_g, ln2_b):
    B, S, D = x.shape
    T = B * S
    x2d = x.reshape(T, D)
    for l in range(DEPTH):
        mix = _mixer(x2d, B, S, w_in[l], cmp_pe[l], cmp_w1[l], cmp_b1[l], cmp_w2[l], cmp_b2[l],
                     w_nsa_proj[l], conv_w[l], conv_b[l], conv_ln_g[l], conv_ln_b[l],
                     w_conv_proj[l])
        rw = jnp.pad(router_w[l], ((0, 0), (0, LANES - N_EXPERTS)))
        rw_hi = rw.astype(BF16)
        rw_lo = (rw - rw_hi.astype(F32)).astype(BF16)
        rb = jnp.pad(router_b[l], (0, LANES - N_EXPERTS)).reshape(1, LANES)
        x1, x1b, top_i, top_g = _outproj(
            mix, w_out[l].astype(BF16), x2d, ln1_g[l].reshape(1, D), ln1_b[l].reshape(1, D),
            rw_hi, rw_lo, rb)
        y4 = _moe(x1b, top_i, w_gate[l], b_gate[l], w_up[l], b_up[l], w_down[l], b_down[l])
        x2d = _final(x1, y4, top_g, ln2_g[l].reshape(1, D), ln2_b[l].reshape(1, D))
    return x2d.reshape(B, S, D)
```

```python
import functools
import math

import numpy as np
import jax
import jax.numpy as jnp
from jax import lax
from jax.experimental import pallas as pl
from jax.experimental.pallas import tpu as pltpu

D_MODEL = 2048
N_HEADS = 16
N_KV_GROUPS = 2
HEADS_PER_GROUP = N_HEADS // N_KV_GROUPS
HEAD_DIM = 128
CMP_BLOCK = 32
CMP_STRIDE = 16
SEL_BLOCK = 64
SEL_SHIFT = 6
SEL_TOP_N = 16
WINDOW = 512
Q_BLOCK = 128
N_OVERLAP = (SEL_BLOCK + CMP_BLOCK) // CMP_STRIDE - 1
FORCE_BONUS = 1.0e4
CONV_CH = D_MODEL // 2
CONV_WIDTH = 31
N_EXPERTS = 32
TOP_K = 4
D_FF = D_MODEL
SWIGLU_LIMIT = 7.0
SWIGLU_ALPHA = 1.702
LN_EPS = 1e-5
DEPTH = 1
DEEPNORM_ALPHA = (2 * DEPTH) ** 0.25
NEG_INF = -1e30
TINY = 1e-30

Q_DIM = N_HEADS * HEAD_DIM
KV_DIM = N_KV_GROUPS * HEAD_DIM
NSA_GATE_DIM = N_HEADS * 3

LANES = 128
SUBLANES = 8
VMEM_LIMIT = 56 * 1024 * 1024

COL_Q = 0
COL_GLU = COL_Q + Q_DIM
COL_MERGE = COL_GLU + 2 * CONV_CH
COL_KC = COL_MERGE + 2 * D_MODEL
COL_VC = COL_KC + KV_DIM
COL_KS = COL_VC + KV_DIM
COL_VS = COL_KS + KV_DIM
COL_KW = COL_VS + KV_DIM
COL_VW = COL_KW + KV_DIM
COL_GATE = COL_VW + KV_DIM
PROJ_W = COL_GATE + N_KV_GROUPS * LANES

BF16 = jnp.bfloat16
F32 = jnp.float32

MOE_BLOCK = 256
MOE_UP_TILE = 1024
MOE_SEGMENTS = 4

_ALIBI = np.exp2(-8.0 * np.arange(1, N_HEADS + 1, dtype=np.float32) / N_HEADS).astype(np.float32)
_ALIBI = _ALIBI.reshape(N_KV_GROUPS, HEADS_PER_GROUP)


def _cparams(sem, vmem=VMEM_LIMIT):
    return pltpu.CompilerParams(dimension_semantics=sem, vmem_limit_bytes=vmem)


def _inproj_kernel(x_ref, w_ref, o_ref, xb_ref):
    @pl.when(pl.program_id(1) == 0)
    def _():
        xb_ref[...] = x_ref[...].astype(BF16)

    o_ref[...] = jnp.dot(xb_ref[...], w_ref[...],
                         preferred_element_type=F32).astype(o_ref.dtype)


def _inproj(x2d, w_p):
    T, D = x2d.shape
    N = w_p.shape[1]
    tm = min(512, T)
    tn = N // 3
    assert T % tm == 0 and N % tn == 0 and tn % (2 * LANES) == 0
    return pl.pallas_call(
        _inproj_kernel,
        out_shape=jax.ShapeDtypeStruct((T, N), BF16),
        grid=(T // tm, N // tn),
        in_specs=[pl.BlockSpec((tm, D), lambda i, j: (i, 0)),
                  pl.BlockSpec((D, tn), lambda i, j: (0, j))],
        out_specs=pl.BlockSpec((tm, tn), lambda i, j: (i, j)),
        scratch_shapes=[pltpu.VMEM((tm, D), BF16)],
        compiler_params=_cparams(("arbitrary", "arbitrary")),
        name="inproj",
    )(x2d, w_p)


def _gelu_tanh(x):
    c = math.sqrt(2.0 / math.pi)
    return 0.5 * x * (1.0 + jnp.tanh(c * (x + 0.044715 * (x * x * x))))


def _compress_kernel(x_ref, pe_ref, w1_ref, b1_ref, w2_ref, b2_ref, o_ref, xf_ref):
    S = x_ref.shape[0]
    n16 = S // CMP_STRIDE
    xf_ref[...] = x_ref[...].astype(F32)
    top = jnp.zeros((n16, HEAD_DIM), F32)
    bot = jnp.zeros((n16, HEAD_DIM), F32)
    for j in range(CMP_STRIDE):
        xj = xf_ref[pl.ds(j, n16, stride=CMP_STRIDE), :]
        lo, hi = j * HEAD_DIM, (CMP_STRIDE + j) * HEAD_DIM
        top = top + jnp.dot((xj + pe_ref[0, :, lo:lo + HEAD_DIM]).astype(BF16),
                            w1_ref[0, lo:lo + HEAD_DIM, :], preferred_element_type=F32)
        bot = bot + jnp.dot((xj + pe_ref[0, :, hi:hi + HEAD_DIM]).astype(BF16),
                            w1_ref[0, hi:hi + HEAD_DIM, :], preferred_element_type=F32)
    pre = top + pltpu.roll(bot, n16 - 1, 0) + b1_ref[0]
    h = _gelu_tanh(pre)
    o = jnp.dot(h.astype(BF16), w2_ref[0], preferred_element_type=F32) + b2_ref[0]
    o_ref[0, 0] = o.astype(o_ref.dtype)


def _compress(proj, pe, w1, b1, w2, b2, B, S):
    n_slot = 2 * N_KV_GROUPS
    n16 = S // CMP_STRIDE
    wide = CMP_BLOCK * HEAD_DIM
    kv = lambda s, b: (s // N_KV_GROUPS, 0, 0)
    return pl.pallas_call(
        _compress_kernel,
        out_shape=jax.ShapeDtypeStruct((n_slot, B, n16, HEAD_DIM), BF16),
        grid=(n_slot, B),
        in_specs=[pl.BlockSpec((S, HEAD_DIM), lambda s, b: (b, COL_KC // HEAD_DIM + s)),
                  pl.BlockSpec((1, 1, wide), kv),
                  pl.BlockSpec((1, wide, HEAD_DIM), kv),
                  pl.BlockSpec((1, 1, HEAD_DIM), kv),
                  pl.BlockSpec((1, HEAD_DIM, HEAD_DIM), kv),
                  pl.BlockSpec((1, 1, HEAD_DIM), kv)],
        out_specs=pl.BlockSpec((1, 1, n16, HEAD_DIM), lambda s, b: (s, b, 0, 0)),
        scratch_shapes=[pltpu.VMEM((S, HEAD_DIM), F32)],
        compiler_params=_cparams(("arbitrary", "arbitrary")),
        name="compress",
    )(proj, pe, w1, b1, w2, b2)


LOG2E = 1.4426950408889634
KAUG = 2 * HEAD_DIM
ROW_ALIBI = 0
ROW_SLOT = 16
SLC_GROUP = 8
GROUP_W = HEADS_PER_GROUP * Q_BLOCK
WIN_TILES = WINDOW // Q_BLOCK
WIN_KEYS = WINDOW + Q_BLOCK


def _window_pos_cols():
    rel = np.arange(WIN_KEYS) - WINDOW
    a = (rel // SEL_BLOCK) * SEL_BLOCK
    r = rel - a
    cols = np.zeros((WIN_KEYS, LANES), np.float32)
    cols[:, ROW_ALIBI:ROW_ALIBI + 3] = a[:, None]
    cols[:, ROW_ALIBI + 3:ROW_ALIBI + 6] = r[:, None]
    return jnp.asarray(cols, BF16)


def _alibi_rows():
    s2 = jnp.asarray(_ALIBI * np.float32(LOG2E), F32)
    hi = s2.astype(BF16)
    r1 = s2 - hi.astype(F32)
    mid = r1.astype(BF16)
    lo = (r1 - mid.astype(F32)).astype(BF16)
    trip = jnp.stack([hi, mid, lo, hi, mid, lo], axis=1)
    rows = jnp.pad(trip, ((0, 0), (0, 16 - 6), (0, 0)))
    return jnp.repeat(rows, Q_BLOCK, axis=2)


def _dot_ta(a, b):
    return lax.dot_general(a, b, (((0,), (0,)), ((), ())), preferred_element_type=F32)


def _fill_qaug(qaug, q_ref, alibi_ref):
    for h in range(HEADS_PER_GROUP):
        qaug[pl.ds(0, HEAD_DIM), pl.ds(h * Q_BLOCK, Q_BLOCK)] = (
            q_ref[:, h * HEAD_DIM:(h + 1) * HEAD_DIM].T)
    qaug[pl.ds(HEAD_DIM, 16), :] = alibi_ref[0]
    qaug[pl.ds(HEAD_DIM + 16, HEAD_DIM - 16), :] = jnp.zeros((HEAD_DIM - 16, GROUP_W), BF16)


def _pos_cols(a_val, r_val, lane, slot_lane=None):
    base = jnp.where((lane >= ROW_ALIBI + 3) & (lane < ROW_ALIBI + 6), r_val, 0.0)
    if slot_lane is not None:
        base = jnp.where(lane == slot_lane, 1.0, base)
    return jnp.where(lane < ROW_ALIBI + 3, a_val, base)


def _nsa_cmp_kernel(q_ref, kc_ref, vc_ref, alibi_ref, ocmpT_ref, selb_ref, qaug, *, n_selb):
    c = pl.program_id(2)
    _fill_qaug(qaug, q_ref, alibi_ref)
    n_vis = c * (Q_BLOCK // CMP_STRIDE) + (Q_BLOCK - CMP_BLOCK) // CMP_STRIDE + 1
    tiles = kc_ref.shape[2] // LANES
    for i in range(1, tiles + 1):
        cond = n_vis > (i - 1) * LANES
        if i < tiles:
            cond = cond & (n_vis <= i * LANES)
        pl.when(cond)(functools.partial(
            _nsa_cmp_rows, c, i * LANES, kc_ref, vc_ref, ocmpT_ref, selb_ref, qaug, n_selb))


def _nsa_cmp_rows(c, ncp, kc_ref, vc_ref, ocmpT_ref, selb_ref, qaug, n_selb):
    nsp = selb_ref.shape[2]

    n_i = lax.broadcasted_iota(jnp.int32, (ncp, LANES), 0)
    lane = lax.broadcasted_iota(jnp.int32, (ncp, LANES), 1)
    end_rel = n_i * CMP_STRIDE + (CMP_BLOCK - 1) - c * Q_BLOCK
    a_val = jnp.left_shift(jnp.right_shift(end_rel, SEL_SHIFT), SEL_SHIFT).astype(F32)
    r_val = (end_rel & (SEL_BLOCK - 1)).astype(F32)
    pc = _pos_cols(a_val, r_val, lane).astype(BF16)
    s = jnp.dot(jnp.concatenate([kc_ref[0, 0, pl.ds(0, ncp), :], pc], axis=1), qaug[...],
                preferred_element_type=F32)

    tq = lax.broadcasted_iota(jnp.int32, (1, Q_BLOCK), 1)
    t_row = c * Q_BLOCK + tq
    n_col = lax.broadcasted_iota(jnp.int32, (ncp, Q_BLOCK), 0)
    mask_bias = jnp.where(n_col * CMP_STRIDE + (CMP_BLOCK - 1) <= t_row, 0.0, NEG_INF)
    has_valid = t_row >= CMP_BLOCK - 1

    imp = jnp.zeros((ncp, Q_BLOCK), F32)
    es, rls = [], []
    for h in range(HEADS_PER_GROUP):
        sh = s[:, h * Q_BLOCK:(h + 1) * Q_BLOCK] + mask_bias
        m = jnp.max(sh, axis=0, keepdims=True)
        e = jnp.exp2(sh - m)
        l = jnp.sum(e, axis=0, keepdims=True)
        rl = jnp.where(has_valid, 1.0 / jnp.maximum(l, TINY), 0.0)
        imp = imp + e * rl
        es.append(e.astype(BF16))
        rls.append(rl)
    oT = _dot_ta(vc_ref[0, 0, pl.ds(0, ncp), :], jnp.concatenate(es, axis=1))
    ocmpT_ref[0, 0] = oT * jnp.concatenate(rls, axis=1)

    j_i = lax.broadcasted_iota(jnp.int32, (nsp, ncp), 0)
    n_j = lax.broadcasted_iota(jnp.int32, (nsp, ncp), 1)
    lo = j_i * (SEL_BLOCK // CMP_STRIDE) - (CMP_BLOCK // CMP_STRIDE) + 1
    ovl = ((n_j >= lo) & (n_j < lo + N_OVERLAP)).astype(BF16)
    i_hi = imp.astype(BF16)
    r1 = imp - i_hi.astype(F32)
    i_mid = r1.astype(BF16)
    i_lo = (r1 - i_mid.astype(F32)).astype(BF16)
    imp_sel = (jnp.dot(ovl, i_hi, preferred_element_type=F32)
               + jnp.dot(ovl, i_mid, preferred_element_type=F32)
               + jnp.dot(ovl, i_lo, preferred_element_type=F32))

    blk = lax.broadcasted_iota(jnp.int32, (nsp, Q_BLOCK), 0)
    cur = jnp.right_shift(t_row, SEL_SHIFT)
    valid = blk * SEL_BLOCK <= t_row
    forced = (blk == 0) | (blk == cur) | (blk == cur - 1)
    score = jnp.where(valid, imp_sel + jnp.where(forced, FORCE_BONUS, 0.0), -1.0)
    score = jnp.where(blk < n_selb, score, -3.0)
    selb = jnp.full((nsp, Q_BLOCK), NEG_INF, F32)
    for _ in range(min(SEL_TOP_N, n_selb)):
        mx = jnp.max(score, axis=0, keepdims=True)
        first = jnp.min(jnp.where(score == mx, blk, nsp), axis=0, keepdims=True)
        pick = blk == first
        selb = jnp.where(pick, 0.0, selb)
        score = jnp.where(pick, -2.0, score)
    selb_ref[0, 0] = selb


def _nsa_cmp(proj, kcvc, alibi, B, S):
    C = S // Q_BLOCK
    G = N_KV_GROUPS
    ncp = kcvc.shape[2]
    n_selb = S // SEL_BLOCK
    nsp = max(LANES, n_selb)
    kern = functools.partial(_nsa_cmp_kernel, n_selb=n_selb)
    return pl.pallas_call(
        kern,
        out_shape=(jax.ShapeDtypeStruct((B * C, G, HEAD_DIM, GROUP_W), F32),
                   jax.ShapeDtypeStruct((B * C, G, nsp, Q_BLOCK), F32)),
        grid=(B, G, C),
        in_specs=[pl.BlockSpec((Q_BLOCK, HEADS_PER_GROUP * HEAD_DIM),
                               lambda b, g, c: (b * C + c, COL_Q // (HEADS_PER_GROUP * HEAD_DIM) + g)),
                  pl.BlockSpec((1, 1, ncp, HEAD_DIM), lambda b, g, c: (g, b, 0, 0)),
                  pl.BlockSpec((1, 1, ncp, HEAD_DIM), lambda b, g, c: (G + g, b, 0, 0)),
                  pl.BlockSpec((1, 16, GROUP_W), lambda b, g, c: (g, 0, 0))],
        out_specs=(pl.BlockSpec((1, 1, HEAD_DIM, GROUP_W), lambda b, g, c: (b * C + c, g, 0, 0)),
                   pl.BlockSpec((1, 1, nsp, Q_BLOCK), lambda b, g, c: (b * C + c, g, 0, 0))),
        scratch_shapes=[pltpu.VMEM((KAUG, GROUP_W), BF16)],
        compiler_params=_cparams(("arbitrary", "arbitrary", "arbitrary")),
        name="nsa_cmp",
    )(proj, kcvc, kcvc, alibi)


def _softmax_tile(st, qaug, k_tile, pc, v_tile, bias, first):
    m_ref, l_ref, acc_ref = st
    kaug = jnp.concatenate([k_tile, pc.astype(BF16)], axis=1)
    s = jnp.dot(kaug, qaug[...], preferred_element_type=F32)
    if bias is not None:
        s = jnp.concatenate([s[:, h * Q_BLOCK:(h + 1) * Q_BLOCK] + bias
                             for h in range(HEADS_PER_GROUP)], axis=1)
    mx = jnp.max(s, axis=0, keepdims=True)
    if first:
        m_new = mx
    else:
        m_old = m_ref[pl.ds(0, 1), :]
        m_new = jnp.maximum(m_old, mx)
    p = jnp.exp2(s - m_new)
    ps = jnp.sum(p, axis=0, keepdims=True)
    pv = _dot_ta(v_tile, p.astype(BF16))
    if first:
        l_ref[pl.ds(0, 1), :] = ps
        acc_ref[...] = pv
    else:
        alpha = jnp.exp2(m_old - m_new)
        l_ref[pl.ds(0, 1), :] = alpha * l_ref[pl.ds(0, 1), :] + ps
        acc_ref[...] = alpha * acc_ref[...] + pv
    m_ref[pl.ds(0, 1), :] = m_new


def _bias_rows(s, bias):
    return jnp.concatenate([s[:, h * Q_BLOCK:(h + 1) * Q_BLOCK] + bias
                            for h in range(HEADS_PER_GROUP)], axis=1)


def _nsa_slc_kernel(nblk_ref, blist_ref, q_ref, ks_ref, vs_ref, kw_ref, vw_ref,
                    selb_ref, gate_ref, ocmpT_ref, alibi_ref, wpc_ref, o_ref,
                    qaug, m_ref, l_ref, acc_ref, owin_ref):
    b = pl.program_id(0)
    g = pl.program_id(1)
    c = pl.program_id(2)
    n_chunk = pl.num_programs(2)
    nsp = selb_ref.shape[2]
    row = (b * N_KV_GROUPS + g) * n_chunk + c
    n_act = nblk_ref[row]
    st = (m_ref, l_ref, acc_ref)
    _fill_qaug(qaug, q_ref, alibi_ref)

    lane = lax.broadcasted_iota(jnp.int32, (SEL_BLOCK, LANES), 1)
    r_val = lax.broadcasted_iota(jnp.int32, (SEL_BLOCK, LANES), 0).astype(F32)
    ki = lax.broadcasted_iota(jnp.int32, (Q_BLOCK, Q_BLOCK), 0)
    qi = lax.broadcasted_iota(jnp.int32, (Q_BLOCK, Q_BLOCK), 1)
    causal_bias = jnp.where(ki <= qi, 0.0, NEG_INF)
    tail_bias = jnp.where(ki > qi, 0.0, NEG_INF)

    def pair_cols(first_block_rel):
        return jnp.concatenate(
            [_pos_cols(float(SEL_BLOCK * (first_block_rel + i)), r_val, lane) for i in range(2)], axis=0)

    def finish():
        return acc_ref[...] / jnp.maximum(l_ref[pl.ds(0, 1), :], TINY)

    n_wt = WIN_TILES
    base = pl.multiple_of(c * Q_BLOCK, Q_BLOCK)

    @pl.when(c >= n_wt)
    def _():
        start = pl.multiple_of((c - n_wt) * Q_BLOCK, Q_BLOCK)
        kaug = jnp.concatenate([kw_ref[pl.ds(start, WIN_KEYS), :], wpc_ref[...]], axis=1)
        s = jnp.dot(kaug, qaug[...], preferred_element_type=F32)
        s = jnp.concatenate([_bias_rows(s[:Q_BLOCK], tail_bias), s[Q_BLOCK:WINDOW],
                             _bias_rows(s[WINDOW:], causal_bias)], axis=0)
        p = jnp.exp2(s - jnp.max(s, axis=0, keepdims=True))
        l = jnp.sum(p, axis=0, keepdims=True)
        pv = _dot_ta(vw_ref[pl.ds(start, WIN_KEYS), :], p.astype(BF16))
        owin_ref[...] = pv / jnp.maximum(l, TINY)

    @pl.when(c < n_wt)
    def _():
        _softmax_tile(st, qaug, kw_ref[pl.ds(base, Q_BLOCK), :], pair_cols(0),
                      vw_ref[pl.ds(base, Q_BLOCK), :], causal_bias, True)
        for w in range(1, n_wt):
            @pl.when(c - n_wt + w >= 0)
            def _(w=w):
                start = pl.multiple_of((c - n_wt + w) * Q_BLOCK, Q_BLOCK)
                _softmax_tile(st, qaug, kw_ref[pl.ds(start, Q_BLOCK), :],
                              pair_cols(2 * (w - n_wt)), vw_ref[pl.ds(start, Q_BLOCK), :],
                              None, False)
        owin_ref[...] = finish()

    _softmax_tile(st, qaug, ks_ref[pl.ds(base, Q_BLOCK), :], pair_cols(0),
                  vs_ref[pl.ds(base, Q_BLOCK), :], causal_bias, True)
    sub = lax.broadcasted_iota(jnp.int32, (16, GROUP_W), 0)

    def slc_body(i, carry):
        ks_t, vs_t, pcs = [], [], []
        slot_rows = jnp.zeros((16, GROUP_W), F32)
        for k in range(SLC_GROUP):
            pos = i * SLC_GROUP + k
            live = pos < n_act
            j = jnp.where(live, blist_ref[row * nsp + jnp.minimum(pos, nsp - 1)], 0)
            off = pl.multiple_of(j * SEL_BLOCK, SEL_BLOCK)
            ks_t.append(ks_ref[pl.ds(off, SEL_BLOCK), :])
            vs_t.append(vs_ref[pl.ds(off, SEL_BLOCK), :])
            a_val = ((j - 2 * c) * SEL_BLOCK).astype(F32)
            pcs.append(_pos_cols(a_val, r_val, lane, slot_lane=ROW_SLOT + k))
            rowk = selb_ref[0, 0, pl.ds(j, 1), :]
            rowk = jnp.where(live, rowk, NEG_INF)
            slot_rows = jnp.where(sub == k, jnp.tile(rowk, (1, HEADS_PER_GROUP)), slot_rows)
        qaug[pl.ds(HEAD_DIM + ROW_SLOT, 16), :] = slot_rows.astype(BF16)
        _softmax_tile(st, qaug, jnp.concatenate(ks_t, axis=0), jnp.concatenate(pcs, axis=0),
                      jnp.concatenate(vs_t, axis=0), None, False)
        return carry

    lax.fori_loop(0, (n_act + SLC_GROUP - 1) // SLC_GROUP, slc_body, 0)
    o_slcT = finish()

    gT = jax.nn.sigmoid(gate_ref[...].astype(F32).T)
    o_cmpT = ocmpT_ref[0, 0]
    o_winT = owin_ref[...]
    outs = []
    for h in range(HEADS_PER_GROUP):
        hs = slice(h * Q_BLOCK, (h + 1) * Q_BLOCK)
        oT = (gT[3 * h:3 * h + 1, :] * o_cmpT[:, hs] + gT[3 * h + 1:3 * h + 2, :] * o_slcT[:, hs]
              + gT[3 * h + 2:3 * h + 3, :] * o_winT[:, hs])
        outs.append(oT.T)
    o_ref[...] = jnp.concatenate(outs, axis=1).astype(o_ref.dtype)


def _nsa_slc(proj, selb, nblk, blist, ocmpT, alibi, B, S):
    T = B * S
    C = S // Q_BLOCK
    G = N_KV_GROUPS
    nsp = selb.shape[2]
    gw = HEADS_PER_GROUP * HEAD_DIM
    kvspec = lambda col: pl.BlockSpec(
        (S, HEAD_DIM), lambda b, g, c, *_: (b, col // HEAD_DIM + g))
    tspec = lambda rows: pl.BlockSpec((1, 1, rows, GROUP_W), lambda b, g, c, *_: (b * C + c, g, 0, 0))
    grid_spec = pltpu.PrefetchScalarGridSpec(
        num_scalar_prefetch=2,
        grid=(B, G, C),
        in_specs=[pl.BlockSpec((Q_BLOCK, gw), lambda b, g, c, *_: (b * C + c, COL_Q // gw + g)),
                  kvspec(COL_KS), kvspec(COL_VS), kvspec(COL_KW), kvspec(COL_VW),
                  pl.BlockSpec((1, 1, nsp, Q_BLOCK), lambda b, g, c, *_: (b * C + c, g, 0, 0)),
                  pl.BlockSpec((Q_BLOCK, LANES), lambda b, g, c, *_: (b * C + c, COL_GATE // LANES + g)),
                  tspec(HEAD_DIM),
                  pl.BlockSpec((1, 16, GROUP_W), lambda b, g, c, *_: (g, 0, 0)),
                  pl.BlockSpec((WIN_KEYS, LANES), lambda b, g, c, *_: (0, 0))],
        out_specs=pl.BlockSpec((Q_BLOCK, gw), lambda b, g, c, *_: (b * C + c, g)),
        scratch_shapes=[pltpu.VMEM((KAUG, GROUP_W), BF16),
                        pltpu.VMEM((SUBLANES, GROUP_W), F32),
                        pltpu.VMEM((SUBLANES, GROUP_W), F32),
                        pltpu.VMEM((HEAD_DIM, GROUP_W), F32),
                        pltpu.VMEM((HEAD_DIM, GROUP_W), F32)],
    )
    return pl.pallas_call(
        _nsa_slc_kernel,
        out_shape=jax.ShapeDtypeStruct((T, Q_DIM), BF16),
        grid_spec=grid_spec,
        compiler_params=_cparams(("arbitrary", "arbitrary", "arbitrary")),
        name="nsa_slc",
    )(nblk, blist, proj, proj, proj, proj, proj, selb, proj, ocmpT, alibi, _window_pos_cols())


CONV_TS = 512
CONV_HALO = 32
CONV_RC = 64
CONV_CC = 256


def _conv_kernel(a_ref, b_ref, w_ref, cb_ref, g_ref, beta_ref, o_ref, hbuf, ybuf, wrep):
    si = pl.program_id(1)

    @pl.when(si == 0)
    def _():
        hbuf[pl.ds(0, CONV_HALO), :] = jnp.zeros((CONV_HALO, CONV_CH), F32)

    @pl.when(si > 0)
    def _():
        hbuf[pl.ds(0, CONV_HALO), :] = hbuf[pl.ds(CONV_TS, CONV_HALO), :]

    a = a_ref[...].astype(F32)
    bb = b_ref[...].astype(F32)
    hbuf[pl.ds(CONV_HALO, CONV_TS), :] = a * jax.nn.sigmoid(bb)

    for k in range(CONV_WIDTH):
        wrep[pl.ds(k * SUBLANES, SUBLANES), :] = jnp.broadcast_to(w_ref[k:k + 1, :],
                                                                  (SUBLANES, CONV_CH))

    off = CONV_HALO - (CONV_WIDTH - 1)
    groups = CONV_RC // SUBLANES
    for cc in range(CONV_CH // CONV_CC):
        cols = slice(cc * CONV_CC, (cc + 1) * CONV_CC)

        def row_body(r, carry, cols=cols):
            r0 = pl.multiple_of(r * CONV_RC, CONV_RC)
            win = hbuf[pl.ds(r0, CONV_RC + CONV_HALO), cols]
            acc = jnp.zeros((groups, SUBLANES, CONV_CC), F32)
            for res in range(SUBLANES):
                taps = [k for k in range(CONV_WIDTH) if (off + k) % SUBLANES == res]
                span = max(off + k for k in taps) - res + CONV_RC
                shifted = win[res:res + span, :]
                for k in taps:
                    a0 = off + k - res
                    rows = shifted[a0:a0 + CONV_RC, :].reshape(groups, SUBLANES, CONV_CC)
                    acc = acc + rows * wrep[pl.ds(k * SUBLANES, SUBLANES), cols][None]
            ybuf[pl.ds(r0, CONV_RC), cols] = acc.reshape(CONV_RC, CONV_CC)
            return carry

        lax.fori_loop(0, CONV_TS // CONV_RC, row_body, 0)

    y = ybuf[...] + cb_ref[...]
    mu = jnp.mean(y, axis=-1, keepdims=True)
    yc = y - mu
    var = jnp.mean(yc * yc, axis=-1, keepdims=True)
    z = yc * lax.rsqrt(var + LN_EPS) * g_ref[...] + beta_ref[...]
    o_ref[...] = (z * jax.nn.sigmoid(z)).astype(o_ref.dtype)


def _conv(proj, conv_w, conv_b, ln_g, ln_b, B, S):
    T = B * S
    ts = CONV_TS
    assert S % ts == 0
    nS = S // ts
    ca = COL_GLU // CONV_CH
    vec = lambda: pl.BlockSpec((1, CONV_CH), lambda b, s: (0, 0))
    return pl.pallas_call(
        _conv_kernel,
        out_shape=jax.ShapeDtypeStruct((T, CONV_CH), BF16),
        grid=(B, nS),
        in_specs=[pl.BlockSpec((ts, CONV_CH), lambda b, s: (b * nS + s, ca)),
                  pl.BlockSpec((ts, CONV_CH), lambda b, s: (b * nS + s, ca + 1)),
                  pl.BlockSpec((CONV_WIDTH, CONV_CH), lambda b, s: (0, 0)),
                  vec(), vec(), vec()],
        out_specs=pl.BlockSpec((ts, CONV_CH), lambda b, s: (b * nS + s, 0)),
        scratch_shapes=[pltpu.VMEM((CONV_HALO + ts, CONV_CH), F32),
                        pltpu.VMEM((ts, CONV_CH), F32),
                        pltpu.VMEM((CONV_WIDTH * SUBLANES, CONV_CH), F32)],
        compiler_params=_cparams(("arbitrary", "arbitrary")),
        name="conformer_conv",
    )(proj, proj, conv_w, conv_b.reshape(1, -1), ln_g.reshape(1, -1), ln_b.reshape(1, -1))


def _merge_kernel(o_ref, h_ref, wa_ref, wb_ref, ga_ref, gb_ref, out_ref):
    ya = jnp.dot(o_ref[...], wa_ref[...], preferred_element_type=F32)
    yb = jnp.dot(h_ref[...], wb_ref[...], preferred_element_type=F32)
    ga = jax.nn.sigmoid(ga_ref[...].astype(F32))
    gb = jax.nn.sigmoid(gb_ref[...].astype(F32))
    out_ref[...] = (ga * ya + gb * yb).astype(out_ref.dtype)


def _merge(o_nsa, h_conv, wa, wb, proj):
    T = o_nsa.shape[0]
    tm = min(512, T)
    tn = 1024
    nN = D_MODEL // tn
    ga0 = COL_MERGE // tn
    return pl.pallas_call(
        _merge_kernel,
        out_shape=jax.ShapeDtypeStruct((T, D_MODEL), BF16),
        grid=(T // tm, nN),
        in_specs=[pl.BlockSpec((tm, Q_DIM), lambda i, j: (i, 0)),
                  pl.BlockSpec((tm, CONV_CH), lambda i, j: (i, 0)),
                  pl.BlockSpec((Q_DIM, tn), lambda i, j: (0, j)),
                  pl.BlockSpec((CONV_CH, tn), lambda i, j: (0, j)),
                  pl.BlockSpec((tm, tn), lambda i, j: (i, ga0 + j)),
                  pl.BlockSpec((tm, tn), lambda i, j: (i, ga0 + nN + j))],
        out_specs=pl.BlockSpec((tm, tn), lambda i, j: (i, j)),
        compiler_params=_cparams(("arbitrary", "arbitrary")),
        name="merge",
    )(o_nsa, h_conv, wa, wb, proj, proj)


def _layer_norm_rows(y, g, b):
    mu = jnp.mean(y, axis=-1, keepdims=True)
    yc = y - mu
    var = jnp.mean(yc * yc, axis=-1, keepdims=True)
    return yc * lax.rsqrt(var + LN_EPS) * g + b


def _split2(v):
    hi = v.astype(BF16)
    return hi, (v - hi.astype(F32)).astype(BF16)


def _outproj_kernel(mix_ref, w_ref, x_ref, g_ref, b_ref, rw_hi_ref, rw_lo_ref, rb_ref,
                    x1_ref, x1b_ref, ti_ref, tg_ref):
    m = jnp.dot(mix_ref[...], w_ref[...], preferred_element_type=F32)
    x1 = _layer_norm_rows(DEEPNORM_ALPHA * x_ref[...] + m, g_ref[...], b_ref[...])
    x1_ref[...] = x1
    x1b_ref[...] = x1.astype(BF16)
    x_hi, x_lo = _split2(x1)
    logits = (jnp.dot(x_hi, rw_hi_ref[...], preferred_element_type=F32)
              + jnp.dot(x_hi, rw_lo_ref[...], preferred_element_type=F32)
              + jnp.dot(x_lo, rw_hi_ref[...], preferred_element_type=F32)) + rb_ref[...]
    lane = lax.broadcasted_iota(jnp.int32, logits.shape, 1)
    logits = jnp.where(lane < N_EXPERTS, logits, -jnp.inf)
    ti = jnp.zeros(logits.shape, jnp.int32)
    tv = jnp.zeros(logits.shape, F32)
    top0 = None
    den = jnp.zeros((logits.shape[0], 1), F32)
    for k in range(TOP_K):
        mx = jnp.max(logits, axis=-1, keepdims=True)
        idx = jnp.min(jnp.where(logits == mx, lane, LANES), axis=-1, keepdims=True)
        if top0 is None:
            top0 = mx
        e = jnp.exp(mx - top0)
        den = den + e
        ti = jnp.where(lane == k, idx, ti)
        tv = jnp.where(lane == k, e, tv)
        logits = jnp.where(lane == idx, -jnp.inf, logits)
    ti_ref[...] = ti
    tg_ref[...] = tv / den


def _outproj(mix, w_out, x2d, g, b, rw_hi, rw_lo, rb):
    T = mix.shape[0]
    tm = min(256, T)
    full = lambda shape: pl.BlockSpec(shape, lambda i: (0, 0))
    rowb = lambda w: pl.BlockSpec((tm, w), lambda i: (i, 0))
    return pl.pallas_call(
        _outproj_kernel,
        out_shape=(jax.ShapeDtypeStruct((T, D_MODEL), F32),
                   jax.ShapeDtypeStruct((T, D_MODEL), BF16),
                   jax.ShapeDtypeStruct((T, LANES), jnp.int32),
                   jax.ShapeDtypeStruct((T, LANES), F32)),
        grid=(T // tm,),
        in_specs=[rowb(D_MODEL), full((D_MODEL, D_MODEL)), rowb(D_MODEL),
                  full((1, D_MODEL)), full((1, D_MODEL)),
                  full((D_MODEL, LANES)), full((D_MODEL, LANES)), full((1, LANES))],
        out_specs=(rowb(D_MODEL), rowb(D_MODEL), rowb(LANES), rowb(LANES)),
        compiler_params=_cparams(("arbitrary",)),
        name="outproj_ln_router",
    )(mix, w_out, x2d, g, b, rw_hi, rw_lo, rb)


def _moe_up_kernel(be_ref, nused_ref, x_ref, wg_ref, wu_ref, bg_ref, bu_ref, *rest, base):
    h_ref, wgb_ref, wub_ref = rest[-3:]
    local = pl.program_id(1)
    r = base + local
    prev = be_ref[jnp.maximum(r - 1, 0)]
    fresh = (local == 0) | (be_ref[r] != prev)

    @pl.when(fresh)
    def _():
        wgb_ref[...] = wg_ref[0].astype(BF16)
        wub_ref[...] = wu_ref[0].astype(BF16)

    @pl.when(r < nused_ref[0])
    def _():
        x = x_ref[...]
        gt = jnp.dot(x, wgb_ref[...], preferred_element_type=F32) + bg_ref[0]
        up = jnp.dot(x, wub_ref[...], preferred_element_type=F32) + bu_ref[0]
        gt = jnp.minimum(gt, SWIGLU_LIMIT)
        up = jnp.clip(up, -SWIGLU_LIMIT, SWIGLU_LIMIT)
        h = gt * jax.nn.sigmoid(SWIGLU_ALPHA * gt) * (up + 1.0)
        h_ref[...] = h.astype(h_ref.dtype)


def _moe_row_maps(nblk):
    rd = lambda r, nu: jnp.minimum(r, nu[0] - 1)
    wr = lambda r, nu: jnp.where(r < nu[0], r, nblk)
    return rd, wr


def _moe_up(blk_e, n_used, xs_seg, seg, nblk, hbuf, w_gate, w_up, b_gate, b_up):
    bm, tf = MOE_BLOCK, MOE_UP_TILE
    seg_blk = xs_seg.shape[0] // bm
    base = seg * seg_blk
    rd = lambda r, nu: jnp.clip(nu[0] - 1 - base, 0, r)
    wr = lambda r, nu: jnp.where(base + r < nu[0], base + r, nblk)
    wspec = pl.BlockSpec((1, D_MODEL, tf), lambda f, r, be, nu: (be[base + r], 0, f))
    bspec = pl.BlockSpec((1, 1, tf), lambda f, r, be, nu: (be[base + r], 0, f))
    in_specs = [pl.BlockSpec((bm, D_MODEL), lambda f, r, be, nu: (rd(r, nu), 0)),
                wspec, wspec, bspec, bspec]
    args = [blk_e, n_used, xs_seg, w_gate, w_up, b_gate.reshape(N_EXPERTS, 1, D_FF),
            b_up.reshape(N_EXPERTS, 1, D_FF)]
    aliases = {}
    if hbuf is not None:
        in_specs.append(pl.BlockSpec(memory_space=pl.ANY))
        aliases = {len(args): 0}
        args.append(hbuf)
    grid_spec = pltpu.PrefetchScalarGridSpec(
        num_scalar_prefetch=2,
        grid=(D_FF // tf, seg_blk),
        in_specs=in_specs,
        out_specs=pl.BlockSpec((bm, tf), lambda f, r, be, nu: (wr(r, nu), f)),
        scratch_shapes=[pltpu.VMEM((D_MODEL, tf), BF16), pltpu.VMEM((D_MODEL, tf), BF16)],
    )
    return pl.pallas_call(
        functools.partial(_moe_up_kernel, base=base),
        out_shape=jax.ShapeDtypeStruct(((nblk + 1) * bm, D_FF), BF16),
        grid_spec=grid_spec,
        input_output_aliases=aliases,
        compiler_params=_cparams(("arbitrary", "arbitrary")),
        name="moe_up",
    )(*args)


def _moe_down_kernel(be_ref, nused_ref, h_ref, wd_ref, bd_ref, o_ref, wdb_ref):
    r = pl.program_id(0)
    prev = be_ref[jnp.maximum(r - 1, 0)]
    fresh = (r == 0) | (be_ref[r] != prev)

    @pl.when(fresh)
    def _():
        wdb_ref[...] = wd_ref[0].astype(BF16)

    @pl.when(r < nused_ref[0])
    def _():
        y = jnp.dot(h_ref[...], wdb_ref[...], preferred_element_type=F32) + bd_ref[0]
        o_ref[...] = y.astype(o_ref.dtype)


def _moe_down(blk_e, n_used, h, w_down, b_down):
    bm = MOE_BLOCK
    nblk = h.shape[0] // bm - 1
    rd, wr = _moe_row_maps(nblk)
    grid_spec = pltpu.PrefetchScalarGridSpec(
        num_scalar_prefetch=2,
        grid=(nblk,),
        in_specs=[pl.BlockSpec((bm, D_FF), lambda r, be, nu: (rd(r, nu), 0)),
                  pl.BlockSpec((1, D_FF, D_MODEL), lambda r, be, nu: (be[r], 0, 0)),
                  pl.BlockSpec((1, 1, D_MODEL), lambda r, be, nu: (be[r], 0, 0))],
        out_specs=pl.BlockSpec((bm, D_MODEL), lambda r, be, nu: (wr(r, nu), 0)),
        scratch_shapes=[pltpu.VMEM((D_FF, D_MODEL), BF16)],
    )
    return pl.pallas_call(
        _moe_down_kernel,
        out_shape=jax.ShapeDtypeStruct(((nblk + 1) * bm, D_MODEL), BF16),
        grid_spec=grid_spec,
        compiler_params=_cparams(("arbitrary",)),
        name="moe_down",
    )(blk_e, n_used, h, w_down, b_down.reshape(N_EXPERTS, 1, D_MODEL))


def _final_kernel(x1_ref, *refs):
    y_refs, (tg_ref, g_ref, b_ref, o_ref) = refs[:TOP_K], refs[TOP_K:]
    tg = tg_ref[...]
    f = tg[:, 0:1] * y_refs[0][...].astype(F32)
    for k in range(1, TOP_K):
        f = f + tg[:, k:k + 1] * y_refs[k][...].astype(F32)
    o_ref[...] = _layer_norm_rows(DEEPNORM_ALPHA * x1_ref[...] + f, g_ref[...], b_ref[...])


def _final(x1, ys, tg, g, b):
    T = x1.shape[0]
    tm = min(256, T)
    rows = pl.BlockSpec((tm, D_MODEL), lambda i: (i, 0))
    vec = pl.BlockSpec((1, D_MODEL), lambda i: (0, 0))
    return pl.pallas_call(
        _final_kernel,
        out_shape=jax.ShapeDtypeStruct((T, D_MODEL), F32),
        grid=(T // tm,),
        in_specs=[rows] + [rows] * TOP_K + [pl.BlockSpec((tm, LANES), lambda i: (i, 0)), vec, vec],
        out_specs=rows,
        compiler_params=_cparams(("arbitrary",)),
        name="combine_ln",
    )(x1, *ys, tg, g, b)


def _prep_w_in(w_in):
    splits = np.cumsum([Q_DIM] + [KV_DIM] * 6 + [NSA_GATE_DIM, 2 * CONV_CH, 2 * D_MODEL])
    q = w_in[:, :splits[0]] * (LOG2E / math.sqrt(HEAD_DIM))
    kv = w_in[:, splits[0]:splits[6]]
    gate = w_in[:, splits[6]:splits[7]]
    glu = w_in[:, splits[7]:splits[8]]
    merge = w_in[:, splits[8]:splits[9]]
    per_g = HEADS_PER_GROUP * 3
    gate_tiles = [jnp.pad(gate[:, g * per_g:(g + 1) * per_g], ((0, 0), (0, LANES - per_g)))
                  for g in range(N_KV_GROUPS)]
    return jnp.concatenate([q, glu, merge, kv] + gate_tiles, axis=1).astype(BF16)


def _mixer(x2d, B, S, w_in, cmp_pe, cmp_w1, cmp_b1, cmp_w2, cmp_b2, w_nsa_proj,
           conv_w, conv_b, conv_ln_g, conv_ln_b, w_conv_proj):
    T = B * S
    G = N_KV_GROUPS
    proj = _inproj(x2d, _prep_w_in(w_in))

    kcvc = _compress(proj, cmp_pe.reshape(2, 1, CMP_BLOCK * HEAD_DIM),
                     cmp_w1.astype(BF16), cmp_b1.reshape(2, 1, HEAD_DIM),
                     cmp_w2.astype(BF16), cmp_b2.reshape(2, 1, HEAD_DIM), B, S)

    C = S // Q_BLOCK
    alibi = _alibi_rows().astype(BF16)
    ocmpT, selb = _nsa_cmp(proj, kcvc, alibi, B, S)
    nsp = selb.shape[2]
    fl = jnp.max(selb, axis=3) > 0.5 * NEG_INF
    fl = fl.reshape(B, C, G, nsp).transpose(0, 2, 1, 3)
    past = (jnp.arange(nsp, dtype=jnp.int32)[None, :]
            < (Q_BLOCK // SEL_BLOCK) * jnp.arange(C, dtype=jnp.int32)[:, None])
    fl = (fl & past[None, None]).reshape(B * G * C, nsp)
    nblk = jnp.sum(fl, axis=1).astype(jnp.int32)
    blist = jnp.argsort(jnp.logical_not(fl), axis=1, stable=True).astype(jnp.int32)
    o_nsa = _nsa_slc(proj, selb, nblk, blist.reshape(-1), ocmpT, alibi, B, S)

    h_conv = _conv(proj, conv_w, conv_b, conv_ln_g, conv_ln_b, B, S)
    return _merge(o_nsa, h_conv, w_nsa_proj.astype(BF16), w_conv_proj.astype(BF16), proj)


def _moe(x1b, top_i, w_gate, b_gate, w_up, b_up, w_down, b_down):
    T = x1b.shape[0]
    A = T * TOP_K
    bm = MOE_BLOCK
    flat_e = top_i[:, :TOP_K].reshape(A)
    onehot = (flat_e[:, None] == jnp.arange(N_EXPERTS, dtype=jnp.int32)[None, :]).astype(jnp.int32)
    csum = jnp.cumsum(onehot, axis=0)
    rank = jnp.sum(onehot * csum, axis=1) - 1
    counts = csum[-1]
    padded = (counts + bm - 1) // bm * bm
    pad_end = jnp.cumsum(padded)
    pad_start = pad_end - padded
    dest = pad_start[flat_e] + rank
    P = -(-A // bm) * bm + N_EXPERTS * bm
    nblk = P // bm
    flat_tok = jnp.arange(A, dtype=jnp.int32) // TOP_K
    buf_tok = jnp.zeros((P,), jnp.int32).at[dest].set(flat_tok)
    blk_start = jnp.arange(nblk, dtype=jnp.int32) * bm
    blk_e = jnp.minimum(jnp.sum(blk_start[:, None] >= pad_end[None, :], axis=1),
                        N_EXPERTS - 1).astype(jnp.int32)
    n_used = (pad_end[-1] // bm).astype(jnp.int32).reshape(1)
    seg_rows = P // MOE_SEGMENTS
    assert seg_rows % bm == 0
    h = None
    for seg in range(MOE_SEGMENTS):
        xs_seg = x1b[buf_tok[seg * seg_rows:(seg + 1) * seg_rows]]
        h = _moe_up(blk_e, n_used, xs_seg, seg, nblk, h, w_gate, w_up, b_gate, b_up)
    out = _moe_down(blk_e, n_used, h, w_down, b_down)
    dest_k = dest.reshape(T, TOP_K)
    return [out[dest_k[:, k]] for k in range(TOP_K)]


def kernel(x, w_in, cmp_pe, cmp_w1, cmp_b1, cmp_w2, cmp_b2, w_nsa_proj, conv_w, conv_b, conv_ln_g, conv_ln_b, w_conv_proj, w_out, ln1_g, ln1_b, router_w, router_b, w_gate, b_gate, w_up, b_up, w_down, b_down, ln2_g, ln2_b):
    B, S, D = x.shape
    T = B * S
    x2d = x.reshape(T, D)
    for l in range(DEPTH):
        mix = _mixer(x2d, B, S, w_in[l], cmp_pe[l], cmp_w1[l], cmp_b1[l], cmp_w2[l], cmp_b2[l],
                     w_nsa_proj[l], conv_w[l], conv_b[l], conv_ln_g[l], conv_ln_b[l],
                     w_conv_proj[l])
        rw = jnp.pad(router_w[l], ((0, 0), (0, LANES - N_EXPERTS)))
        rw_hi = rw.astype(BF16)
        rw_lo = (rw - rw_hi.astype(F32)).astype(BF16)
        rb = jnp.pad(router_b[l], (0, LANES - N_EXPERTS)).reshape(1, LANES)
        x1, x1b, top_i, top_g = _outproj(
            mix, w_out[l].astype(BF16), x2d, ln1_g[l].reshape(1, D), ln1_b[l].reshape(1, D),
            rw_hi, rw_lo, rb)
        y4 = _moe(x1b, top_i, w_gate[l], b_gate[l], w_up[l], b_up[l], w_down[l], b_down[l])
        x2d = _final(x1, y4, top_g, ln2_g[l].reshape(1, D), ln2_b[l].reshape(1, D))
    return x2d.reshape(B, S, D)
```

```python
import functools
import math

import numpy as np
import jax
import jax.numpy as jnp
from jax import lax
from jax.experimental import pallas as pl
from jax.experimental.pallas import tpu as pltpu

D_MODEL = 2048
N_HEADS = 16
N_KV_GROUPS = 2
HEADS_PER_GROUP = N_HEADS // N_KV_GROUPS
HEAD_DIM = 128
CMP_BLOCK = 32
CMP_STRIDE = 16
SEL_BLOCK = 64
SEL_SHIFT = 6
SEL_TOP_N = 16
WINDOW = 512
Q_BLOCK = 128
N_OVERLAP = (SEL_BLOCK + CMP_BLOCK) // CMP_STRIDE - 1
FORCE_BONUS = 1.0e4
CONV_CH = D_MODEL // 2
CONV_WIDTH = 31
N_EXPERTS = 32
TOP_K = 4
D_FF = D_MODEL
SWIGLU_LIMIT = 7.0
SWIGLU_ALPHA = 1.702
LN_EPS = 1e-5
DEPTH = 1
DEEPNORM_ALPHA = (2 * DEPTH) ** 0.25
NEG_INF = -1e30
TINY = 1e-30

Q_DIM = N_HEADS * HEAD_DIM
KV_DIM = N_KV_GROUPS * HEAD_DIM
NSA_GATE_DIM = N_HEADS * 3

LANES = 128
SUBLANES = 8
VMEM_LIMIT = 56 * 1024 * 1024

COL_Q = 0
COL_GLU = COL_Q + Q_DIM
COL_MERGE = COL_GLU + 2 * CONV_CH
COL_KC = COL_MERGE + 2 * D_MODEL
COL_VC = COL_KC + KV_DIM
COL_KS = COL_VC + KV_DIM
COL_VS = COL_KS + KV_DIM
COL_KW = COL_VS + KV_DIM
COL_VW = COL_KW + KV_DIM
COL_GATE = COL_VW + KV_DIM
PROJ_W = COL_GATE + N_KV_GROUPS * LANES

BF16 = jnp.bfloat16
F32 = jnp.float32

MOE_BLOCK = 512
MOE_UP_TILE = 1024
MOE_SEGMENTS = 4

_ALIBI = np.exp2(-8.0 * np.arange(1, N_HEADS + 1, dtype=np.float32) / N_HEADS).astype(np.float32)
_ALIBI = _ALIBI.reshape(N_KV_GROUPS, HEADS_PER_GROUP)


def _cparams(sem, vmem=VMEM_LIMIT):
    return pltpu.CompilerParams(dimension_semantics=sem, vmem_limit_bytes=vmem)


def _inproj_kernel(x_ref, w_ref, o_ref, xb_ref):
    @pl.when(pl.program_id(1) == 0)
    def _():
        xb_ref[...] = x_ref[...].astype(BF16)

    o_ref[...] = jnp.dot(xb_ref[...], w_ref[...],
                         preferred_element_type=F32).astype(o_ref.dtype)


def _inproj(x2d, w_p):
    T, D = x2d.shape
    N = w_p.shape[1]
    tm = min(512, T)
    tn = N // 3
    assert T % tm == 0 and N % tn == 0 and tn % (2 * LANES) == 0
    return pl.pallas_call(
        _inproj_kernel,
        out_shape=jax.ShapeDtypeStruct((T, N), BF16),
        grid=(T // tm, N // tn),
        in_specs=[pl.BlockSpec((tm, D), lambda i, j: (i, 0)),
                  pl.BlockSpec((D, tn), lambda i, j: (0, j))],
        out_specs=pl.BlockSpec((tm, tn), lambda i, j: (i, j)),
        scratch_shapes=[pltpu.VMEM((tm, D), BF16)],
        compiler_params=_cparams(("arbitrary", "arbitrary")),
        name="inproj",
    )(x2d, w_p)


def _gelu_tanh(x):
    c = math.sqrt(2.0 / math.pi)
    return 0.5 * x * (1.0 + jnp.tanh(c * (x + 0.044715 * (x * x * x))))


def _compress_kernel(x_ref, pe_ref, w1_ref, b1_ref, w2_ref, b2_ref, o_ref, xf_ref):
    S = x_ref.shape[0]
    n16 = S // CMP_STRIDE
    xf_ref[...] = x_ref[...].astype(F32)
    top = jnp.zeros((n16, HEAD_DIM), F32)
    bot = jnp.zeros((n16, HEAD_DIM), F32)
    for j in range(CMP_STRIDE):
        xj = xf_ref[pl.ds(j, n16, stride=CMP_STRIDE), :]
        lo, hi = j * HEAD_DIM, (CMP_STRIDE + j) * HEAD_DIM
        top = top + jnp.dot((xj + pe_ref[0, :, lo:lo + HEAD_DIM]).astype(BF16),
                            w1_ref[0, lo:lo + HEAD_DIM, :], preferred_element_type=F32)
        bot = bot + jnp.dot((xj + pe_ref[0, :, hi:hi + HEAD_DIM]).astype(BF16),
                            w1_ref[0, hi:hi + HEAD_DIM, :], preferred_element_type=F32)
    pre = top + pltpu.roll(bot, n16 - 1, 0) + b1_ref[0]
    h = _gelu_tanh(pre)
    o = jnp.dot(h.astype(BF16), w2_ref[0], preferred_element_type=F32) + b2_ref[0]
    o_ref[0, 0] = o.astype(o_ref.dtype)


def _compress(proj, pe, w1, b1, w2, b2, B, S):
    n_slot = 2 * N_KV_GROUPS
    n16 = S // CMP_STRIDE
    wide = CMP_BLOCK * HEAD_DIM
    kv = lambda s, b: (s // N_KV_GROUPS, 0, 0)
    return pl.pallas_call(
        _compress_kernel,
        out_shape=jax.ShapeDtypeStruct((n_slot, B, n16, HEAD_DIM), BF16),
        grid=(n_slot, B),
        in_specs=[pl.BlockSpec((S, HEAD_DIM), lambda s, b: (b, COL_KC // HEAD_DIM + s)),
                  pl.BlockSpec((1, 1, wide), kv),
                  pl.BlockSpec((1, wide, HEAD_DIM), kv),
                  pl.BlockSpec((1, 1, HEAD_DIM), kv),
                  pl.BlockSpec((1, HEAD_DIM, HEAD_DIM), kv),
                  pl.BlockSpec((1, 1, HEAD_DIM), kv)],
        out_specs=pl.BlockSpec((1, 1, n16, HEAD_DIM), lambda s, b: (s, b, 0, 0)),
        scratch_shapes=[pltpu.VMEM((S, HEAD_DIM), F32)],
        compiler_params=_cparams(("arbitrary", "arbitrary")),
        name="compress",
    )(proj, pe, w1, b1, w2, b2)


LOG2E = 1.4426950408889634
KAUG = 2 * HEAD_DIM
ROW_ALIBI = 0
ROW_SLOT = 16
SLC_GROUP = 8
SLC_TAIL = 4
assert 2 * SLC_GROUP == 16 and SLC_TAIL == 4
GROUP_W = HEADS_PER_GROUP * Q_BLOCK
WIN_TILES = WINDOW // Q_BLOCK
WIN_KEYS = WINDOW + Q_BLOCK


def _window_pos_cols():
    rel = np.arange(WIN_KEYS) - WINDOW
    a = (rel // SEL_BLOCK) * SEL_BLOCK
    r = rel - a
    cols = np.zeros((WIN_KEYS, LANES), np.float32)
    cols[:, ROW_ALIBI:ROW_ALIBI + 3] = a[:, None]
    cols[:, ROW_ALIBI + 3:ROW_ALIBI + 6] = r[:, None]
    return jnp.asarray(cols, BF16)


def _alibi_rows():
    s2 = jnp.asarray(_ALIBI * np.float32(LOG2E), F32)
    hi = s2.astype(BF16)
    r1 = s2 - hi.astype(F32)
    mid = r1.astype(BF16)
    lo = (r1 - mid.astype(F32)).astype(BF16)
    trip = jnp.stack([hi, mid, lo, hi, mid, lo], axis=1)
    rows = jnp.pad(trip, ((0, 0), (0, 16 - 6), (0, 0)))
    return jnp.repeat(rows, Q_BLOCK, axis=2)


def _dot_ta(a, b):
    return lax.dot_general(a, b, (((0,), (0,)), ((), ())), preferred_element_type=F32)


def _fill_qaug(qaug, q_ref, alibi_ref):
    for h in range(HEADS_PER_GROUP):
        qaug[pl.ds(0, HEAD_DIM), pl.ds(h * Q_BLOCK, Q_BLOCK)] = (
            q_ref[:, h * HEAD_DIM:(h + 1) * HEAD_DIM].T)
    qaug[pl.ds(HEAD_DIM, 16), :] = alibi_ref[0]
    qaug[pl.ds(HEAD_DIM + 16, HEAD_DIM - 16), :] = jnp.zeros((HEAD_DIM - 16, GROUP_W), BF16)


def _pos_cols(a_val, r_val, lane, slot_lane=None):
    base = jnp.where((lane >= ROW_ALIBI + 3) & (lane < ROW_ALIBI + 6), r_val, 0.0)
    if slot_lane is not None:
        base = jnp.where(lane == slot_lane, 1.0, base)
    return jnp.where(lane < ROW_ALIBI + 3, a_val, base)


def _nsa_cmp_kernel(q_ref, kc_ref, vc_ref, alibi_ref, ocmpT_ref, selb_ref, qaug, *, n_selb):
    c = pl.program_id(2)
    _fill_qaug(qaug, q_ref, alibi_ref)
    n_vis = c * (Q_BLOCK // CMP_STRIDE) + (Q_BLOCK - CMP_BLOCK) // CMP_STRIDE + 1
    tiles = kc_ref.shape[2] // LANES
    for i in range(1, tiles + 1):
        cond = n_vis > (i - 1) * LANES
        if i < tiles:
            cond = cond & (n_vis <= i * LANES)
        pl.when(cond)(functools.partial(
            _nsa_cmp_rows, c, i * LANES, kc_ref, vc_ref, ocmpT_ref, selb_ref, qaug, n_selb))


def _nsa_cmp_rows(c, ncp, kc_ref, vc_ref, ocmpT_ref, selb_ref, qaug, n_selb):
    nsp = selb_ref.shape[2]

    n_i = lax.broadcasted_iota(jnp.int32, (ncp, LANES), 0)
    lane = lax.broadcasted_iota(jnp.int32, (ncp, LANES), 1)
    end_rel = n_i * CMP_STRIDE + (CMP_BLOCK - 1) - c * Q_BLOCK
    a_val = jnp.left_shift(jnp.right_shift(end_rel, SEL_SHIFT), SEL_SHIFT).astype(F32)
    r_val = (end_rel & (SEL_BLOCK - 1)).astype(F32)
    pc = _pos_cols(a_val, r_val, lane).astype(BF16)
    s = jnp.dot(jnp.concatenate([kc_ref[0, 0, pl.ds(0, ncp), :], pc], axis=1), qaug[...],
                preferred_element_type=F32)

    tq = lax.broadcasted_iota(jnp.int32, (1, Q_BLOCK), 1)
    t_row = c * Q_BLOCK + tq
    n_col = lax.broadcasted_iota(jnp.int32, (ncp, Q_BLOCK), 0)
    mask_bias = jnp.where(n_col * CMP_STRIDE + (CMP_BLOCK - 1) <= t_row, 0.0, NEG_INF)
    has_valid = t_row >= CMP_BLOCK - 1

    imp = jnp.zeros((ncp, Q_BLOCK), F32)
    es, rls = [], []
    for h in range(HEADS_PER_GROUP):
        sh = s[:, h * Q_BLOCK:(h + 1) * Q_BLOCK] + mask_bias
        m = jnp.max(sh, axis=0, keepdims=True)
        e = jnp.exp2(sh - m)
        l = jnp.sum(e, axis=0, keepdims=True)
        rl = jnp.where(has_valid, 1.0 / jnp.maximum(l, TINY), 0.0)
        imp = imp + e * rl
        es.append(e.astype(BF16))
        rls.append(rl)
    oT = _dot_ta(vc_ref[0, 0, pl.ds(0, ncp), :], jnp.concatenate(es, axis=1))
    ocmpT_ref[0, 0] = oT * jnp.concatenate(rls, axis=1)

    j_i = lax.broadcasted_iota(jnp.int32, (nsp, ncp), 0)
    n_j = lax.broadcasted_iota(jnp.int32, (nsp, ncp), 1)
    lo = j_i * (SEL_BLOCK // CMP_STRIDE) - (CMP_BLOCK // CMP_STRIDE) + 1
    ovl = ((n_j >= lo) & (n_j < lo + N_OVERLAP)).astype(BF16)
    i_hi = imp.astype(BF16)
    r1 = imp - i_hi.astype(F32)
    i_mid = r1.astype(BF16)
    i_lo = (r1 - i_mid.astype(F32)).astype(BF16)
    imp_sel = (jnp.dot(ovl, i_hi, preferred_element_type=F32)
               + jnp.dot(ovl, i_mid, preferred_element_type=F32)
               + jnp.dot(ovl, i_lo, preferred_element_type=F32))

    blk = lax.broadcasted_iota(jnp.int32, (nsp, Q_BLOCK), 0)
    cur = jnp.right_shift(t_row, SEL_SHIFT)
    valid = blk * SEL_BLOCK <= t_row
    forced = (blk == 0) | (blk == cur) | (blk == cur - 1)
    score = jnp.where(valid, imp_sel + jnp.where(forced, FORCE_BONUS, 0.0), -1.0)
    score = jnp.where(blk < n_selb, score, -3.0)
    selb = jnp.full((nsp, Q_BLOCK), NEG_INF, F32)
    for _ in range(min(SEL_TOP_N, n_selb)):
        mx = jnp.max(score, axis=0, keepdims=True)
        first = jnp.min(jnp.where(score == mx, blk, nsp), axis=0, keepdims=True)
        pick = blk == first
        selb = jnp.where(pick, 0.0, selb)
        score = jnp.where(pick, -2.0, score)
    selb_ref[0, 0] = selb


def _nsa_cmp(proj, kcvc, alibi, B, S):
    C = S // Q_BLOCK
    G = N_KV_GROUPS
    ncp = kcvc.shape[2]
    n_selb = S // SEL_BLOCK
    nsp = max(LANES, n_selb)
    kern = functools.partial(_nsa_cmp_kernel, n_selb=n_selb)
    return pl.pallas_call(
        kern,
        out_shape=(jax.ShapeDtypeStruct((B * C, G, HEAD_DIM, GROUP_W), F32),
                   jax.ShapeDtypeStruct((B * C, G, nsp, Q_BLOCK), F32)),
        grid=(B, G, C),
        in_specs=[pl.BlockSpec((Q_BLOCK, HEADS_PER_GROUP * HEAD_DIM),
                               lambda b, g, c: (b * C + c, COL_Q // (HEADS_PER_GROUP * HEAD_DIM) + g)),
                  pl.BlockSpec((1, 1, ncp, HEAD_DIM), lambda b, g, c: (g, b, 0, 0)),
                  pl.BlockSpec((1, 1, ncp, HEAD_DIM), lambda b, g, c: (G + g, b, 0, 0)),
                  pl.BlockSpec((1, 16, GROUP_W), lambda b, g, c: (g, 0, 0))],
        out_specs=(pl.BlockSpec((1, 1, HEAD_DIM, GROUP_W), lambda b, g, c: (b * C + c, g, 0, 0)),
                   pl.BlockSpec((1, 1, nsp, Q_BLOCK), lambda b, g, c: (b * C + c, g, 0, 0))),
        scratch_shapes=[pltpu.VMEM((KAUG, GROUP_W), BF16)],
        compiler_params=_cparams(("arbitrary", "arbitrary", "arbitrary")),
        name="nsa_cmp",
    )(proj, kcvc, kcvc, alibi)


def _tile_scores(qaug, k_tile, pc):
    kaug = jnp.concatenate([k_tile, pc.astype(BF16)], axis=1)
    return jnp.dot(kaug, qaug[...], preferred_element_type=F32)


def _softmax_tile(st, qaug, k_tile, pc, v_tile, bias, first):
    _softmax_update(st, _tile_scores(qaug, k_tile, pc), v_tile, bias, first)


def _softmax_update(st, s, v_tile, bias, first):
    m_ref, l_ref, acc_ref = st
    if bias is not None:
        s = jnp.concatenate([s[:, h * Q_BLOCK:(h + 1) * Q_BLOCK] + bias
                             for h in range(HEADS_PER_GROUP)], axis=1)
    mx = jnp.max(s, axis=0, keepdims=True)
    if first:
        m_new = mx
    else:
        m_old = m_ref[pl.ds(0, 1), :]
        m_new = jnp.maximum(m_old, mx)
    p = jnp.exp2(s - m_new)
    ps = jnp.sum(p, axis=0, keepdims=True)
    pv = _dot_ta(v_tile, p.astype(BF16))
    if first:
        l_ref[pl.ds(0, 1), :] = ps
        acc_ref[...] = pv
    else:
        alpha = jnp.exp2(m_old - m_new)
        l_ref[pl.ds(0, 1), :] = alpha * l_ref[pl.ds(0, 1), :] + ps
        acc_ref[...] = alpha * acc_ref[...] + pv
    m_ref[pl.ds(0, 1), :] = m_new


def _bias_rows(s, bias):
    return jnp.concatenate([s[:, h * Q_BLOCK:(h + 1) * Q_BLOCK] + bias
                            for h in range(HEADS_PER_GROUP)], axis=1)


def _nsa_slc_kernel(nblk_ref, blist_ref, q_ref, ks_ref, vs_ref, kw_ref, vw_ref,
                    selb_ref, gate_ref, ocmpT_ref, alibi_ref, wpc_ref, o_ref,
                    qaug, m_ref, l_ref, acc_ref, owin_ref):
    b = pl.program_id(0)
    g = pl.program_id(1)
    c = pl.program_id(2)
    n_chunk = pl.num_programs(2)
    nsp = selb_ref.shape[2]
    row = (b * N_KV_GROUPS + g) * n_chunk + c
    n_act = nblk_ref[row]
    st = (m_ref, l_ref, acc_ref)
    _fill_qaug(qaug, q_ref, alibi_ref)

    lane = lax.broadcasted_iota(jnp.int32, (SEL_BLOCK, LANES), 1)
    r_val = lax.broadcasted_iota(jnp.int32, (SEL_BLOCK, LANES), 0).astype(F32)
    ki = lax.broadcasted_iota(jnp.int32, (Q_BLOCK, Q_BLOCK), 0)
    qi = lax.broadcasted_iota(jnp.int32, (Q_BLOCK, Q_BLOCK), 1)
    causal_bias = jnp.where(ki <= qi, 0.0, NEG_INF)
    tail_bias = jnp.where(ki > qi, 0.0, NEG_INF)

    def pair_cols(first_block_rel):
        return jnp.concatenate(
            [_pos_cols(float(SEL_BLOCK * (first_block_rel + i)), r_val, lane) for i in range(2)], axis=0)

    def finish():
        return acc_ref[...] / jnp.maximum(l_ref[pl.ds(0, 1), :], TINY)

    n_wt = WIN_TILES
    base = pl.multiple_of(c * Q_BLOCK, Q_BLOCK)

    @pl.when(c >= n_wt)
    def _():
        start = pl.multiple_of((c - n_wt) * Q_BLOCK, Q_BLOCK)
        kaug = jnp.concatenate([kw_ref[pl.ds(start, WIN_KEYS), :], wpc_ref[...]], axis=1)
        s = jnp.dot(kaug, qaug[...], preferred_element_type=F32)
        s = jnp.concatenate([_bias_rows(s[:Q_BLOCK], tail_bias), s[Q_BLOCK:WINDOW],
                             _bias_rows(s[WINDOW:], causal_bias)], axis=0)
        p = jnp.exp2(s - jnp.max(s, axis=0, keepdims=True))
        l = jnp.sum(p, axis=0, keepdims=True)
        pv = _dot_ta(vw_ref[pl.ds(start, WIN_KEYS), :], p.astype(BF16))
        owin_ref[...] = pv / jnp.maximum(l, TINY)

    @pl.when(c < n_wt)
    def _():
        _softmax_tile(st, qaug, kw_ref[pl.ds(base, Q_BLOCK), :], pair_cols(0),
                      vw_ref[pl.ds(base, Q_BLOCK), :], causal_bias, True)
        for w in range(1, n_wt):
            @pl.when(c - n_wt + w >= 0)
            def _(w=w):
                start = pl.multiple_of((c - n_wt + w) * Q_BLOCK, Q_BLOCK)
                _softmax_tile(st, qaug, kw_ref[pl.ds(start, Q_BLOCK), :],
                              pair_cols(2 * (w - n_wt)), vw_ref[pl.ds(start, Q_BLOCK), :],
                              None, False)
        owin_ref[...] = finish()

    _softmax_tile(st, qaug, ks_ref[pl.ds(base, Q_BLOCK), :], pair_cols(0),
                  vs_ref[pl.ds(base, Q_BLOCK), :], causal_bias, True)
    sub = lax.broadcasted_iota(jnp.int32, (16, GROUP_W), 0)

    def slc_tiles(pos0, n_tiles, per_tile):
        slot_rows = jnp.zeros((16, GROUP_W), F32)
        tiles = []
        for ti in range(n_tiles):
            ks_t, vs_t, pcs = [], [], []
            for k in range(per_tile):
                slot = ti * per_tile + k
                pos = pos0 + slot
                live = pos < n_act
                j = jnp.where(live, blist_ref[row * nsp + jnp.minimum(pos, nsp - 1)], 0)
                off = pl.multiple_of(j * SEL_BLOCK, SEL_BLOCK)
                ks_t.append(ks_ref[pl.ds(off, SEL_BLOCK), :])
                vs_t.append(vs_ref[pl.ds(off, SEL_BLOCK), :])
                a_val = ((j - 2 * c) * SEL_BLOCK).astype(F32)
                pcs.append(_pos_cols(a_val, r_val, lane, slot_lane=ROW_SLOT + slot))
                rowk = selb_ref[0, 0, pl.ds(j, 1), :]
                rowk = jnp.where(live, rowk, NEG_INF)
                slot_rows = jnp.where(sub == slot, jnp.tile(rowk, (1, HEADS_PER_GROUP)), slot_rows)
            tiles.append((jnp.concatenate(ks_t, axis=0), jnp.concatenate(pcs, axis=0),
                          jnp.concatenate(vs_t, axis=0)))
        qaug[pl.ds(HEAD_DIM + ROW_SLOT, 16), :] = slot_rows.astype(BF16)
        scores = [_tile_scores(qaug, k_tile, pc) for k_tile, pc, _ in tiles]
        for s, (_, _, v_tile) in zip(scores, tiles):
            _softmax_update(st, s, v_tile, None, False)

    pair = 2 * SLC_GROUP
    rem = n_act & (pair - 1)
    n_pairs = jnp.right_shift(n_act, 4) + jnp.where(rem > SLC_GROUP, 1, 0)
    n_tail = jnp.where(rem > SLC_GROUP, 0, jnp.right_shift(rem + SLC_TAIL - 1, 2))

    def pair_body(i, carry):
        slc_tiles(i * pair, 2, SLC_GROUP)
        return carry

    def tail_body(i, carry):
        slc_tiles(n_pairs * pair + i * SLC_TAIL, 1, SLC_TAIL)
        return carry

    lax.fori_loop(0, n_pairs, pair_body, 0)
    lax.fori_loop(0, n_tail, tail_body, 0)
    o_slcT = finish()

    gT = jax.nn.sigmoid(gate_ref[...].astype(F32).T)
    o_cmpT = ocmpT_ref[0, 0]
    o_winT = owin_ref[...]
    outs = []
    for h in range(HEADS_PER_GROUP):
        hs = slice(h * Q_BLOCK, (h + 1) * Q_BLOCK)
        oT = (gT[3 * h:3 * h + 1, :] * o_cmpT[:, hs] + gT[3 * h + 1:3 * h + 2, :] * o_slcT[:, hs]
              + gT[3 * h + 2:3 * h + 3, :] * o_winT[:, hs])
        outs.append(oT.T)
    o_ref[...] = jnp.concatenate(outs, axis=1).astype(o_ref.dtype)


def _nsa_slc(proj, selb, nblk, blist, ocmpT, alibi, B, S):
    T = B * S
    C = S // Q_BLOCK
    G = N_KV_GROUPS
    nsp = selb.shape[2]
    gw = HEADS_PER_GROUP * HEAD_DIM
    kvspec = lambda col: pl.BlockSpec(
        (S, HEAD_DIM), lambda b, g, c, *_: (b, col // HEAD_DIM + g))
    tspec = lambda rows: pl.BlockSpec((1, 1, rows, GROUP_W), lambda b, g, c, *_: (b * C + c, g, 0, 0))
    grid_spec = pltpu.PrefetchScalarGridSpec(
        num_scalar_prefetch=2,
        grid=(B, G, C),
        in_specs=[pl.BlockSpec((Q_BLOCK, gw), lambda b, g, c, *_: (b * C + c, COL_Q // gw + g)),
                  kvspec(COL_KS), kvspec(COL_VS), kvspec(COL_KW), kvspec(COL_VW),
                  pl.BlockSpec((1, 1, nsp, Q_BLOCK), lambda b, g, c, *_: (b * C + c, g, 0, 0)),
                  pl.BlockSpec((Q_BLOCK, LANES), lambda b, g, c, *_: (b * C + c, COL_GATE // LANES + g)),
                  tspec(HEAD_DIM),
                  pl.BlockSpec((1, 16, GROUP_W), lambda b, g, c, *_: (g, 0, 0)),
                  pl.BlockSpec((WIN_KEYS, LANES), lambda b, g, c, *_: (0, 0))],
        out_specs=pl.BlockSpec((Q_BLOCK, gw), lambda b, g, c, *_: (b * C + c, g)),
        scratch_shapes=[pltpu.VMEM((KAUG, GROUP_W), BF16),
                        pltpu.VMEM((SUBLANES, GROUP_W), F32),
                        pltpu.VMEM((SUBLANES, GROUP_W), F32),
                        pltpu.VMEM((HEAD_DIM, GROUP_W), F32),
                        pltpu.VMEM((HEAD_DIM, GROUP_W), F32)],
    )
    return pl.pallas_call(
        _nsa_slc_kernel,
        out_shape=jax.ShapeDtypeStruct((T, Q_DIM), BF16),
        grid_spec=grid_spec,
        compiler_params=_cparams(("arbitrary", "arbitrary", "arbitrary")),
        name="nsa_slc",
    )(nblk, blist, proj, proj, proj, proj, proj, selb, proj, ocmpT, alibi, _window_pos_cols())


CONV_TS = 512
CONV_HALO = 32
CONV_RC = 64
CONV_CC = 256


def _conv_kernel(a_ref, b_ref, w_ref, cb_ref, g_ref, beta_ref, o_ref, hbuf, ybuf, wrep):
    si = pl.program_id(1)

    @pl.when(si == 0)
    def _():
        hbuf[pl.ds(0, CONV_HALO), :] = jnp.zeros((CONV_HALO, CONV_CH), F32)

    @pl.when(si > 0)
    def _():
        hbuf[pl.ds(0, CONV_HALO), :] = hbuf[pl.ds(CONV_TS, CONV_HALO), :]

    a = a_ref[...].astype(F32)
    bb = b_ref[...].astype(F32)
    hbuf[pl.ds(CONV_HALO, CONV_TS), :] = a * jax.nn.sigmoid(bb)

    for k in range(CONV_WIDTH):
        wrep[pl.ds(k * SUBLANES, SUBLANES), :] = jnp.broadcast_to(w_ref[k:k + 1, :],
                                                                  (SUBLANES, CONV_CH))

    off = CONV_HALO - (CONV_WIDTH - 1)
    groups = CONV_RC // SUBLANES
    for cc in range(CONV_CH // CONV_CC):
        cols = slice(cc * CONV_CC, (cc + 1) * CONV_CC)

        def row_body(r, carry, cols=cols):
            r0 = pl.multiple_of(r * CONV_RC, CONV_RC)
            win = hbuf[pl.ds(r0, CONV_RC + CONV_HALO), cols]
            acc = jnp.zeros((groups, SUBLANES, CONV_CC), F32)
            for res in range(SUBLANES):
                taps = [k for k in range(CONV_WIDTH) if (off + k) % SUBLANES == res]
                span = max(off + k for k in taps) - res + CONV_RC
                shifted = win[res:res + span, :]
                for k in taps:
                    a0 = off + k - res
                    rows = shifted[a0:a0 + CONV_RC, :].reshape(groups, SUBLANES, CONV_CC)
                    acc = acc + rows * wrep[pl.ds(k * SUBLANES, SUBLANES), cols][None]
            ybuf[pl.ds(r0, CONV_RC), cols] = acc.reshape(CONV_RC, CONV_CC)
            return carry

        lax.fori_loop(0, CONV_TS // CONV_RC, row_body, 0)

    y = ybuf[...] + cb_ref[...]
    mu = jnp.mean(y, axis=-1, keepdims=True)
    yc = y - mu
    var = jnp.mean(yc * yc, axis=-1, keepdims=True)
    z = yc * lax.rsqrt(var + LN_EPS) * g_ref[...] + beta_ref[...]
    o_ref[...] = (z * jax.nn.sigmoid(z)).astype(o_ref.dtype)


def _conv(proj, conv_w, conv_b, ln_g, ln_b, B, S):
    T = B * S
    ts = CONV_TS
    assert S % ts == 0
    nS = S // ts
    ca = COL_GLU // CONV_CH
    vec = lambda: pl.BlockSpec((1, CONV_CH), lambda b, s: (0, 0))
    return pl.pallas_call(
        _conv_kernel,
        out_shape=jax.ShapeDtypeStruct((T, CONV_CH), BF16),
        grid=(B, nS),
        in_specs=[pl.BlockSpec((ts, CONV_CH), lambda b, s: (b * nS + s, ca)),
                  pl.BlockSpec((ts, CONV_CH), lambda b, s: (b * nS + s, ca + 1)),
                  pl.BlockSpec((CONV_WIDTH, CONV_CH), lambda b, s: (0, 0)),
                  vec(), vec(), vec()],
        out_specs=pl.BlockSpec((ts, CONV_CH), lambda b, s: (b * nS + s, 0)),
        scratch_shapes=[pltpu.VMEM((CONV_HALO + ts, CONV_CH), F32),
                        pltpu.VMEM((ts, CONV_CH), F32),
                        pltpu.VMEM((CONV_WIDTH * SUBLANES, CONV_CH), F32)],
        compiler_params=_cparams(("arbitrary", "arbitrary")),
        name="conformer_conv",
    )(proj, proj, conv_w, conv_b.reshape(1, -1), ln_g.reshape(1, -1), ln_b.reshape(1, -1))


def _merge_kernel(o_ref, h_ref, wa_ref, wb_ref, ga_ref, gb_ref, out_ref):
    ya = jnp.dot(o_ref[...], wa_ref[...], preferred_element_type=F32)
    yb = jnp.dot(h_ref[...], wb_ref[...], preferred_element_type=F32)
    ga = jax.nn.sigmoid(ga_ref[...].astype(F32))
    gb = jax.nn.sigmoid(gb_ref[...].astype(F32))
    out_ref[...] = (ga * ya + gb * yb).astype(out_ref.dtype)


def _merge(o_nsa, h_conv, wa, wb, proj):
    T = o_nsa.shape[0]
    tm = min(512, T)
    tn = 1024
    nN = D_MODEL // tn
    ga0 = COL_MERGE // tn
    return pl.pallas_call(
        _merge_kernel,
        out_shape=jax.ShapeDtypeStruct((T, D_MODEL), BF16),
        grid=(T // tm, nN),
        in_specs=[pl.BlockSpec((tm, Q_DIM), lambda i, j: (i, 0)),
                  pl.BlockSpec((tm, CONV_CH), lambda i, j: (i, 0)),
                  pl.BlockSpec((Q_DIM, tn), lambda i, j: (0, j)),
                  pl.BlockSpec((CONV_CH, tn), lambda i, j: (0, j)),
                  pl.BlockSpec((tm, tn), lambda i, j: (i, ga0 + j)),
                  pl.BlockSpec((tm, tn), lambda i, j: (i, ga0 + nN + j))],
        out_specs=pl.BlockSpec((tm, tn), lambda i, j: (i, j)),
        compiler_params=_cparams(("arbitrary", "arbitrary")),
        name="merge",
    )(o_nsa, h_conv, wa, wb, proj, proj)


def _layer_norm_rows(y, g, b):
    mu = jnp.mean(y, axis=-1, keepdims=True)
    yc = y - mu
    var = jnp.mean(yc * yc, axis=-1, keepdims=True)
    return yc * lax.rsqrt(var + LN_EPS) * g + b


def _split2(v):
    hi = v.astype(BF16)
    return hi, (v - hi.astype(F32)).astype(BF16)


def _outproj_kernel(mix_ref, w_ref, x_ref, g_ref, b_ref, rw_hi_ref, rw_lo_ref, rb_ref,
                    x1_ref, x1b_ref, ti_ref, tg_ref):
    m = jnp.dot(mix_ref[...], w_ref[...], preferred_element_type=F32)
    x1 = _layer_norm_rows(DEEPNORM_ALPHA * x_ref[...] + m, g_ref[...], b_ref[...])
    x1_ref[...] = x1
    x1b_ref[...] = x1.astype(BF16)
    x_hi, x_lo = _split2(x1)
    logits = (jnp.dot(x_hi, rw_hi_ref[...], preferred_element_type=F32)
              + jnp.dot(x_hi, rw_lo_ref[...], preferred_element_type=F32)
              + jnp.dot(x_lo, rw_hi_ref[...], preferred_element_type=F32)) + rb_ref[...]
    lane = lax.broadcasted_iota(jnp.int32, logits.shape, 1)
    logits = jnp.where(lane < N_EXPERTS, logits, -jnp.inf)
    ti = jnp.zeros(logits.shape, jnp.int32)
    tv = jnp.zeros(logits.shape, F32)
    top0 = None
    den = jnp.zeros((logits.shape[0], 1), F32)
    for k in range(TOP_K):
        mx = jnp.max(logits, axis=-1, keepdims=True)
        idx = jnp.min(jnp.where(logits == mx, lane, LANES), axis=-1, keepdims=True)
        if top0 is None:
            top0 = mx
        e = jnp.exp(mx - top0)
        den = den + e
        ti = jnp.where(lane == k, idx, ti)
        tv = jnp.where(lane == k, e, tv)
        logits = jnp.where(lane == idx, -jnp.inf, logits)
    ti_ref[...] = ti
    tg_ref[...] = tv / den


def _outproj(mix, w_out, x2d, g, b, rw_hi, rw_lo, rb):
    T = mix.shape[0]
    tm = min(256, T)
    full = lambda shape: pl.BlockSpec(shape, lambda i: (0, 0))
    rowb = lambda w: pl.BlockSpec((tm, w), lambda i: (i, 0))
    return pl.pallas_call(
        _outproj_kernel,
        out_shape=(jax.ShapeDtypeStruct((T, D_MODEL), F32),
                   jax.ShapeDtypeStruct((T, D_MODEL), BF16),
                   jax.ShapeDtypeStruct((T, LANES), jnp.int32),
                   jax.ShapeDtypeStruct((T, LANES), F32)),
        grid=(T // tm,),
        in_specs=[rowb(D_MODEL), full((D_MODEL, D_MODEL)), rowb(D_MODEL),
                  full((1, D_MODEL)), full((1, D_MODEL)),
                  full((D_MODEL, LANES)), full((D_MODEL, LANES)), full((1, LANES))],
        out_specs=(rowb(D_MODEL), rowb(D_MODEL), rowb(LANES), rowb(LANES)),
        compiler_params=_cparams(("arbitrary",)),
        name="outproj_ln_router",
    )(mix, w_out, x2d, g, b, rw_hi, rw_lo, rb)


def _moe_up_kernel(be_ref, nused_ref, x_ref, wg_ref, wu_ref, bg_ref, bu_ref, *rest, base):
    h_ref, wgb_ref, wub_ref = rest[-3:]
    local = pl.program_id(1)
    r = base + local
    prev = be_ref[jnp.maximum(r - 1, 0)]
    fresh = (local == 0) | (be_ref[r] != prev)

    @pl.when(fresh)
    def _():
        wgb_ref[...] = wg_ref[0].astype(BF16)
        wub_ref[...] = wu_ref[0].astype(BF16)

    @pl.when(r < nused_ref[0])
    def _():
        x = x_ref[...]
        gt = jnp.dot(x, wgb_ref[...], preferred_element_type=F32) + bg_ref[0]
        up = jnp.dot(x, wub_ref[...], preferred_element_type=F32) + bu_ref[0]
        gt = jnp.minimum(gt, SWIGLU_LIMIT)
        up = jnp.clip(up, -SWIGLU_LIMIT, SWIGLU_LIMIT)
        h = gt * jax.nn.sigmoid(SWIGLU_ALPHA * gt) * (up + 1.0)
        h_ref[...] = h.astype(h_ref.dtype)


def _moe_row_maps(nblk):
    rd = lambda r, nu: jnp.minimum(r, nu[0] - 1)
    wr = lambda r, nu: jnp.where(r < nu[0], r, nblk)
    return rd, wr


def _moe_up(blk_e, n_used, xs_seg, seg, nblk, hbuf, w_gate, w_up, b_gate, b_up):
    bm, tf = MOE_BLOCK, MOE_UP_TILE
    seg_blk = xs_seg.shape[0] // bm
    base = seg * seg_blk
    rd = lambda r, nu: jnp.clip(nu[0] - 1 - base, 0, r)
    wr = lambda r, nu: jnp.where(base + r < nu[0], base + r, nblk)
    wspec = pl.BlockSpec((1, D_MODEL, tf), lambda f, r, be, nu: (be[base + r], 0, f))
    bspec = pl.BlockSpec((1, 1, tf), lambda f, r, be, nu: (be[base + r], 0, f))
    in_specs = [pl.BlockSpec((bm, D_MODEL), lambda f, r, be, nu: (rd(r, nu), 0)),
                wspec, wspec, bspec, bspec]
    args = [blk_e, n_used, xs_seg, w_gate, w_up, b_gate.reshape(N_EXPERTS, 1, D_FF),
            b_up.reshape(N_EXPERTS, 1, D_FF)]
    aliases = {}
    if hbuf is not None:
        in_specs.append(pl.BlockSpec(memory_space=pl.ANY))
        aliases = {len(args): 0}
        args.append(hbuf)
    grid_spec = pltpu.PrefetchScalarGridSpec(
        num_scalar_prefetch=2,
        grid=(D_FF // tf, seg_blk),
        in_specs=in_specs,
        out_specs=pl.BlockSpec((bm, tf), lambda f, r, be, nu: (wr(r, nu), f)),
        scratch_shapes=[pltpu.VMEM((D_MODEL, tf), BF16), pltpu.VMEM((D_MODEL, tf), BF16)],
    )
    return pl.pallas_call(
        functools.partial(_moe_up_kernel, base=base),
        out_shape=jax.ShapeDtypeStruct(((nblk + 1) * bm, D_FF), BF16),
        grid_spec=grid_spec,
        input_output_aliases=aliases,
        compiler_params=_cparams(("arbitrary", "arbitrary")),
        name="moe_up",
    )(*args)


def _moe_down_kernel(be_ref, nused_ref, h_ref, wd_ref, bd_ref, o_ref, wdb_ref):
    r = pl.program_id(0)
    prev = be_ref[jnp.maximum(r - 1, 0)]
    fresh = (r == 0) | (be_ref[r] != prev)

    @pl.when(fresh)
    def _():
        wdb_ref[...] = wd_ref[0].astype(BF16)

    @pl.when(r < nused_ref[0])
    def _():
        y = jnp.dot(h_ref[...], wdb_ref[...], preferred_element_type=F32) + bd_ref[0]
        o_ref[...] = y.astype(o_ref.dtype)


def _moe_down(blk_e, n_used, h, w_down, b_down):
    bm = MOE_BLOCK
    nblk = h.shape[0] // bm - 1
    rd, wr = _moe_row_maps(nblk)
    grid_spec = pltpu.PrefetchScalarGridSpec(
        num_scalar_prefetch=2,
        grid=(nblk,),
        in_specs=[pl.BlockSpec((bm, D_FF), lambda r, be, nu: (rd(r, nu), 0)),
                  pl.BlockSpec((1, D_FF, D_MODEL), lambda r, be, nu: (be[r], 0, 0)),
                  pl.BlockSpec((1, 1, D_MODEL), lambda r, be, nu: (be[r], 0, 0))],
        out_specs=pl.BlockSpec((bm, D_MODEL), lambda r, be, nu: (wr(r, nu), 0)),
        scratch_shapes=[pltpu.VMEM((D_FF, D_MODEL), BF16)],
    )
    return pl.pallas_call(
        _moe_down_kernel,
        out_shape=jax.ShapeDtypeStruct(((nblk + 1) * bm, D_MODEL), BF16),
        grid_spec=grid_spec,
        compiler_params=_cparams(("arbitrary",)),
        name="moe_down",
    )(blk_e, n_used, h, w_down, b_down.reshape(N_EXPERTS, 1, D_MODEL))


def _final_kernel(x1_ref, *refs):
    y_refs, (tg_ref, g_ref, b_ref, o_ref) = refs[:TOP_K], refs[TOP_K:]
    tg = tg_ref[...]
    f = tg[:, 0:1] * y_refs[0][...].astype(F32)
    for k in range(1, TOP_K):
        f = f + tg[:, k:k + 1] * y_refs[k][...].astype(F32)
    o_ref[...] = _layer_norm_rows(DEEPNORM_ALPHA * x1_ref[...] + f, g_ref[...], b_ref[...])


def _final(x1, ys, tg, g, b):
    T = x1.shape[0]
    tm = min(256, T)
    rows = pl.BlockSpec((tm, D_MODEL), lambda i: (i, 0))
    vec = pl.BlockSpec((1, D_MODEL), lambda i: (0, 0))
    return pl.pallas_call(
        _final_kernel,
        out_shape=jax.ShapeDtypeStruct((T, D_MODEL), F32),
        grid=(T // tm,),
        in_specs=[rows] + [rows] * TOP_K + [pl.BlockSpec((tm, LANES), lambda i: (i, 0)), vec, vec],
        out_specs=rows,
        compiler_params=_cparams(("arbitrary",)),
        name="combine_ln",
    )(x1, *ys, tg, g, b)


def _prep_w_in(w_in):
    splits = np.cumsum([Q_DIM] + [KV_DIM] * 6 + [NSA_GATE_DIM, 2 * CONV_CH, 2 * D_MODEL])
    q = w_in[:, :splits[0]] * (LOG2E / math.sqrt(HEAD_DIM))
    kv = w_in[:, splits[0]:splits[6]]
    gate = w_in[:, splits[6]:splits[7]]
    glu = w_in[:, splits[7]:splits[8]]
    merge = w_in[:, splits[8]:splits[9]]
    per_g = HEADS_PER_GROUP * 3
    gate_tiles = [jnp.pad(gate[:, g * per_g:(g + 1) * per_g], ((0, 0), (0, LANES - per_g)))
                  for g in range(N_KV_GROUPS)]
    return jnp.concatenate([q, glu, merge, kv] + gate_tiles, axis=1).astype(BF16)


def _mixer(x2d, B, S, w_in, cmp_pe, cmp_w1, cmp_b1, cmp_w2, cmp_b2, w_nsa_proj,
           conv_w, conv_b, conv_ln_g, conv_ln_b, w_conv_proj):
    T = B * S
    G = N_KV_GROUPS
    proj = _inproj(x2d, _prep_w_in(w_in))

    kcvc = _compress(proj, cmp_pe.reshape(2, 1, CMP_BLOCK * HEAD_DIM),
                     cmp_w1.astype(BF16), cmp_b1.reshape(2, 1, HEAD_DIM),
                     cmp_w2.astype(BF16), cmp_b2.reshape(2, 1, HEAD_DIM), B, S)

    C = S // Q_BLOCK
    alibi = _alibi_rows().astype(BF16)
    ocmpT, selb = _nsa_cmp(proj, kcvc, alibi, B, S)
    nsp = selb.shape[2]
    fl = jnp.max(selb, axis=3) > 0.5 * NEG_INF
    fl = fl.reshape(B, C, G, nsp).transpose(0, 2, 1, 3)
    past = (jnp.arange(nsp, dtype=jnp.int32)[None, :]
            < (Q_BLOCK // SEL_BLOCK) * jnp.arange(C, dtype=jnp.int32)[:, None])
    fl = (fl & past[None, None]).reshape(B * G * C, nsp)
    nblk = jnp.sum(fl, axis=1).astype(jnp.int32)
    blist = jnp.argsort(jnp.logical_not(fl), axis=1, stable=True).astype(jnp.int32)
    o_nsa = _nsa_slc(proj, selb, nblk, blist.reshape(-1), ocmpT, alibi, B, S)

    h_conv = _conv(proj, conv_w, conv_b, conv_ln_g, conv_ln_b, B, S)
    return _merge(o_nsa, h_conv, w_nsa_proj.astype(BF16), w_conv_proj.astype(BF16), proj)


def _moe(x1b, top_i, w_gate, b_gate, w_up, b_up, w_down, b_down):
    T = x1b.shape[0]
    A = T * TOP_K
    bm = MOE_BLOCK
    flat_e = top_i[:, :TOP_K].reshape(A)
    onehot = (flat_e[:, None] == jnp.arange(N_EXPERTS, dtype=jnp.int32)[None, :]).astype(jnp.int32)
    csum = jnp.cumsum(onehot, axis=0)
    rank = jnp.sum(onehot * csum, axis=1) - 1
    counts = csum[-1]
    padded = (counts + bm - 1) // bm * bm
    pad_end = jnp.cumsum(padded)
    pad_start = pad_end - padded
    dest = pad_start[flat_e] + rank
    P = -(-A // bm) * bm + N_EXPERTS * bm
    nblk = P // bm
    flat_tok = jnp.arange(A, dtype=jnp.int32) // TOP_K
    buf_tok = jnp.zeros((P,), jnp.int32).at[dest].set(flat_tok)
    blk_start = jnp.arange(nblk, dtype=jnp.int32) * bm
    blk_e = jnp.minimum(jnp.sum(blk_start[:, None] >= pad_end[None, :], axis=1),
                        N_EXPERTS - 1).astype(jnp.int32)
    n_used = (pad_end[-1] // bm).astype(jnp.int32).reshape(1)
    seg_rows = P // MOE_SEGMENTS
    assert seg_rows % bm == 0
    h = None
    for seg in range(MOE_SEGMENTS):
        xs_seg = x1b[buf_tok[seg * seg_rows:(seg + 1) * seg_rows]]
        h = _moe_up(blk_e, n_used, xs_seg, seg, nblk, h, w_gate, w_up, b_gate, b_up)
    out = _moe_down(blk_e, n_used, h, w_down, b_down)
    dest_k = dest.reshape(T, TOP_K)
    return [out[dest_k[:, k]] for k in range(TOP_K)]


def kernel(x, w_in, cmp_pe, cmp_w1, cmp_b1, cmp_w2, cmp_b2, w_nsa_proj, conv_w, conv_b, conv_ln_g, conv_ln_b, w_conv_proj, w_out, ln1_g, ln1_b, router_w, router_b, w_gate, b_gate, w_up, b_up, w_down, b_down, ln2_g, ln2_b):
    B, S, D = x.shape
    T = B * S
    x2d = x.reshape(T, D)
    for l in range(DEPTH):
        mix = _mixer(x2d, B, S, w_in[l], cmp_pe[l], cmp_w1[l], cmp_b1[l], cmp_w2[l], cmp_b2[l],
                     w_nsa_proj[l], conv_w[l], conv_b[l], conv_ln_g[l], conv_ln_b[l],
                     w_conv_proj[l])
        rw = jnp.pad(router_w[l], ((0, 0), (0, LANES - N_EXPERTS)))
        rw_hi = rw.astype(BF16)
        rw_lo = (rw - rw_hi.astype(F32)).astype(BF16)
        rb = jnp.pad(router_b[l], (0, LANES - N_EXPERTS)).reshape(1, LANES)
        x1, x1b, top_i, top_g = _outproj(
            mix, w_out[l].astype(BF16), x2d, ln1_g[l].reshape(1, D), ln1_b[l].reshape(1, D),
            rw_hi, rw_lo, rb)
        y4 = _moe(x1b, top_i, w_gate[l], b_gate[l], w_up[l], b_up[l], w_down[l], b_down[l])
        x2d = _final(x1, y4, top_g, ln2_g[l].reshape(1, D), ln2_b[l].reshape(1, D))
    return x2d.reshape(B, S, D)
```

```python
import functools
import math

import numpy as np
import jax
import jax.numpy as jnp
from jax import lax
from jax.experimental import pallas as pl
from jax.experimental.pallas import tpu as pltpu

D_MODEL = 2048
N_HEADS = 16
N_KV_GROUPS = 2
HEADS_PER_GROUP = N_HEADS // N_KV_GROUPS
HEAD_DIM = 128
CMP_BLOCK = 32
CMP_STRIDE = 16
SEL_BLOCK = 64
SEL_SHIFT = 6
SEL_TOP_N = 16
WINDOW = 512
Q_BLOCK = 128
N_OVERLAP = (SEL_BLOCK + CMP_BLOCK) // CMP_STRIDE - 1
FORCE_BONUS = 1.0e4
CONV_CH = D_MODEL // 2
CONV_WIDTH = 31
N_EXPERTS = 32
TOP_K = 4
D_FF = D_MODEL
SWIGLU_LIMIT = 7.0
SWIGLU_ALPHA = 1.702
LN_EPS = 1e-5
DEPTH = 1
DEEPNORM_ALPHA = (2 * DEPTH) ** 0.25
NEG_INF = -1e30
TINY = 1e-30

Q_DIM = N_HEADS * HEAD_DIM
KV_DIM = N_KV_GROUPS * HEAD_DIM
NSA_GATE_DIM = N_HEADS * 3

LANES = 128
SUBLANES = 8
VMEM_LIMIT = 56 * 1024 * 1024

COL_Q = 0
COL_GLU = COL_Q + Q_DIM
COL_MERGE = COL_GLU + 2 * CONV_CH
COL_KC = COL_MERGE + 2 * D_MODEL
COL_VC = COL_KC + KV_DIM
COL_KS = COL_VC + KV_DIM
COL_VS = COL_KS + KV_DIM
COL_KW = COL_VS + KV_DIM
COL_VW = COL_KW + KV_DIM
COL_GATE = COL_VW + KV_DIM
PROJ_W = COL_GATE + N_KV_GROUPS * LANES

BF16 = jnp.bfloat16
F32 = jnp.float32

MOE_BLOCK = 256
MOE_UP_TILE = 1024
MOE_SEGMENTS = 4

_ALIBI = np.exp2(-8.0 * np.arange(1, N_HEADS + 1, dtype=np.float32) / N_HEADS).astype(np.float32)
_ALIBI = _ALIBI.reshape(N_KV_GROUPS, HEADS_PER_GROUP)


def _cparams(sem, vmem=VMEM_LIMIT):
    return pltpu.CompilerParams(dimension_semantics=sem, vmem_limit_bytes=vmem)


def _inproj_kernel(x_ref, w_ref, o_ref, xb_ref):
    @pl.when(pl.program_id(1) == 0)
    def _():
        xb_ref[...] = x_ref[...].astype(BF16)

    o_ref[...] = jnp.dot(xb_ref[...], w_ref[...],
                         preferred_element_type=F32).astype(o_ref.dtype)


def _inproj(x2d, w_p):
    T, D = x2d.shape
    N = w_p.shape[1]
    tm = min(512, T)
    tn = N // 3
    assert T % tm == 0 and N % tn == 0 and tn % (2 * LANES) == 0
    return pl.pallas_call(
        _inproj_kernel,
        out_shape=jax.ShapeDtypeStruct((T, N), BF16),
        grid=(T // tm, N // tn),
        in_specs=[pl.BlockSpec((tm, D), lambda i, j: (i, 0)),
                  pl.BlockSpec((D, tn), lambda i, j: (0, j))],
        out_specs=pl.BlockSpec((tm, tn), lambda i, j: (i, j)),
        scratch_shapes=[pltpu.VMEM((tm, D), BF16)],
        compiler_params=_cparams(("arbitrary", "arbitrary")),
        name="inproj",
    )(x2d, w_p)


def _gelu_tanh(x):
    c = math.sqrt(2.0 / math.pi)
    return 0.5 * x * (1.0 + jnp.tanh(c * (x + 0.044715 * (x * x * x))))


def _compress_kernel(x_ref, pe_ref, w1_ref, b1_ref, w2_ref, b2_ref, o_ref, xf_ref):
    S = x_ref.shape[0]
    n16 = S // CMP_STRIDE
    xf_ref[...] = x_ref[...].astype(F32)
    top = jnp.zeros((n16, HEAD_DIM), F32)
    bot = jnp.zeros((n16, HEAD_DIM), F32)
    for j in range(CMP_STRIDE):
        xj = xf_ref[pl.ds(j, n16, stride=CMP_STRIDE), :]
        lo, hi = j * HEAD_DIM, (CMP_STRIDE + j) * HEAD_DIM
        top = top + jnp.dot((xj + pe_ref[0, :, lo:lo + HEAD_DIM]).astype(BF16),
                            w1_ref[0, lo:lo + HEAD_DIM, :], preferred_element_type=F32)
        bot = bot + jnp.dot((xj + pe_ref[0, :, hi:hi + HEAD_DIM]).astype(BF16),
                            w1_ref[0, hi:hi + HEAD_DIM, :], preferred_element_type=F32)
    pre = top + pltpu.roll(bot, n16 - 1, 0) + b1_ref[0]
    h = _gelu_tanh(pre)
    o = jnp.dot(h.astype(BF16), w2_ref[0], preferred_element_type=F32) + b2_ref[0]
    o_ref[0, 0] = o.astype(o_ref.dtype)


def _compress(proj, pe, w1, b1, w2, b2, B, S):
    n_slot = 2 * N_KV_GROUPS
    n16 = S // CMP_STRIDE
    wide = CMP_BLOCK * HEAD_DIM
    kv = lambda s, b: (s // N_KV_GROUPS, 0, 0)
    return pl.pallas_call(
        _compress_kernel,
        out_shape=jax.ShapeDtypeStruct((n_slot, B, n16, HEAD_DIM), BF16),
        grid=(n_slot, B),
        in_specs=[pl.BlockSpec((S, HEAD_DIM), lambda s, b: (b, COL_KC // HEAD_DIM + s)),
                  pl.BlockSpec((1, 1, wide), kv),
                  pl.BlockSpec((1, wide, HEAD_DIM), kv),
                  pl.BlockSpec((1, 1, HEAD_DIM), kv),
                  pl.BlockSpec((1, HEAD_DIM, HEAD_DIM), kv),
                  pl.BlockSpec((1, 1, HEAD_DIM), kv)],
        out_specs=pl.BlockSpec((1, 1, n16, HEAD_DIM), lambda s, b: (s, b, 0, 0)),
        scratch_shapes=[pltpu.VMEM((S, HEAD_DIM), F32)],
        compiler_params=_cparams(("arbitrary", "arbitrary")),
        name="compress",
    )(proj, pe, w1, b1, w2, b2)


LOG2E = 1.4426950408889634
KAUG = 2 * HEAD_DIM
ROW_ALIBI = 0
ROW_SLOT = 16
SLC_GROUP = 8
SLC_TAIL = 4
assert 2 * SLC_GROUP == 16 and SLC_TAIL == 4
GROUP_W = HEADS_PER_GROUP * Q_BLOCK
WIN_TILES = WINDOW // Q_BLOCK
WIN_KEYS = WINDOW + Q_BLOCK


def _window_pos_cols():
    rel = np.arange(WIN_KEYS) - WINDOW
    a = (rel // SEL_BLOCK) * SEL_BLOCK
    r = rel - a
    cols = np.zeros((WIN_KEYS, LANES), np.float32)
    cols[:, ROW_ALIBI:ROW_ALIBI + 3] = a[:, None]
    cols[:, ROW_ALIBI + 3:ROW_ALIBI + 6] = r[:, None]
    return jnp.asarray(cols, BF16)


def _alibi_rows():
    s2 = jnp.asarray(_ALIBI * np.float32(LOG2E), F32)
    hi = s2.astype(BF16)
    r1 = s2 - hi.astype(F32)
    mid = r1.astype(BF16)
    lo = (r1 - mid.astype(F32)).astype(BF16)
    trip = jnp.stack([hi, mid, lo, hi, mid, lo], axis=1)
    rows = jnp.pad(trip, ((0, 0), (0, 16 - 6), (0, 0)))
    return jnp.repeat(rows, Q_BLOCK, axis=2)


def _dot_ta(a, b):
    return lax.dot_general(a, b, (((0,), (0,)), ((), ())), preferred_element_type=F32)


def _fill_qaug(qaug, q_ref, alibi_ref):
    for h in range(HEADS_PER_GROUP):
        qaug[pl.ds(0, HEAD_DIM), pl.ds(h * Q_BLOCK, Q_BLOCK)] = (
            q_ref[:, h * HEAD_DIM:(h + 1) * HEAD_DIM].T)
    qaug[pl.ds(HEAD_DIM, 16), :] = alibi_ref[0]
    qaug[pl.ds(HEAD_DIM + 16, HEAD_DIM - 16), :] = jnp.zeros((HEAD_DIM - 16, GROUP_W), BF16)


def _pos_cols(a_val, r_val, lane, slot_lane=None):
    base = jnp.where((lane >= ROW_ALIBI + 3) & (lane < ROW_ALIBI + 6), r_val, 0.0)
    if slot_lane is not None:
        base = jnp.where(lane == slot_lane, 1.0, base)
    return jnp.where(lane < ROW_ALIBI + 3, a_val, base)


def _nsa_cmp_kernel(q_ref, kc_ref, vc_ref, alibi_ref, ocmpT_ref, selb_ref, qaug, *, n_selb):
    c = pl.program_id(2)
    _fill_qaug(qaug, q_ref, alibi_ref)
    n_vis = c * (Q_BLOCK // CMP_STRIDE) + (Q_BLOCK - CMP_BLOCK) // CMP_STRIDE + 1
    tiles = kc_ref.shape[2] // LANES
    for i in range(1, tiles + 1):
        cond = n_vis > (i - 1) * LANES
        if i < tiles:
            cond = cond & (n_vis <= i * LANES)
        pl.when(cond)(functools.partial(
            _nsa_cmp_rows, c, i * LANES, kc_ref, vc_ref, ocmpT_ref, selb_ref, qaug, n_selb))


def _nsa_cmp_rows(c, ncp, kc_ref, vc_ref, ocmpT_ref, selb_ref, qaug, n_selb):
    nsp = selb_ref.shape[2]

    n_i = lax.broadcasted_iota(jnp.int32, (ncp, LANES), 0)
    lane = lax.broadcasted_iota(jnp.int32, (ncp, LANES), 1)
    end_rel = n_i * CMP_STRIDE + (CMP_BLOCK - 1) - c * Q_BLOCK
    a_val = jnp.left_shift(jnp.right_shift(end_rel, SEL_SHIFT), SEL_SHIFT).astype(F32)
    r_val = (end_rel & (SEL_BLOCK - 1)).astype(F32)
    pc = _pos_cols(a_val, r_val, lane).astype(BF16)
    s = jnp.dot(jnp.concatenate([kc_ref[0, 0, pl.ds(0, ncp), :], pc], axis=1), qaug[...],
                preferred_element_type=F32)

    tq = lax.broadcasted_iota(jnp.int32, (1, Q_BLOCK), 1)
    t_row = c * Q_BLOCK + tq
    n_col = lax.broadcasted_iota(jnp.int32, (ncp, Q_BLOCK), 0)
    mask_bias = jnp.where(n_col * CMP_STRIDE + (CMP_BLOCK - 1) <= t_row, 0.0, NEG_INF)
    has_valid = t_row >= CMP_BLOCK - 1

    imp = jnp.zeros((ncp, Q_BLOCK), F32)
    es, rls = [], []
    for h in range(HEADS_PER_GROUP):
        sh = s[:, h * Q_BLOCK:(h + 1) * Q_BLOCK] + mask_bias
        m = jnp.max(sh, axis=0, keepdims=True)
        e = jnp.exp2(sh - m)
        l = jnp.sum(e, axis=0, keepdims=True)
        rl = jnp.where(has_valid, 1.0 / jnp.maximum(l, TINY), 0.0)
        imp = imp + e * rl
        es.append(e.astype(BF16))
        rls.append(rl)
    oT = _dot_ta(vc_ref[0, 0, pl.ds(0, ncp), :], jnp.concatenate(es, axis=1))
    ocmpT_ref[0, 0] = oT * jnp.concatenate(rls, axis=1)

    j_i = lax.broadcasted_iota(jnp.int32, (nsp, ncp), 0)
    n_j = lax.broadcasted_iota(jnp.int32, (nsp, ncp), 1)
    lo = j_i * (SEL_BLOCK // CMP_STRIDE) - (CMP_BLOCK // CMP_STRIDE) + 1
    ovl = ((n_j >= lo) & (n_j < lo + N_OVERLAP)).astype(BF16)
    i_hi = imp.astype(BF16)
    r1 = imp - i_hi.astype(F32)
    i_mid = r1.astype(BF16)
    i_lo = (r1 - i_mid.astype(F32)).astype(BF16)
    imp_sel = (jnp.dot(ovl, i_hi, preferred_element_type=F32)
               + jnp.dot(ovl, i_mid, preferred_element_type=F32)
               + jnp.dot(ovl, i_lo, preferred_element_type=F32))

    blk = lax.broadcasted_iota(jnp.int32, (nsp, Q_BLOCK), 0)
    cur = jnp.right_shift(t_row, SEL_SHIFT)
    valid = blk * SEL_BLOCK <= t_row
    forced = (blk == 0) | (blk == cur) | (blk == cur - 1)
    score = jnp.where(valid, imp_sel + jnp.where(forced, FORCE_BONUS, 0.0), -1.0)
    score = jnp.where(blk < n_selb, score, -3.0)
    selb = jnp.full((nsp, Q_BLOCK), NEG_INF, F32)
    for _ in range(min(SEL_TOP_N, n_selb)):
        mx = jnp.max(score, axis=0, keepdims=True)
        first = jnp.min(jnp.where(score == mx, blk, nsp), axis=0, keepdims=True)
        pick = blk == first
        selb = jnp.where(pick, 0.0, selb)
        score = jnp.where(pick, -2.0, score)
    selb_ref[0, 0] = selb


def _nsa_cmp(proj, kcvc, alibi, B, S):
    C = S // Q_BLOCK
    G = N_KV_GROUPS
    ncp = kcvc.shape[2]
    n_selb = S // SEL_BLOCK
    nsp = max(LANES, n_selb)
    kern = functools.partial(_nsa_cmp_kernel, n_selb=n_selb)
    return pl.pallas_call(
        kern,
        out_shape=(jax.ShapeDtypeStruct((B * C, G, HEAD_DIM, GROUP_W), F32),
                   jax.ShapeDtypeStruct((B * C, G, nsp, Q_BLOCK), F32)),
        grid=(B, G, C),
        in_specs=[pl.BlockSpec((Q_BLOCK, HEADS_PER_GROUP * HEAD_DIM),
                               lambda b, g, c: (b * C + c, COL_Q // (HEADS_PER_GROUP * HEAD_DIM) + g)),
                  pl.BlockSpec((1, 1, ncp, HEAD_DIM), lambda b, g, c: (g, b, 0, 0)),
                  pl.BlockSpec((1, 1, ncp, HEAD_DIM), lambda b, g, c: (G + g, b, 0, 0)),
                  pl.BlockSpec((1, 16, GROUP_W), lambda b, g, c: (g, 0, 0))],
        out_specs=(pl.BlockSpec((1, 1, HEAD_DIM, GROUP_W), lambda b, g, c: (b * C + c, g, 0, 0)),
                   pl.BlockSpec((1, 1, nsp, Q_BLOCK), lambda b, g, c: (b * C + c, g, 0, 0))),
        scratch_shapes=[pltpu.VMEM((KAUG, GROUP_W), BF16)],
        compiler_params=_cparams(("arbitrary", "arbitrary", "arbitrary")),
        name="nsa_cmp",
    )(proj, kcvc, kcvc, alibi)


def _tile_scores(qaug, k_tile, pc):
    kaug = jnp.concatenate([k_tile, pc.astype(BF16)], axis=1)
    return jnp.dot(kaug, qaug[...], preferred_element_type=F32)


def _softmax_tile(st, qaug, k_tile, pc, v_tile, bias, first):
    _softmax_update(st, _tile_scores(qaug, k_tile, pc), v_tile, bias, first)


def _softmax_update(st, s, v_tile, bias, first):
    m_ref, l_ref, acc_ref = st
    if bias is not None:
        s = jnp.concatenate([s[:, h * Q_BLOCK:(h + 1) * Q_BLOCK] + bias
                             for h in range(HEADS_PER_GROUP)], axis=1)
    mx = jnp.max(s, axis=0, keepdims=True)
    if first:
        m_new = mx
    else:
        m_old = m_ref[pl.ds(0, 1), :]
        m_new = jnp.maximum(m_old, mx)
    p = jnp.exp2(s - m_new)
    ps = jnp.sum(p, axis=0, keepdims=True)
    pv = _dot_ta(v_tile, p.astype(BF16))
    if first:
        l_ref[pl.ds(0, 1), :] = ps
        acc_ref[...] = pv
    else:
        alpha = jnp.exp2(m_old - m_new)
        l_ref[pl.ds(0, 1), :] = alpha * l_ref[pl.ds(0, 1), :] + ps
        acc_ref[...] = alpha * acc_ref[...] + pv
    m_ref[pl.ds(0, 1), :] = m_new


def _bias_rows(s, bias):
    return jnp.concatenate([s[:, h * Q_BLOCK:(h + 1) * Q_BLOCK] + bias
                            for h in range(HEADS_PER_GROUP)], axis=1)


def _nsa_slc_kernel(nblk_ref, blist_ref, q_ref, ks_ref, vs_ref, kw_ref, vw_ref,
                    selb_ref, gate_ref, ocmpT_ref, alibi_ref, wpc_ref, o_ref,
                    qaug, m_ref, l_ref, acc_ref, owin_ref):
    b = pl.program_id(0)
    g = pl.program_id(1)
    c = pl.program_id(2)
    n_chunk = pl.num_programs(2)
    nsp = selb_ref.shape[2]
    row = (b * N_KV_GROUPS + g) * n_chunk + c
    n_act = nblk_ref[row]
    st = (m_ref, l_ref, acc_ref)
    _fill_qaug(qaug, q_ref, alibi_ref)

    lane = lax.broadcasted_iota(jnp.int32, (SEL_BLOCK, LANES), 1)
    r_val = lax.broadcasted_iota(jnp.int32, (SEL_BLOCK, LANES), 0).astype(F32)
    ki = lax.broadcasted_iota(jnp.int32, (Q_BLOCK, Q_BLOCK), 0)
    qi = lax.broadcasted_iota(jnp.int32, (Q_BLOCK, Q_BLOCK), 1)
    causal_bias = jnp.where(ki <= qi, 0.0, NEG_INF)
    tail_bias = jnp.where(ki > qi, 0.0, NEG_INF)

    def pair_cols(first_block_rel):
        return jnp.concatenate(
            [_pos_cols(float(SEL_BLOCK * (first_block_rel + i)), r_val, lane) for i in range(2)], axis=0)

    def finish():
        return acc_ref[...] / jnp.maximum(l_ref[pl.ds(0, 1), :], TINY)

    n_wt = WIN_TILES
    base = pl.multiple_of(c * Q_BLOCK, Q_BLOCK)

    @pl.when(c >= n_wt)
    def _():
        start = pl.multiple_of((c - n_wt) * Q_BLOCK, Q_BLOCK)
        kaug = jnp.concatenate([kw_ref[pl.ds(start, WIN_KEYS), :], wpc_ref[...]], axis=1)
        s = jnp.dot(kaug, qaug[...], preferred_element_type=F32)
        s = jnp.concatenate([_bias_rows(s[:Q_BLOCK], tail_bias), s[Q_BLOCK:WINDOW],
                             _bias_rows(s[WINDOW:], causal_bias)], axis=0)
        p = jnp.exp2(s - jnp.max(s, axis=0, keepdims=True))
        l = jnp.sum(p, axis=0, keepdims=True)
        pv = _dot_ta(vw_ref[pl.ds(start, WIN_KEYS), :], p.astype(BF16))
        owin_ref[...] = pv / jnp.maximum(l, TINY)

    @pl.when(c < n_wt)
    def _():
        _softmax_tile(st, qaug, kw_ref[pl.ds(base, Q_BLOCK), :], pair_cols(0),
                      vw_ref[pl.ds(base, Q_BLOCK), :], causal_bias, True)
        for w in range(1, n_wt):
            @pl.when(c - n_wt + w >= 0)
            def _(w=w):
                start = pl.multiple_of((c - n_wt + w) * Q_BLOCK, Q_BLOCK)
                _softmax_tile(st, qaug, kw_ref[pl.ds(start, Q_BLOCK), :],
                              pair_cols(2 * (w - n_wt)), vw_ref[pl.ds(start, Q_BLOCK), :],
                              None, False)
        owin_ref[...] = finish()

    _softmax_tile(st, qaug, ks_ref[pl.ds(base, Q_BLOCK), :], pair_cols(0),
                  vs_ref[pl.ds(base, Q_BLOCK), :], causal_bias, True)
    sub = lax.broadcasted_iota(jnp.int32, (16, GROUP_W), 0)

    def slc_tiles(pos0, n_tiles, per_tile):
        slot_rows = jnp.zeros((16, GROUP_W), F32)
        tiles = []
        for ti in range(n_tiles):
            ks_t, vs_t, pcs = [], [], []
            for k in range(per_tile):
                slot = ti * per_tile + k
                pos = pos0 + slot
                live = pos < n_act
                j = jnp.where(live, blist_ref[row * nsp + jnp.minimum(pos, nsp - 1)], 0)
                off = pl.multiple_of(j * SEL_BLOCK, SEL_BLOCK)
                ks_t.append(ks_ref[pl.ds(off, SEL_BLOCK), :])
                vs_t.append(vs_ref[pl.ds(off, SEL_BLOCK), :])
                a_val = ((j - 2 * c) * SEL_BLOCK).astype(F32)
                pcs.append(_pos_cols(a_val, r_val, lane, slot_lane=ROW_SLOT + slot))
                rowk = selb_ref[0, 0, pl.ds(j, 1), :]
                rowk = jnp.where(live, rowk, NEG_INF)
                slot_rows = jnp.where(sub == slot, jnp.tile(rowk, (1, HEADS_PER_GROUP)), slot_rows)
            tiles.append((jnp.concatenate(ks_t, axis=0), jnp.concatenate(pcs, axis=0),
                          jnp.concatenate(vs_t, axis=0)))
        qaug[pl.ds(HEAD_DIM + ROW_SLOT, 16), :] = slot_rows.astype(BF16)
        scores = [_tile_scores(qaug, k_tile, pc) for k_tile, pc, _ in tiles]
        for s, (_, _, v_tile) in zip(scores, tiles):
            _softmax_update(st, s, v_tile, None, False)

    pair = 2 * SLC_GROUP
    rem = n_act & (pair - 1)
    n_pairs = jnp.right_shift(n_act, 4) + jnp.where(rem > SLC_GROUP, 1, 0)
    n_tail = jnp.where(rem > SLC_GROUP, 0, jnp.right_shift(rem + SLC_TAIL - 1, 2))

    def pair_body(i, carry):
        slc_tiles(i * pair, 2, SLC_GROUP)
        return carry

    def tail_body(i, carry):
        slc_tiles(n_pairs * pair + i * SLC_TAIL, 1, SLC_TAIL)
        return carry

    lax.fori_loop(0, n_pairs, pair_body, 0)
    lax.fori_loop(0, n_tail, tail_body, 0)
    o_slcT = finish()

    gT = jax.nn.sigmoid(gate_ref[...].astype(F32).T)
    o_cmpT = ocmpT_ref[0, 0]
    o_winT = owin_ref[...]
    outs = []
    for h in range(HEADS_PER_GROUP):
        hs = slice(h * Q_BLOCK, (h + 1) * Q_BLOCK)
        oT = (gT[3 * h:3 * h + 1, :] * o_cmpT[:, hs] + gT[3 * h + 1:3 * h + 2, :] * o_slcT[:, hs]
              + gT[3 * h + 2:3 * h + 3, :] * o_winT[:, hs])
        outs.append(oT.T)
    o_ref[...] = jnp.concatenate(outs, axis=1).astype(o_ref.dtype)


def _nsa_slc(proj, selb, nblk, blist, ocmpT, alibi, B, S):
    T = B * S
    C = S // Q_BLOCK
    G = N_KV_GROUPS
    nsp = selb.shape[2]
    gw = HEADS_PER_GROUP * HEAD_DIM
    kvspec = lambda col: pl.BlockSpec(
        (S, HEAD_DIM), lambda b, g, c, *_: (b, col // HEAD_DIM + g))
    tspec = lambda rows: pl.BlockSpec((1, 1, rows, GROUP_W), lambda b, g, c, *_: (b * C + c, g, 0, 0))
    grid_spec = pltpu.PrefetchScalarGridSpec(
        num_scalar_prefetch=2,
        grid=(B, G, C),
        in_specs=[pl.BlockSpec((Q_BLOCK, gw), lambda b, g, c, *_: (b * C + c, COL_Q // gw + g)),
                  kvspec(COL_KS), kvspec(COL_VS), kvspec(COL_KW), kvspec(COL_VW),
                  pl.BlockSpec((1, 1, nsp, Q_BLOCK), lambda b, g, c, *_: (b * C + c, g, 0, 0)),
                  pl.BlockSpec((Q_BLOCK, LANES), lambda b, g, c, *_: (b * C + c, COL_GATE // LANES + g)),
                  tspec(HEAD_DIM),
                  pl.BlockSpec((1, 16, GROUP_W), lambda b, g, c, *_: (g, 0, 0)),
                  pl.BlockSpec((WIN_KEYS, LANES), lambda b, g, c, *_: (0, 0))],
        out_specs=pl.BlockSpec((Q_BLOCK, gw), lambda b, g, c, *_: (b * C + c, g)),
        scratch_shapes=[pltpu.VMEM((KAUG, GROUP_W), BF16),
                        pltpu.VMEM((SUBLANES, GROUP_W), F32),
                        pltpu.VMEM((SUBLANES, GROUP_W), F32),
                        pltpu.VMEM((HEAD_DIM, GROUP_W), F32),
                        pltpu.VMEM((HEAD_DIM, GROUP_W), F32)],
    )
    return pl.pallas_call(
        _nsa_slc_kernel,
        out_shape=jax.ShapeDtypeStruct((T, Q_DIM), BF16),
        grid_spec=grid_spec,
        compiler_params=_cparams(("arbitrary", "arbitrary", "arbitrary")),
        name="nsa_slc",
    )(nblk, blist, proj, proj, proj, proj, proj, selb, proj, ocmpT, alibi, _window_pos_cols())


CONV_TS = 512
CONV_HALO = 32
CONV_RC = 64
CONV_CC = 256


def _conv_kernel(a_ref, b_ref, w_ref, cb_ref, g_ref, beta_ref, o_ref, hbuf, ybuf, wrep):
    si = pl.program_id(1)

    @pl.when(si == 0)
    def _():
        hbuf[pl.ds(0, CONV_HALO), :] = jnp.zeros((CONV_HALO, CONV_CH), F32)

    @pl.when(si > 0)
    def _():
        hbuf[pl.ds(0, CONV_HALO), :] = hbuf[pl.ds(CONV_TS, CONV_HALO), :]

    a = a_ref[...].astype(F32)
    bb = b_ref[...].astype(F32)
    hbuf[pl.ds(CONV_HALO, CONV_TS), :] = a * jax.nn.sigmoid(bb)

    for k in range(CONV_WIDTH):
        wrep[pl.ds(k * SUBLANES, SUBLANES), :] = jnp.broadcast_to(w_ref[k:k + 1, :],
                                                                  (SUBLANES, CONV_CH))

    off = CONV_HALO - (CONV_WIDTH - 1)
    groups = CONV_RC // SUBLANES
    for cc in range(CONV_CH // CONV_CC):
        cols = slice(cc * CONV_CC, (cc + 1) * CONV_CC)

        def row_body(r, carry, cols=cols):
            r0 = pl.multiple_of(r * CONV_RC, CONV_RC)
            win = hbuf[pl.ds(r0, CONV_RC + CONV_HALO), cols]
            acc = jnp.zeros((groups, SUBLANES, CONV_CC), F32)
            for res in range(SUBLANES):
                taps = [k for k in range(CONV_WIDTH) if (off + k) % SUBLANES == res]
                span = max(off + k for k in taps) - res + CONV_RC
                shifted = win[res:res + span, :]
                for k in taps:
                    a0 = off + k - res
                    rows = shifted[a0:a0 + CONV_RC, :].reshape(groups, SUBLANES, CONV_CC)
                    acc = acc + rows * wrep[pl.ds(k * SUBLANES, SUBLANES), cols][None]
            ybuf[pl.ds(r0, CONV_RC), cols] = acc.reshape(CONV_RC, CONV_CC)
            return carry

        lax.fori_loop(0, CONV_TS // CONV_RC, row_body, 0)

    y = ybuf[...] + cb_ref[...]
    mu = jnp.mean(y, axis=-1, keepdims=True)
    yc = y - mu
    var = jnp.mean(yc * yc, axis=-1, keepdims=True)
    z = yc * lax.rsqrt(var + LN_EPS) * g_ref[...] + beta_ref[...]
    o_ref[...] = (z * jax.nn.sigmoid(z)).astype(o_ref.dtype)


def _conv(proj, conv_w, conv_b, ln_g, ln_b, B, S):
    T = B * S
    ts = CONV_TS
    assert S % ts == 0
    nS = S // ts
    ca = COL_GLU // CONV_CH
    vec = lambda: pl.BlockSpec((1, CONV_CH), lambda b, s: (0, 0))
    return pl.pallas_call(
        _conv_kernel,
        out_shape=jax.ShapeDtypeStruct((T, CONV_CH), BF16),
        grid=(B, nS),
        in_specs=[pl.BlockSpec((ts, CONV_CH), lambda b, s: (b * nS + s, ca)),
                  pl.BlockSpec((ts, CONV_CH), lambda b, s: (b * nS + s, ca + 1)),
                  pl.BlockSpec((CONV_WIDTH, CONV_CH), lambda b, s: (0, 0)),
                  vec(), vec(), vec()],
        out_specs=pl.BlockSpec((ts, CONV_CH), lambda b, s: (b * nS + s, 0)),
        scratch_shapes=[pltpu.VMEM((CONV_HALO + ts, CONV_CH), F32),
                        pltpu.VMEM((ts, CONV_CH), F32),
                        pltpu.VMEM((CONV_WIDTH * SUBLANES, CONV_CH), F32)],
        compiler_params=_cparams(("arbitrary", "arbitrary")),
        name="conformer_conv",
    )(proj, proj, conv_w, conv_b.reshape(1, -1), ln_g.reshape(1, -1), ln_b.reshape(1, -1))


def _merge_kernel(o_ref, h_ref, wa_ref, wb_ref, ga_ref, gb_ref, out_ref):
    ya = jnp.dot(o_ref[...], wa_ref[...], preferred_element_type=F32)
    yb = jnp.dot(h_ref[...], wb_ref[...], preferred_element_type=F32)
    ga = jax.nn.sigmoid(ga_ref[...].astype(F32))
    gb = jax.nn.sigmoid(gb_ref[...].astype(F32))
    out_ref[...] = (ga * ya + gb * yb).astype(out_ref.dtype)


def _merge(o_nsa, h_conv, wa, wb, proj):
    T = o_nsa.shape[0]
    tm = min(512, T)
    tn = D_MODEL
    nN = D_MODEL // tn
    ga0 = COL_MERGE // tn
    return pl.pallas_call(
        _merge_kernel,
        out_shape=jax.ShapeDtypeStruct((T, D_MODEL), BF16),
        grid=(T // tm, nN),
        in_specs=[pl.BlockSpec((tm, Q_DIM), lambda i, j: (i, 0)),
                  pl.BlockSpec((tm, CONV_CH), lambda i, j: (i, 0)),
                  pl.BlockSpec((Q_DIM, tn), lambda i, j: (0, j)),
                  pl.BlockSpec((CONV_CH, tn), lambda i, j: (0, j)),
                  pl.BlockSpec((tm, tn), lambda i, j: (i, ga0 + j)),
                  pl.BlockSpec((tm, tn), lambda i, j: (i, ga0 + nN + j))],
        out_specs=pl.BlockSpec((tm, tn), lambda i, j: (i, j)),
        compiler_params=_cparams(("arbitrary", "arbitrary")),
        name="merge",
    )(o_nsa, h_conv, wa, wb, proj, proj)


def _layer_norm_rows(y, g, b):
    mu = jnp.mean(y, axis=-1, keepdims=True)
    yc = y - mu
    var = jnp.mean(yc * yc, axis=-1, keepdims=True)
    return yc * lax.rsqrt(var + LN_EPS) * g + b


def _split2(v):
    hi = v.astype(BF16)
    return hi, (v - hi.astype(F32)).astype(BF16)


def _outproj_kernel(mix_ref, w_ref, x_ref, g_ref, b_ref, rw_hi_ref, rw_lo_ref, rb_ref,
                    x1_ref, x1b_ref, ti_ref, tg_ref):
    m = jnp.dot(mix_ref[...], w_ref[...], preferred_element_type=F32)
    x1 = _layer_norm_rows(DEEPNORM_ALPHA * x_ref[...] + m, g_ref[...], b_ref[...])
    x1_ref[...] = x1
    x1b_ref[...] = x1.astype(BF16)
    x_hi, x_lo = _split2(x1)
    logits = (jnp.dot(x_hi, rw_hi_ref[...], preferred_element_type=F32)
              + jnp.dot(x_hi, rw_lo_ref[...], preferred_element_type=F32)
              + jnp.dot(x_lo, rw_hi_ref[...], preferred_element_type=F32)) + rb_ref[...]
    lane = lax.broadcasted_iota(jnp.int32, logits.shape, 1)
    logits = jnp.where(lane < N_EXPERTS, logits, -jnp.inf)
    ti = jnp.zeros(logits.shape, jnp.int32)
    tv = jnp.zeros(logits.shape, F32)
    top0 = None
    den = jnp.zeros((logits.shape[0], 1), F32)
    for k in range(TOP_K):
        mx = jnp.max(logits, axis=-1, keepdims=True)
        idx = jnp.min(jnp.where(logits == mx, lane, LANES), axis=-1, keepdims=True)
        if top0 is None:
            top0 = mx
        e = jnp.exp(mx - top0)
        den = den + e
        ti = jnp.where(lane == k, idx, ti)
        tv = jnp.where(lane == k, e, tv)
        logits = jnp.where(lane == idx, -jnp.inf, logits)
    ti_ref[...] = ti
    tg_ref[...] = tv / den


def _outproj(mix, w_out, x2d, g, b, rw_hi, rw_lo, rb):
    T = mix.shape[0]
    tm = min(256, T)
    full = lambda shape: pl.BlockSpec(shape, lambda i: (0, 0))
    rowb = lambda w: pl.BlockSpec((tm, w), lambda i: (i, 0))
    return pl.pallas_call(
        _outproj_kernel,
        out_shape=(jax.ShapeDtypeStruct((T, D_MODEL), F32),
                   jax.ShapeDtypeStruct((T, D_MODEL), BF16),
                   jax.ShapeDtypeStruct((T, LANES), jnp.int32),
                   jax.ShapeDtypeStruct((T, LANES), F32)),
        grid=(T // tm,),
        in_specs=[rowb(D_MODEL), full((D_MODEL, D_MODEL)), rowb(D_MODEL),
                  full((1, D_MODEL)), full((1, D_MODEL)),
                  full((D_MODEL, LANES)), full((D_MODEL, LANES)), full((1, LANES))],
        out_specs=(rowb(D_MODEL), rowb(D_MODEL), rowb(LANES), rowb(LANES)),
        compiler_params=_cparams(("arbitrary",)),
        name="outproj_ln_router",
    )(mix, w_out, x2d, g, b, rw_hi, rw_lo, rb)


def _moe_up_kernel(be_ref, nused_ref, x_ref, wg_ref, wu_ref, bg_ref, bu_ref, *rest, base):
    h_ref, wgb_ref, wub_ref = rest[-3:]
    local = pl.program_id(1)
    r = base + local
    prev = be_ref[jnp.maximum(r - 1, 0)]
    fresh = (local == 0) | (be_ref[r] != prev)

    @pl.when(fresh)
    def _():
        wgb_ref[...] = wg_ref[0].astype(BF16)
        wub_ref[...] = wu_ref[0].astype(BF16)

    @pl.when(r < nused_ref[0])
    def _():
        x = x_ref[...]
        gt = jnp.dot(x, wgb_ref[...], preferred_element_type=F32) + bg_ref[0]
        up = jnp.dot(x, wub_ref[...], preferred_element_type=F32) + bu_ref[0]
        gt = jnp.minimum(gt, SWIGLU_LIMIT)
        up = jnp.clip(up, -SWIGLU_LIMIT, SWIGLU_LIMIT)
        h = gt * jax.nn.sigmoid(SWIGLU_ALPHA * gt) * (up + 1.0)
        h_ref[...] = h.astype(h_ref.dtype)


def _moe_row_maps(nblk):
    rd = lambda r, nu: jnp.minimum(r, nu[0] - 1)
    wr = lambda r, nu: jnp.where(r < nu[0], r, nblk)
    return rd, wr


def _moe_up(blk_e, n_used, xs_seg, seg, nblk, hbuf, w_gate, w_up, b_gate, b_up):
    bm, tf = MOE_BLOCK, MOE_UP_TILE
    seg_blk = xs_seg.shape[0] // bm
    base = seg * seg_blk
    rd = lambda r, nu: jnp.clip(nu[0] - 1 - base, 0, r)
    wr = lambda r, nu: jnp.where(base + r < nu[0], base + r, nblk)
    wspec = pl.BlockSpec((1, D_MODEL, tf), lambda f, r, be, nu: (be[base + r], 0, f))
    bspec = pl.BlockSpec((1, 1, tf), lambda f, r, be, nu: (be[base + r], 0, f))
    in_specs = [pl.BlockSpec((bm, D_MODEL), lambda f, r, be, nu: (rd(r, nu), 0)),
                wspec, wspec, bspec, bspec]
    args = [blk_e, n_used, xs_seg, w_gate, w_up, b_gate.reshape(N_EXPERTS, 1, D_FF),
            b_up.reshape(N_EXPERTS, 1, D_FF)]
    aliases = {}
    if hbuf is not None:
        in_specs.append(pl.BlockSpec(memory_space=pl.ANY))
        aliases = {len(args): 0}
        args.append(hbuf)
    grid_spec = pltpu.PrefetchScalarGridSpec(
        num_scalar_prefetch=2,
        grid=(D_FF // tf, seg_blk),
        in_specs=in_specs,
        out_specs=pl.BlockSpec((bm, tf), lambda f, r, be, nu: (wr(r, nu), f)),
        scratch_shapes=[pltpu.VMEM((D_MODEL, tf), BF16), pltpu.VMEM((D_MODEL, tf), BF16)],
    )
    return pl.pallas_call(
        functools.partial(_moe_up_kernel, base=base),
        out_shape=jax.ShapeDtypeStruct(((nblk + 1) * bm, D_FF), BF16),
        grid_spec=grid_spec,
        input_output_aliases=aliases,
        compiler_params=_cparams(("arbitrary", "arbitrary")),
        name="moe_up",
    )(*args)


def _moe_down_kernel(be_ref, nused_ref, h_ref, wd_ref, bd_ref, o_ref, wdb_ref):
    r = pl.program_id(0)
    prev = be_ref[jnp.maximum(r - 1, 0)]
    fresh = (r == 0) | (be_ref[r] != prev)

    @pl.when(fresh)
    def _():
        wdb_ref[...] = wd_ref[0].astype(BF16)

    @pl.when(r < nused_ref[0])
    def _():
        y = jnp.dot(h_ref[...], wdb_ref[...], preferred_element_type=F32) + bd_ref[0]
        o_ref[...] = y.astype(o_ref.dtype)


def _moe_down(blk_e, n_used, h, w_down, b_down):
    bm = MOE_BLOCK
    nblk = h.shape[0] // bm - 1
    rd, wr = _moe_row_maps(nblk)
    grid_spec = pltpu.PrefetchScalarGridSpec(
        num_scalar_prefetch=2,
        grid=(nblk,),
        in_specs=[pl.BlockSpec((bm, D_FF), lambda r, be, nu: (rd(r, nu), 0)),
                  pl.BlockSpec((1, D_FF, D_MODEL), lambda r, be, nu: (be[r], 0, 0)),
                  pl.BlockSpec((1, 1, D_MODEL), lambda r, be, nu: (be[r], 0, 0))],
        out_specs=pl.BlockSpec((bm, D_MODEL), lambda r, be, nu: (wr(r, nu), 0)),
        scratch_shapes=[pltpu.VMEM((D_FF, D_MODEL), BF16)],
    )
    return pl.pallas_call(
        _moe_down_kernel,
        out_shape=jax.ShapeDtypeStruct(((nblk + 1) * bm, D_MODEL), BF16),
        grid_spec=grid_spec,
        compiler_params=_cparams(("arbitrary",)),
        name="moe_down",
    )(blk_e, n_used, h, w_down, b_down.reshape(N_EXPERTS, 1, D_MODEL))


def _final_kernel(x1_ref, *refs):
    y_refs, (tg_ref, g_ref, b_ref, o_ref) = refs[:TOP_K], refs[TOP_K:]
    tg = tg_ref[...]
    f = tg[:, 0:1] * y_refs[0][...].astype(F32)
    for k in range(1, TOP_K):
        f = f + tg[:, k:k + 1] * y_refs[k][...].astype(F32)
    o_ref[...] = _layer_norm_rows(DEEPNORM_ALPHA * x1_ref[...] + f, g_ref[...], b_ref[...])


def _final(x1, ys, tg, g, b):
    T = x1.shape[0]
    tm = min(512, T)
    rows = pl.BlockSpec((tm, D_MODEL), lambda i: (i, 0))
    vec = pl.BlockSpec((1, D_MODEL), lambda i: (0, 0))
    return pl.pallas_call(
        _final_kernel,
        out_shape=jax.ShapeDtypeStruct((T, D_MODEL), F32),
        grid=(T // tm,),
        in_specs=[rows] + [rows] * TOP_K + [pl.BlockSpec((tm, LANES), lambda i: (i, 0)), vec, vec],
        out_specs=rows,
        compiler_params=_cparams(("arbitrary",)),
        name="combine_ln",
    )(x1, *ys, tg, g, b)


def _prep_w_in(w_in):
    splits = np.cumsum([Q_DIM] + [KV_DIM] * 6 + [NSA_GATE_DIM, 2 * CONV_CH, 2 * D_MODEL])
    q = w_in[:, :splits[0]] * (LOG2E / math.sqrt(HEAD_DIM))
    kv = w_in[:, splits[0]:splits[6]]
    gate = w_in[:, splits[6]:splits[7]]
    glu = w_in[:, splits[7]:splits[8]]
    merge = w_in[:, splits[8]:splits[9]]
    per_g = HEADS_PER_GROUP * 3
    gate_tiles = [jnp.pad(gate[:, g * per_g:(g + 1) * per_g], ((0, 0), (0, LANES - per_g)))
                  for g in range(N_KV_GROUPS)]
    return jnp.concatenate([q, glu, merge, kv] + gate_tiles, axis=1).astype(BF16)


def _mixer(x2d, B, S, w_in, cmp_pe, cmp_w1, cmp_b1, cmp_w2, cmp_b2, w_nsa_proj,
           conv_w, conv_b, conv_ln_g, conv_ln_b, w_conv_proj):
    T = B * S
    G = N_KV_GROUPS
    proj = _inproj(x2d, _prep_w_in(w_in))

    kcvc = _compress(proj, cmp_pe.reshape(2, 1, CMP_BLOCK * HEAD_DIM),
                     cmp_w1.astype(BF16), cmp_b1.reshape(2, 1, HEAD_DIM),
                     cmp_w2.astype(BF16), cmp_b2.reshape(2, 1, HEAD_DIM), B, S)

    C = S // Q_BLOCK
    alibi = _alibi_rows().astype(BF16)
    ocmpT, selb = _nsa_cmp(proj, kcvc, alibi, B, S)
    nsp = selb.shape[2]
    fl = jnp.max(selb, axis=3) > 0.5 * NEG_INF
    fl = fl.reshape(B, C, G, nsp).transpose(0, 2, 1, 3)
    past = (jnp.arange(nsp, dtype=jnp.int32)[None, :]
            < (Q_BLOCK // SEL_BLOCK) * jnp.arange(C, dtype=jnp.int32)[:, None])
    fl = (fl & past[None, None]).reshape(B * G * C, nsp)
    nblk = jnp.sum(fl, axis=1).astype(jnp.int32)
    blist = jnp.argsort(jnp.logical_not(fl), axis=1, stable=True).astype(jnp.int32)
    o_nsa = _nsa_slc(proj, selb, nblk, blist.reshape(-1), ocmpT, alibi, B, S)

    h_conv = _conv(proj, conv_w, conv_b, conv_ln_g, conv_ln_b, B, S)
    return _merge(o_nsa, h_conv, w_nsa_proj.astype(BF16), w_conv_proj.astype(BF16), proj)


def _moe(x1b, top_i, w_gate, b_gate, w_up, b_up, w_down, b_down):
    T = x1b.shape[0]
    A = T * TOP_K
    bm = MOE_BLOCK
    flat_e = top_i[:, :TOP_K].reshape(A)
    onehot = (flat_e[:, None] == jnp.arange(N_EXPERTS, dtype=jnp.int32)[None, :]).astype(jnp.int32)
    csum = jnp.cumsum(onehot, axis=0)
    rank = jnp.sum(onehot * csum, axis=1) - 1
    counts = csum[-1]
    padded = (counts + bm - 1) // bm * bm
    pad_end = jnp.cumsum(padded)
    pad_start = pad_end - padded
    dest = pad_start[flat_e] + rank
    P = -(-A // bm) * bm + N_EXPERTS * bm
    nblk = P // bm
    flat_tok = jnp.arange(A, dtype=jnp.int32) // TOP_K
    buf_tok = (jnp.arange(P, dtype=jnp.int32) % T).at[dest].set(flat_tok)
    blk_start = jnp.arange(nblk, dtype=jnp.int32) * bm
    blk_e = jnp.minimum(jnp.sum(blk_start[:, None] >= pad_end[None, :], axis=1),
                        N_EXPERTS - 1).astype(jnp.int32)
    n_used = (pad_end[-1] // bm).astype(jnp.int32).reshape(1)
    seg_rows = P // MOE_SEGMENTS
    assert seg_rows % bm == 0
    h = None
    for seg in range(MOE_SEGMENTS):
        xs_seg = x1b[buf_tok[seg * seg_rows:(seg + 1) * seg_rows]]
        h = _moe_up(blk_e, n_used, xs_seg, seg, nblk, h, w_gate, w_up, b_gate, b_up)
    out = _moe_down(blk_e, n_used, h, w_down, b_down)
    dest_k = dest.reshape(T, TOP_K)
    return [out[dest_k[:, k]] for k in range(TOP_K)]


def kernel(x, w_in, cmp_pe, cmp_w1, cmp_b1, cmp_w2, cmp_b2, w_nsa_proj, conv_w, conv_b, conv_ln_g, conv_ln_b, w_conv_proj, w_out, ln1_g, ln1_b, router_w, router_b, w_gate, b_gate, w_up, b_up, w_down, b_down, ln2_g, ln2_b):
    B, S, D = x.shape
    T = B * S
    x2d = x.reshape(T, D)
    for l in range(DEPTH):
        mix = _mixer(x2d, B, S, w_in[l], cmp_pe[l], cmp_w1[l], cmp_b1[l], cmp_w2[l], cmp_b2[l],
                     w_nsa_proj[l], conv_w[l], conv_b[l], conv_ln_g[l], conv_ln_b[l],
                     w_conv_proj[l])
        rw = jnp.pad(router_w[l], ((0, 0), (0, LANES - N_EXPERTS)))
        rw_hi = rw.astype(BF16)
        rw_lo = (rw - rw_hi.astype(F32)).astype(BF16)
        rb = jnp.pad(router_b[l], (0, LANES - N_EXPERTS)).reshape(1, LANES)
        x1, x1b, top_i, top_g = _outproj(
            mix, w_out[l].astype(BF16), x2d, ln1_g[l].reshape(1, D), ln1_b[l].reshape(1, D),
            rw_hi, rw_lo, rb)
        y4 = _moe(x1b, top_i, w_gate[l], b_gate[l], w_up[l], b_up[l], w_down[l], b_down[l])
        x2d = _final(x1, y4, top_g, ln2_g[l].reshape(1, D), ln2_b[l].reshape(1, D))
    return x2d.reshape(B, S, D)
```

```python
import functools
import math

import numpy as np
import jax
import jax.numpy as jnp
from jax import lax
from jax.experimental import pallas as pl
from jax.experimental.pallas import tpu as pltpu

D_MODEL = 2048
N_HEADS = 16
N_KV_GROUPS = 2
HEADS_PER_GROUP = N_HEADS // N_KV_GROUPS
HEAD_DIM = 128
CMP_BLOCK = 32
CMP_STRIDE = 16
SEL_BLOCK = 64
SEL_SHIFT = 6
SEL_TOP_N = 16
WINDOW = 512
Q_BLOCK = 128
N_OVERLAP = (SEL_BLOCK + CMP_BLOCK) // CMP_STRIDE - 1
FORCE_BONUS = 1.0e4
CONV_CH = D_MODEL // 2
CONV_WIDTH = 31
N_EXPERTS = 32
TOP_K = 4
D_FF = D_MODEL
SWIGLU_LIMIT = 7.0
SWIGLU_ALPHA = 1.702
LN_EPS = 1e-5
DEPTH = 1
DEEPNORM_ALPHA = (2 * DEPTH) ** 0.25
NEG_INF = -1e30
TINY = 1e-30

Q_DIM = N_HEADS * HEAD_DIM
KV_DIM = N_KV_GROUPS * HEAD_DIM
NSA_GATE_DIM = N_HEADS * 3

LANES = 128
SUBLANES = 8
VMEM_LIMIT = 56 * 1024 * 1024

COL_Q = 0
COL_GLU = COL_Q + Q_DIM
COL_MERGE = COL_GLU + 2 * CONV_CH
COL_KC = COL_MERGE + 2 * D_MODEL
COL_VC = COL_KC + KV_DIM
COL_KS = COL_VC + KV_DIM
COL_VS = COL_KS + KV_DIM
COL_KW = COL_VS + KV_DIM
COL_VW = COL_KW + KV_DIM
COL_GATE = COL_VW + KV_DIM
PROJ_W = COL_GATE + N_KV_GROUPS * LANES

BF16 = jnp.bfloat16
F32 = jnp.float32

MOE_BLOCK = 256
MOE_UP_TILE = 1024
MOE_SEGMENTS = 4

_ALIBI = np.exp2(-8.0 * np.arange(1, N_HEADS + 1, dtype=np.float32) / N_HEADS).astype(np.float32)
_ALIBI = _ALIBI.reshape(N_KV_GROUPS, HEADS_PER_GROUP)


def _cparams(sem, vmem=VMEM_LIMIT):
    return pltpu.CompilerParams(dimension_semantics=sem, vmem_limit_bytes=vmem)


def _inproj_kernel(x_ref, w_ref, o_ref, xb_ref):
    @pl.when(pl.program_id(1) == 0)
    def _():
        xb_ref[...] = x_ref[...].astype(BF16)

    o_ref[...] = jnp.dot(xb_ref[...], w_ref[...],
                         preferred_element_type=F32).astype(o_ref.dtype)


def _inproj(x2d, w_p):
    T, D = x2d.shape
    N = w_p.shape[1]
    tm = min(512, T)
    tn = N // 3
    assert T % tm == 0 and N % tn == 0 and tn % (2 * LANES) == 0
    return pl.pallas_call(
        _inproj_kernel,
        out_shape=jax.ShapeDtypeStruct((T, N), BF16),
        grid=(T // tm, N // tn),
        in_specs=[pl.BlockSpec((tm, D), lambda i, j: (i, 0)),
                  pl.BlockSpec((D, tn), lambda i, j: (0, j))],
        out_specs=pl.BlockSpec((tm, tn), lambda i, j: (i, j)),
        scratch_shapes=[pltpu.VMEM((tm, D), BF16)],
        compiler_params=_cparams(("arbitrary", "arbitrary")),
        name="inproj",
    )(x2d, w_p)


def _gelu_tanh(x):
    c = math.sqrt(2.0 / math.pi)
    return 0.5 * x * (1.0 + jnp.tanh(c * (x + 0.044715 * (x * x * x))))


def _compress_kernel(x_ref, pe_ref, w1_ref, b1_ref, w2_ref, b2_ref, o_ref, xf_ref):
    S = x_ref.shape[0]
    n16 = S // CMP_STRIDE
    xf_ref[...] = x_ref[...].astype(F32)
    top = jnp.zeros((n16, HEAD_DIM), F32)
    bot = jnp.zeros((n16, HEAD_DIM), F32)
    for j in range(CMP_STRIDE):
        xj = xf_ref[pl.ds(j, n16, stride=CMP_STRIDE), :]
        lo, hi = j * HEAD_DIM, (CMP_STRIDE + j) * HEAD_DIM
        top = top + jnp.dot((xj + pe_ref[0, :, lo:lo + HEAD_DIM]).astype(BF16),
                            w1_ref[0, lo:lo + HEAD_DIM, :], preferred_element_type=F32)
        bot = bot + jnp.dot((xj + pe_ref[0, :, hi:hi + HEAD_DIM]).astype(BF16),
                            w1_ref[0, hi:hi + HEAD_DIM, :], preferred_element_type=F32)
    pre = top + pltpu.roll(bot, n16 - 1, 0) + b1_ref[0]
    h = _gelu_tanh(pre)
    o = jnp.dot(h.astype(BF16), w2_ref[0], preferred_element_type=F32) + b2_ref[0]
    o_ref[0, 0] = o.astype(o_ref.dtype)


def _compress(proj, pe, w1, b1, w2, b2, B, S):
    n_slot = 2 * N_KV_GROUPS
    n16 = S // CMP_STRIDE
    wide = CMP_BLOCK * HEAD_DIM
    kv = lambda s, b: (s // N_KV_GROUPS, 0, 0)
    return pl.pallas_call(
        _compress_kernel,
        out_shape=jax.ShapeDtypeStruct((n_slot, B, n16, HEAD_DIM), BF16),
        grid=(n_slot, B),
        in_specs=[pl.BlockSpec((S, HEAD_DIM), lambda s, b: (b, COL_KC // HEAD_DIM + s)),
                  pl.BlockSpec((1, 1, wide), kv),
                  pl.BlockSpec((1, wide, HEAD_DIM), kv),
                  pl.BlockSpec((1, 1, HEAD_DIM), kv),
                  pl.BlockSpec((1, HEAD_DIM, HEAD_DIM), kv),
                  pl.BlockSpec((1, 1, HEAD_DIM), kv)],
        out_specs=pl.BlockSpec((1, 1, n16, HEAD_DIM), lambda s, b: (s, b, 0, 0)),
        scratch_shapes=[pltpu.VMEM((S, HEAD_DIM), F32)],
        compiler_params=_cparams(("arbitrary", "arbitrary")),
        name="compress",
    )(proj, pe, w1, b1, w2, b2)


LOG2E = 1.4426950408889634
KAUG = 2 * HEAD_DIM
ROW_ALIBI = 0
ROW_SLOT = 16
SLC_GROUP = 8
SLC_TAIL = 4
assert 2 * SLC_GROUP == 16 and SLC_TAIL == 4
GROUP_W = HEADS_PER_GROUP * Q_BLOCK
WIN_TILES = WINDOW // Q_BLOCK
WIN_KEYS = WINDOW + Q_BLOCK


def _window_pos_cols():
    rel = np.arange(WIN_KEYS) - WINDOW
    a = (rel // SEL_BLOCK) * SEL_BLOCK
    r = rel - a
    cols = np.zeros((WIN_KEYS, LANES), np.float32)
    cols[:, ROW_ALIBI:ROW_ALIBI + 3] = a[:, None]
    cols[:, ROW_ALIBI + 3:ROW_ALIBI + 6] = r[:, None]
    return jnp.asarray(cols, BF16)


def _alibi_rows():
    s2 = jnp.asarray(_ALIBI * np.float32(LOG2E), F32)
    hi = s2.astype(BF16)
    r1 = s2 - hi.astype(F32)
    mid = r1.astype(BF16)
    lo = (r1 - mid.astype(F32)).astype(BF16)
    trip = jnp.stack([hi, mid, lo, hi, mid, lo], axis=1)
    rows = jnp.pad(trip, ((0, 0), (0, 16 - 6), (0, 0)))
    return jnp.repeat(rows, Q_BLOCK, axis=2)


def _dot_ta(a, b):
    return lax.dot_general(a, b, (((0,), (0,)), ((), ())), preferred_element_type=F32)


def _fill_qaug(qaug, q_ref, alibi_ref):
    for h in range(HEADS_PER_GROUP):
        qaug[pl.ds(0, HEAD_DIM), pl.ds(h * Q_BLOCK, Q_BLOCK)] = (
            q_ref[:, h * HEAD_DIM:(h + 1) * HEAD_DIM].T)
    qaug[pl.ds(HEAD_DIM, 16), :] = alibi_ref[0]
    qaug[pl.ds(HEAD_DIM + 16, HEAD_DIM - 16), :] = jnp.zeros((HEAD_DIM - 16, GROUP_W), BF16)


def _pos_cols(a_val, r_val, lane, slot_lane=None):
    base = jnp.where((lane >= ROW_ALIBI + 3) & (lane < ROW_ALIBI + 6), r_val, 0.0)
    if slot_lane is not None:
        base = jnp.where(lane == slot_lane, 1.0, base)
    return jnp.where(lane < ROW_ALIBI + 3, a_val, base)


def _nsa_cmp_kernel(q_ref, kc_ref, vc_ref, alibi_ref, ocmpT_ref, selb_ref, qaug, *, n_selb):
    c = pl.program_id(2)
    _fill_qaug(qaug, q_ref, alibi_ref)
    n_vis = c * (Q_BLOCK // CMP_STRIDE) + (Q_BLOCK - CMP_BLOCK) // CMP_STRIDE + 1
    tiles = kc_ref.shape[2] // LANES
    for i in range(1, tiles + 1):
        cond = n_vis > (i - 1) * LANES
        if i < tiles:
            cond = cond & (n_vis <= i * LANES)
        pl.when(cond)(functools.partial(
            _nsa_cmp_rows, c, i * LANES, kc_ref, vc_ref, ocmpT_ref, selb_ref, qaug, n_selb))


def _nsa_cmp_rows(c, ncp, kc_ref, vc_ref, ocmpT_ref, selb_ref, qaug, n_selb):
    nsp = selb_ref.shape[2]

    n_i = lax.broadcasted_iota(jnp.int32, (ncp, LANES), 0)
    lane = lax.broadcasted_iota(jnp.int32, (ncp, LANES), 1)
    end_rel = n_i * CMP_STRIDE + (CMP_BLOCK - 1) - c * Q_BLOCK
    a_val = jnp.left_shift(jnp.right_shift(end_rel, SEL_SHIFT), SEL_SHIFT).astype(F32)
    r_val = (end_rel & (SEL_BLOCK - 1)).astype(F32)
    pc = _pos_cols(a_val, r_val, lane).astype(BF16)
    s = jnp.dot(jnp.concatenate([kc_ref[0, 0, pl.ds(0, ncp), :], pc], axis=1), qaug[...],
                preferred_element_type=F32)

    tq = lax.broadcasted_iota(jnp.int32, (1, Q_BLOCK), 1)
    t_row = c * Q_BLOCK + tq
    n_col = lax.broadcasted_iota(jnp.int32, (ncp, Q_BLOCK), 0)
    mask_bias = jnp.where(n_col * CMP_STRIDE + (CMP_BLOCK - 1) <= t_row, 0.0, NEG_INF)
    has_valid = t_row >= CMP_BLOCK - 1

    imp = jnp.zeros((ncp, Q_BLOCK), F32)
    es, rls = [], []
    for h in range(HEADS_PER_GROUP):
        sh = s[:, h * Q_BLOCK:(h + 1) * Q_BLOCK] + mask_bias
        m = jnp.max(sh, axis=0, keepdims=True)
        e = jnp.exp2(sh - m)
        l = jnp.sum(e, axis=0, keepdims=True)
        rl = jnp.where(has_valid, 1.0 / jnp.maximum(l, TINY), 0.0)
        imp = imp + e * rl
        es.append(e.astype(BF16))
        rls.append(rl)
    oT = _dot_ta(vc_ref[0, 0, pl.ds(0, ncp), :], jnp.concatenate(es, axis=1))
    ocmpT_ref[0, 0] = oT * jnp.concatenate(rls, axis=1)

    j_i = lax.broadcasted_iota(jnp.int32, (nsp, ncp), 0)
    n_j = lax.broadcasted_iota(jnp.int32, (nsp, ncp), 1)
    lo = j_i * (SEL_BLOCK // CMP_STRIDE) - (CMP_BLOCK // CMP_STRIDE) + 1
    ovl = ((n_j >= lo) & (n_j < lo + N_OVERLAP)).astype(BF16)
    i_hi = imp.astype(BF16)
    r1 = imp - i_hi.astype(F32)
    i_mid = r1.astype(BF16)
    i_lo = (r1 - i_mid.astype(F32)).astype(BF16)
    imp_sel = (jnp.dot(ovl, i_hi, preferred_element_type=F32)
               + jnp.dot(ovl, i_mid, preferred_element_type=F32)
               + jnp.dot(ovl, i_lo, preferred_element_type=F32))

    blk = lax.broadcasted_iota(jnp.int32, (nsp, Q_BLOCK), 0)
    cur = jnp.right_shift(t_row, SEL_SHIFT)
    valid = blk * SEL_BLOCK <= t_row
    forced = (blk == 0) | (blk == cur) | (blk == cur - 1)
    score = jnp.where(valid, imp_sel + jnp.where(forced, FORCE_BONUS, 0.0), -1.0)
    score = jnp.where(blk < n_selb, score, -3.0)
    selb = jnp.full((nsp, Q_BLOCK), NEG_INF, F32)
    for _ in range(min(SEL_TOP_N, n_selb)):
        mx = jnp.max(score, axis=0, keepdims=True)
        first = jnp.min(jnp.where(score == mx, blk, nsp), axis=0, keepdims=True)
        pick = blk == first
        selb = jnp.where(pick, 0.0, selb)
        score = jnp.where(pick, -2.0, score)
    selb_ref[0, 0] = selb


def _nsa_cmp(proj, kcvc, alibi, B, S):
    C = S // Q_BLOCK
    G = N_KV_GROUPS
    ncp = kcvc.shape[2]
    n_selb = S // SEL_BLOCK
    nsp = max(LANES, n_selb)
    kern = functools.partial(_nsa_cmp_kernel, n_selb=n_selb)
    return pl.pallas_call(
        kern,
        out_shape=(jax.ShapeDtypeStruct((B * C, G, HEAD_DIM, GROUP_W), F32),
                   jax.ShapeDtypeStruct((B * C, G, nsp, Q_BLOCK), F32)),
        grid=(B, G, C),
        in_specs=[pl.BlockSpec((Q_BLOCK, HEADS_PER_GROUP * HEAD_DIM),
                               lambda b, g, c: (b * C + c, COL_Q // (HEADS_PER_GROUP * HEAD_DIM) + g)),
                  pl.BlockSpec((1, 1, ncp, HEAD_DIM), lambda b, g, c: (g, b, 0, 0)),
                  pl.BlockSpec((1, 1, ncp, HEAD_DIM), lambda b, g, c: (G + g, b, 0, 0)),
                  pl.BlockSpec((1, 16, GROUP_W), lambda b, g, c: (g, 0, 0))],
        out_specs=(pl.BlockSpec((1, 1, HEAD_DIM, GROUP_W), lambda b, g, c: (b * C + c, g, 0, 0)),
                   pl.BlockSpec((1, 1, nsp, Q_BLOCK), lambda b, g, c: (b * C + c, g, 0, 0))),
        scratch_shapes=[pltpu.VMEM((KAUG, GROUP_W), BF16)],
        compiler_params=_cparams(("arbitrary", "arbitrary", "arbitrary")),
        name="nsa_cmp",
    )(proj, kcvc, kcvc, alibi)


def _tile_scores(qaug, k_tile, pc):
    kaug = jnp.concatenate([k_tile, pc.astype(BF16)], axis=1)
    return jnp.dot(kaug, qaug[...], preferred_element_type=F32)


def _softmax_tile(st, qaug, k_tile, pc, v_tile, bias, first):
    _softmax_update(st, _tile_scores(qaug, k_tile, pc), v_tile, bias, first)


def _softmax_update(st, s, v_tile, bias, first):
    m_ref, l_ref, acc_ref = st
    if bias is not None:
        s = jnp.concatenate([s[:, h * Q_BLOCK:(h + 1) * Q_BLOCK] + bias
                             for h in range(HEADS_PER_GROUP)], axis=1)
    mx = jnp.max(s, axis=0, keepdims=True)
    if first:
        m_new = mx
    else:
        m_old = m_ref[pl.ds(0, 1), :]
        m_new = jnp.maximum(m_old, mx)
    p = jnp.exp2(s - m_new)
    ps = jnp.sum(p, axis=0, keepdims=True)
    pv = _dot_ta(v_tile, p.astype(BF16))
    if first:
        l_ref[pl.ds(0, 1), :] = ps
        acc_ref[...] = pv
    else:
        alpha = jnp.exp2(m_old - m_new)
        l_ref[pl.ds(0, 1), :] = alpha * l_ref[pl.ds(0, 1), :] + ps
        acc_ref[...] = alpha * acc_ref[...] + pv
    m_ref[pl.ds(0, 1), :] = m_new


def _bias_rows(s, bias):
    return jnp.concatenate([s[:, h * Q_BLOCK:(h + 1) * Q_BLOCK] + bias
                            for h in range(HEADS_PER_GROUP)], axis=1)


def _nsa_slc_kernel(nblk_ref, blist_ref, q_ref, ks_ref, vs_ref, kw_ref, vw_ref,
                    selb_ref, gate_ref, ocmpT_ref, alibi_ref, wpc_ref, o_ref,
                    qaug, m_ref, l_ref, acc_ref, owin_ref):
    b = pl.program_id(0)
    g = pl.program_id(1)
    c = pl.program_id(2)
    n_chunk = pl.num_programs(2)
    nsp = selb_ref.shape[2]
    row = (b * N_KV_GROUPS + g) * n_chunk + c
    n_act = nblk_ref[row]
    st = (m_ref, l_ref, acc_ref)
    _fill_qaug(qaug, q_ref, alibi_ref)

    lane = lax.broadcasted_iota(jnp.int32, (SEL_BLOCK, LANES), 1)
    r_val = lax.broadcasted_iota(jnp.int32, (SEL_BLOCK, LANES), 0).astype(F32)
    ki = lax.broadcasted_iota(jnp.int32, (Q_BLOCK, Q_BLOCK), 0)
    qi = lax.broadcasted_iota(jnp.int32, (Q_BLOCK, Q_BLOCK), 1)
    causal_bias = jnp.where(ki <= qi, 0.0, NEG_INF)
    tail_bias = jnp.where(ki > qi, 0.0, NEG_INF)

    def pair_cols(first_block_rel):
        return jnp.concatenate(
            [_pos_cols(float(SEL_BLOCK * (first_block_rel + i)), r_val, lane) for i in range(2)], axis=0)

    def finish():
        return acc_ref[...] / jnp.maximum(l_ref[pl.ds(0, 1), :], TINY)

    n_wt = WIN_TILES
    base = pl.multiple_of(c * Q_BLOCK, Q_BLOCK)

    @pl.when(c >= n_wt)
    def _():
        start = pl.multiple_of((c - n_wt) * Q_BLOCK, Q_BLOCK)
        kaug = jnp.concatenate([kw_ref[pl.ds(start, WIN_KEYS), :], wpc_ref[...]], axis=1)
        s = jnp.dot(kaug, qaug[...], preferred_element_type=F32)
        s = jnp.concatenate([_bias_rows(s[:Q_BLOCK], tail_bias), s[Q_BLOCK:WINDOW],
                             _bias_rows(s[WINDOW:], causal_bias)], axis=0)
        p = jnp.exp2(s - jnp.max(s, axis=0, keepdims=True))
        l = jnp.sum(p, axis=0, keepdims=True)
        pv = _dot_ta(vw_ref[pl.ds(start, WIN_KEYS), :], p.astype(BF16))
        owin_ref[...] = pv / jnp.maximum(l, TINY)

    @pl.when(c < n_wt)
    def _():
        _softmax_tile(st, qaug, kw_ref[pl.ds(base, Q_BLOCK), :], pair_cols(0),
                      vw_ref[pl.ds(base, Q_BLOCK), :], causal_bias, True)
        for w in range(1, n_wt):
            @pl.when(c - n_wt + w >= 0)
            def _(w=w):
                start = pl.multiple_of((c - n_wt + w) * Q_BLOCK, Q_BLOCK)
                _softmax_tile(st, qaug, kw_ref[pl.ds(start, Q_BLOCK), :],
                              pair_cols(2 * (w - n_wt)), vw_ref[pl.ds(start, Q_BLOCK), :],
                              None, False)
        owin_ref[...] = finish()

    _softmax_tile(st, qaug, ks_ref[pl.ds(base, Q_BLOCK), :], pair_cols(0),
                  vs_ref[pl.ds(base, Q_BLOCK), :], causal_bias, True)
    sub = lax.broadcasted_iota(jnp.int32, (16, GROUP_W), 0)

    def slc_tiles(pos0, n_tiles, per_tile):
        slot_rows = jnp.zeros((16, GROUP_W), F32)
        tiles = []
        for ti in range(n_tiles):
            ks_t, vs_t, pcs = [], [], []
            for k in range(per_tile):
                slot = ti * per_tile + k
                pos = pos0 + slot
                live = pos < n_act
                j = jnp.where(live, blist_ref[row * nsp + jnp.minimum(pos, nsp - 1)], 0)
                off = pl.multiple_of(j * SEL_BLOCK, SEL_BLOCK)
                ks_t.append(ks_ref[pl.ds(off, SEL_BLOCK), :])
                vs_t.append(vs_ref[pl.ds(off, SEL_BLOCK), :])
                a_val = ((j - 2 * c) * SEL_BLOCK).astype(F32)
                pcs.append(_pos_cols(a_val, r_val, lane, slot_lane=ROW_SLOT + slot))
                rowk = selb_ref[0, 0, pl.ds(j, 1), :]
                rowk = jnp.where(live, rowk, NEG_INF)
                slot_rows = jnp.where(sub == slot, jnp.tile(rowk, (1, HEADS_PER_GROUP)), slot_rows)
            tiles.append((jnp.concatenate(ks_t, axis=0), jnp.concatenate(pcs, axis=0),
                          jnp.concatenate(vs_t, axis=0)))
        qaug[pl.ds(HEAD_DIM + ROW_SLOT, 16), :] = slot_rows.astype(BF16)
        scores = [_tile_scores(qaug, k_tile, pc) for k_tile, pc, _ in tiles]
        for s, (_, _, v_tile) in zip(scores, tiles):
            _softmax_update(st, s, v_tile, None, False)

    pair = 2 * SLC_GROUP
    rem = n_act & (pair - 1)
    n_pairs = jnp.right_shift(n_act, 4) + jnp.where(rem > SLC_GROUP, 1, 0)
    n_tail = jnp.where(rem > SLC_GROUP, 0, jnp.right_shift(rem + SLC_TAIL - 1, 2))

    def pair_body(i, carry):
        slc_tiles(i * pair, 2, SLC_GROUP)
        return carry

    def tail_body(i, carry):
        slc_tiles(n_pairs * pair + i * SLC_TAIL, 1, SLC_TAIL)
        return carry

    lax.fori_loop(0, n_pairs, pair_body, 0)
    lax.fori_loop(0, n_tail, tail_body, 0)
    o_slcT = finish()

    gT = jax.nn.sigmoid(gate_ref[...].astype(F32).T)
    o_cmpT = ocmpT_ref[0, 0]
    o_winT = owin_ref[...]
    outs = []
    for h in range(HEADS_PER_GROUP):
        hs = slice(h * Q_BLOCK, (h + 1) * Q_BLOCK)
        oT = (gT[3 * h:3 * h + 1, :] * o_cmpT[:, hs] + gT[3 * h + 1:3 * h + 2, :] * o_slcT[:, hs]
              + gT[3 * h + 2:3 * h + 3, :] * o_winT[:, hs])
        outs.append(oT.T)
    o_ref[...] = jnp.concatenate(outs, axis=1).astype(o_ref.dtype)


def _nsa_slc(proj, selb, nblk, blist, ocmpT, alibi, B, S):
    T = B * S
    C = S // Q_BLOCK
    G = N_KV_GROUPS
    nsp = selb.shape[2]
    gw = HEADS_PER_GROUP * HEAD_DIM
    kvspec = lambda col: pl.BlockSpec(
        (S, HEAD_DIM), lambda b, g, c, *_: (b, col // HEAD_DIM + g))
    tspec = lambda rows: pl.BlockSpec((1, 1, rows, GROUP_W), lambda b, g, c, *_: (b * C + c, g, 0, 0))
    grid_spec = pltpu.PrefetchScalarGridSpec(
        num_scalar_prefetch=2,
        grid=(B, G, C),
        in_specs=[pl.BlockSpec((Q_BLOCK, gw), lambda b, g, c, *_: (b * C + c, COL_Q // gw + g)),
                  kvspec(COL_KS), kvspec(COL_VS), kvspec(COL_KW), kvspec(COL_VW),
                  pl.BlockSpec((1, 1, nsp, Q_BLOCK), lambda b, g, c, *_: (b * C + c, g, 0, 0)),
                  pl.BlockSpec((Q_BLOCK, LANES), lambda b, g, c, *_: (b * C + c, COL_GATE // LANES + g)),
                  tspec(HEAD_DIM),
                  pl.BlockSpec((1, 16, GROUP_W), lambda b, g, c, *_: (g, 0, 0)),
                  pl.BlockSpec((WIN_KEYS, LANES), lambda b, g, c, *_: (0, 0))],
        out_specs=pl.BlockSpec((Q_BLOCK, gw), lambda b, g, c, *_: (b * C + c, g)),
        scratch_shapes=[pltpu.VMEM((KAUG, GROUP_W), BF16),
                        pltpu.VMEM((SUBLANES, GROUP_W), F32),
                        pltpu.VMEM((SUBLANES, GROUP_W), F32),
                        pltpu.VMEM((HEAD_DIM, GROUP_W), F32),
                        pltpu.VMEM((HEAD_DIM, GROUP_W), F32)],
    )
    return pl.pallas_call(
        _nsa_slc_kernel,
        out_shape=jax.ShapeDtypeStruct((T, Q_DIM), BF16),
        grid_spec=grid_spec,
        compiler_params=_cparams(("arbitrary", "arbitrary", "arbitrary")),
        name="nsa_slc",
    )(nblk, blist, proj, proj, proj, proj, proj, selb, proj, ocmpT, alibi, _window_pos_cols())


CONV_TS = 512
CONV_HALO = 32
CONV_RC = 64
CONV_CC = 256


def _conv_kernel(a_ref, b_ref, w_ref, cb_ref, g_ref, beta_ref, o_ref, hbuf, ybuf, wrep):
    si = pl.program_id(1)

    @pl.when(si == 0)
    def _():
        hbuf[pl.ds(0, CONV_HALO), :] = jnp.zeros((CONV_HALO, CONV_CH), F32)

    @pl.when(si > 0)
    def _():
        hbuf[pl.ds(0, CONV_HALO), :] = hbuf[pl.ds(CONV_TS, CONV_HALO), :]

    a = a_ref[...].astype(F32)
    bb = b_ref[...].astype(F32)
    hbuf[pl.ds(CONV_HALO, CONV_TS), :] = a * jax.nn.sigmoid(bb)

    for k in range(CONV_WIDTH):
        wrep[pl.ds(k * SUBLANES, SUBLANES), :] = jnp.broadcast_to(w_ref[k:k + 1, :],
                                                                  (SUBLANES, CONV_CH))

    off = CONV_HALO - (CONV_WIDTH - 1)
    groups = CONV_RC // SUBLANES
    for cc in range(CONV_CH // CONV_CC):
        cols = slice(cc * CONV_CC, (cc + 1) * CONV_CC)

        def row_body(r, carry, cols=cols):
            r0 = pl.multiple_of(r * CONV_RC, CONV_RC)
            win = hbuf[pl.ds(r0, CONV_RC + CONV_HALO), cols]
            acc = jnp.zeros((groups, SUBLANES, CONV_CC), F32)
            for res in range(SUBLANES):
                taps = [k for k in range(CONV_WIDTH) if (off + k) % SUBLANES == res]
                span = max(off + k for k in taps) - res + CONV_RC
                shifted = win[res:res + span, :]
                for k in taps:
                    a0 = off + k - res
                    rows = shifted[a0:a0 + CONV_RC, :].reshape(groups, SUBLANES, CONV_CC)
                    acc = acc + rows * wrep[pl.ds(k * SUBLANES, SUBLANES), cols][None]
            ybuf[pl.ds(r0, CONV_RC), cols] = acc.reshape(CONV_RC, CONV_CC)
            return carry

        lax.fori_loop(0, CONV_TS // CONV_RC, row_body, 0)

    y = ybuf[...] + cb_ref[...]
    mu = jnp.mean(y, axis=-1, keepdims=True)
    yc = y - mu
    var = jnp.mean(yc * yc, axis=-1, keepdims=True)
    z = yc * lax.rsqrt(var + LN_EPS) * g_ref[...] + beta_ref[...]
    o_ref[...] = (z * jax.nn.sigmoid(z)).astype(o_ref.dtype)


def _conv(proj, conv_w, conv_b, ln_g, ln_b, B, S):
    T = B * S
    ts = CONV_TS
    assert S % ts == 0
    nS = S // ts
    ca = COL_GLU // CONV_CH
    vec = lambda: pl.BlockSpec((1, CONV_CH), lambda b, s: (0, 0))
    return pl.pallas_call(
        _conv_kernel,
        out_shape=jax.ShapeDtypeStruct((T, CONV_CH), BF16),
        grid=(B, nS),
        in_specs=[pl.BlockSpec((ts, CONV_CH), lambda b, s: (b * nS + s, ca)),
                  pl.BlockSpec((ts, CONV_CH), lambda b, s: (b * nS + s, ca + 1)),
                  pl.BlockSpec((CONV_WIDTH, CONV_CH), lambda b, s: (0, 0)),
                  vec(), vec(), vec()],
        out_specs=pl.BlockSpec((ts, CONV_CH), lambda b, s: (b * nS + s, 0)),
        scratch_shapes=[pltpu.VMEM((CONV_HALO + ts, CONV_CH), F32),
                        pltpu.VMEM((ts, CONV_CH), F32),
                        pltpu.VMEM((CONV_WIDTH * SUBLANES, CONV_CH), F32)],
        compiler_params=_cparams(("arbitrary", "arbitrary")),
        name="conformer_conv",
    )(proj, proj, conv_w, conv_b.reshape(1, -1), ln_g.reshape(1, -1), ln_b.reshape(1, -1))


def _merge_kernel(o_ref, h_ref, wa_ref, wb_ref, ga_ref, gb_ref, out_ref):
    ya = jnp.dot(o_ref[...], wa_ref[...], preferred_element_type=F32)
    yb = jnp.dot(h_ref[...], wb_ref[...], preferred_element_type=F32)
    ga = jax.nn.sigmoid(ga_ref[...].astype(F32))
    gb = jax.nn.sigmoid(gb_ref[...].astype(F32))
    out_ref[...] = (ga * ya + gb * yb).astype(out_ref.dtype)


def _merge(o_nsa, h_conv, wa, wb, proj):
    T = o_nsa.shape[0]
    tm = min(512, T)
    tn = D_MODEL
    nN = D_MODEL // tn
    ga0 = COL_MERGE // tn
    return pl.pallas_call(
        _merge_kernel,
        out_shape=jax.ShapeDtypeStruct((T, D_MODEL), BF16),
        grid=(T // tm, nN),
        in_specs=[pl.BlockSpec((tm, Q_DIM), lambda i, j: (i, 0)),
                  pl.BlockSpec((tm, CONV_CH), lambda i, j: (i, 0)),
                  pl.BlockSpec((Q_DIM, tn), lambda i, j: (0, j)),
                  pl.BlockSpec((CONV_CH, tn), lambda i, j: (0, j)),
                  pl.BlockSpec((tm, tn), lambda i, j: (i, ga0 + j)),
                  pl.BlockSpec((tm, tn), lambda i, j: (i, ga0 + nN + j))],
        out_specs=pl.BlockSpec((tm, tn), lambda i, j: (i, j)),
        compiler_params=_cparams(("arbitrary", "arbitrary")),
        name="merge",
    )(o_nsa, h_conv, wa, wb, proj, proj)


def _layer_norm_rows(y, g, b):
    mu = jnp.mean(y, axis=-1, keepdims=True)
    yc = y - mu
    var = jnp.mean(yc * yc, axis=-1, keepdims=True)
    return yc * lax.rsqrt(var + LN_EPS) * g + b


def _split2(v):
    hi = v.astype(BF16)
    return hi, (v - hi.astype(F32)).astype(BF16)


def _outproj_kernel(mix_ref, w_ref, x_ref, g_ref, b_ref, rw_hi_ref, rw_lo_ref, rb_ref,
                    x1_ref, x1b_ref, ti_ref, tg_ref, rk_ref, cnt_ref):
    m = jnp.dot(mix_ref[...], w_ref[...], preferred_element_type=F32)
    x1 = _layer_norm_rows(DEEPNORM_ALPHA * x_ref[...] + m, g_ref[...], b_ref[...])
    x1_ref[...] = x1
    x1b_ref[...] = x1.astype(BF16)
    x_hi, x_lo = _split2(x1)
    logits = (jnp.dot(x_hi, rw_hi_ref[...], preferred_element_type=F32)
              + jnp.dot(x_hi, rw_lo_ref[...], preferred_element_type=F32)
              + jnp.dot(x_lo, rw_hi_ref[...], preferred_element_type=F32)) + rb_ref[...]
    lane = lax.broadcasted_iota(jnp.int32, logits.shape, 1)
    logits = jnp.where(lane < N_EXPERTS, logits, -jnp.inf)
    ti = jnp.zeros(logits.shape, jnp.int32)
    tv = jnp.zeros(logits.shape, F32)
    top0 = None
    den = jnp.zeros((logits.shape[0], 1), F32)
    picks = []
    for k in range(TOP_K):
        mx = jnp.max(logits, axis=-1, keepdims=True)
        idx = jnp.min(jnp.where(logits == mx, lane, LANES), axis=-1, keepdims=True)
        if top0 is None:
            top0 = mx
        e = jnp.exp(mx - top0)
        den = den + e
        ti = jnp.where(lane == k, idx, ti)
        tv = jnp.where(lane == k, e, tv)
        picks.append(lane == idx)
        logits = jnp.where(picks[-1], -jnp.inf, logits)
    ti_ref[...] = ti
    tg_ref[...] = tv / den

    @pl.when(pl.program_id(0) == 0)
    def _():
        cnt_ref[...] = jnp.zeros(cnt_ref.shape, F32)

    tm = logits.shape[0]
    chosen = functools.reduce(jnp.logical_or, picks).astype(BF16)
    tri = (lax.broadcasted_iota(jnp.int32, (tm, tm), 1)
           <= lax.broadcasted_iota(jnp.int32, (tm, tm), 0)).astype(BF16)
    upto = jnp.dot(tri, chosen, preferred_element_type=F32) + cnt_ref[pl.ds(0, 1), :]
    rk = jnp.zeros(logits.shape, jnp.int32)
    for k in range(TOP_K):
        r_k = jnp.sum(jnp.where(picks[k], upto, 0.0), axis=-1, keepdims=True) - 1.0
        rk = jnp.where(lane == k, r_k.astype(jnp.int32), rk)
    rk_ref[...] = rk
    cnt_ref[...] = jnp.broadcast_to(upto[tm - 1:tm, :], cnt_ref.shape)


def _outproj(mix, w_out, x2d, g, b, rw_hi, rw_lo, rb):
    T = mix.shape[0]
    tm = min(256, T)
    full = lambda shape: pl.BlockSpec(shape, lambda i: (0, 0))
    rowb = lambda w: pl.BlockSpec((tm, w), lambda i: (i, 0))
    return pl.pallas_call(
        _outproj_kernel,
        out_shape=(jax.ShapeDtypeStruct((T, D_MODEL), F32),
                   jax.ShapeDtypeStruct((T, D_MODEL), BF16),
                   jax.ShapeDtypeStruct((T, LANES), jnp.int32),
                   jax.ShapeDtypeStruct((T, LANES), F32),
                   jax.ShapeDtypeStruct((T, LANES), jnp.int32),
                   jax.ShapeDtypeStruct((SUBLANES, LANES), F32)),
        grid=(T // tm,),
        in_specs=[rowb(D_MODEL), full((D_MODEL, D_MODEL)), rowb(D_MODEL),
                  full((1, D_MODEL)), full((1, D_MODEL)),
                  full((D_MODEL, LANES)), full((D_MODEL, LANES)), full((1, LANES))],
        out_specs=(rowb(D_MODEL), rowb(D_MODEL), rowb(LANES), rowb(LANES), rowb(LANES),
                   full((SUBLANES, LANES))),
        compiler_params=_cparams(("arbitrary",)),
        name="outproj_ln_router",
    )(mix, w_out, x2d, g, b, rw_hi, rw_lo, rb)


def _moe_up_kernel(be_ref, nused_ref, x_ref, wg_ref, wu_ref, bg_ref, bu_ref, *rest, base):
    h_ref, wgb_ref, wub_ref = rest[-3:]
    local = pl.program_id(1)
    r = base + local
    prev = be_ref[jnp.maximum(r - 1, 0)]
    fresh = (local == 0) | (be_ref[r] != prev)

    @pl.when(fresh)
    def _():
        wgb_ref[...] = wg_ref[0].astype(BF16)
        wub_ref[...] = wu_ref[0].astype(BF16)

    @pl.when(r < nused_ref[0])
    def _():
        x = x_ref[...]
        gt = jnp.dot(x, wgb_ref[...], preferred_element_type=F32) + bg_ref[0]
        up = jnp.dot(x, wub_ref[...], preferred_element_type=F32) + bu_ref[0]
        gt = jnp.minimum(gt, SWIGLU_LIMIT)
        up = jnp.clip(up, -SWIGLU_LIMIT, SWIGLU_LIMIT)
        h = gt * jax.nn.sigmoid(SWIGLU_ALPHA * gt) * (up + 1.0)
        h_ref[...] = h.astype(h_ref.dtype)


def _moe_row_maps(nblk):
    rd = lambda r, nu: jnp.minimum(r, nu[0] - 1)
    wr = lambda r, nu: jnp.where(r < nu[0], r, nblk)
    return rd, wr


def _moe_up(blk_e, n_used, xs_seg, seg, nblk, hbuf, w_gate, w_up, b_gate, b_up):
    bm, tf = MOE_BLOCK, MOE_UP_TILE
    seg_blk = xs_seg.shape[0] // bm
    base = seg * seg_blk
    rd = lambda r, nu: jnp.clip(nu[0] - 1 - base, 0, r)
    wr = lambda r, nu: jnp.where(base + r < nu[0], base + r, nblk)
    wspec = pl.BlockSpec((1, D_MODEL, tf), lambda f, r, be, nu: (be[base + r], 0, f))
    bspec = pl.BlockSpec((1, 1, tf), lambda f, r, be, nu: (be[base + r], 0, f))
    in_specs = [pl.BlockSpec((bm, D_MODEL), lambda f, r, be, nu: (rd(r, nu), 0)),
                wspec, wspec, bspec, bspec]
    args = [blk_e, n_used, xs_seg, w_gate, w_up, b_gate.reshape(N_EXPERTS, 1, D_FF),
            b_up.reshape(N_EXPERTS, 1, D_FF)]
    aliases = {}
    if hbuf is not None:
        in_specs.append(pl.BlockSpec(memory_space=pl.ANY))
        aliases = {len(args): 0}
        args.append(hbuf)
    grid_spec = pltpu.PrefetchScalarGridSpec(
        num_scalar_prefetch=2,
        grid=(D_FF // tf, seg_blk),
        in_specs=in_specs,
        out_specs=pl.BlockSpec((bm, tf), lambda f, r, be, nu: (wr(r, nu), f)),
        scratch_shapes=[pltpu.VMEM((D_MODEL, tf), BF16), pltpu.VMEM((D_MODEL, tf), BF16)],
    )
    return pl.pallas_call(
        functools.partial(_moe_up_kernel, base=base),
        out_shape=jax.ShapeDtypeStruct(((nblk + 1) * bm, D_FF), BF16),
        grid_spec=grid_spec,
        input_output_aliases=aliases,
        compiler_params=_cparams(("arbitrary", "arbitrary")),
        name="moe_up",
    )(*args)


def _moe_down_kernel(be_ref, nused_ref, h_ref, wd_ref, bd_ref, o_ref, wdb_ref):
    r = pl.program_id(0)
    prev = be_ref[jnp.maximum(r - 1, 0)]
    fresh = (r == 0) | (be_ref[r] != prev)

    @pl.when(fresh)
    def _():
        wdb_ref[...] = wd_ref[0].astype(BF16)

    @pl.when(r < nused_ref[0])
    def _():
        y = jnp.dot(h_ref[...], wdb_ref[...], preferred_element_type=F32) + bd_ref[0]
        o_ref[...] = y.astype(o_ref.dtype)


def _moe_down(blk_e, n_used, h, w_down, b_down):
    bm = MOE_BLOCK
    nblk = h.shape[0] // bm - 1
    rd, wr = _moe_row_maps(nblk)
    grid_spec = pltpu.PrefetchScalarGridSpec(
        num_scalar_prefetch=2,
        grid=(nblk,),
        in_specs=[pl.BlockSpec((bm, D_FF), lambda r, be, nu: (rd(r, nu), 0)),
                  pl.BlockSpec((1, D_FF, D_MODEL), lambda r, be, nu: (be[r], 0, 0)),
                  pl.BlockSpec((1, 1, D_MODEL), lambda r, be, nu: (be[r], 0, 0))],
        out_specs=pl.BlockSpec((bm, D_MODEL), lambda r, be, nu: (wr(r, nu), 0)),
        scratch_shapes=[pltpu.VMEM((D_FF, D_MODEL), BF16)],
    )
    return pl.pallas_call(
        _moe_down_kernel,
        out_shape=jax.ShapeDtypeStruct(((nblk + 1) * bm, D_MODEL), BF16),
        grid_spec=grid_spec,
        compiler_params=_cparams(("arbitrary",)),
        name="moe_down",
    )(blk_e, n_used, h, w_down, b_down.reshape(N_EXPERTS, 1, D_MODEL))


def _final_kernel(x1_ref, *refs):
    y_refs, (tg_ref, g_ref, b_ref, o_ref) = refs[:TOP_K], refs[TOP_K:]
    tg = tg_ref[...]
    f = tg[:, 0:1] * y_refs[0][...].astype(F32)
    for k in range(1, TOP_K):
        f = f + tg[:, k:k + 1] * y_refs[k][...].astype(F32)
    o_ref[...] = _layer_norm_rows(DEEPNORM_ALPHA * x1_ref[...] + f, g_ref[...], b_ref[...])


def _final(x1, ys, tg, g, b):
    T = x1.shape[0]
    tm = min(512, T)
    rows = pl.BlockSpec((tm, D_MODEL), lambda i: (i, 0))
    vec = pl.BlockSpec((1, D_MODEL), lambda i: (0, 0))
    return pl.pallas_call(
        _final_kernel,
        out_shape=jax.ShapeDtypeStruct((T, D_MODEL), F32),
        grid=(T // tm,),
        in_specs=[rows] + [rows] * TOP_K + [pl.BlockSpec((tm, LANES), lambda i: (i, 0)), vec, vec],
        out_specs=rows,
        compiler_params=_cparams(("arbitrary",)),
        name="combine_ln",
    )(x1, *ys, tg, g, b)


def _prep_w_in(w_in):
    splits = np.cumsum([Q_DIM] + [KV_DIM] * 6 + [NSA_GATE_DIM, 2 * CONV_CH, 2 * D_MODEL])
    q = w_in[:, :splits[0]] * (LOG2E / math.sqrt(HEAD_DIM))
    kv = w_in[:, splits[0]:splits[6]]
    gate = w_in[:, splits[6]:splits[7]]
    glu = w_in[:, splits[7]:splits[8]]
    merge = w_in[:, splits[8]:splits[9]]
    per_g = HEADS_PER_GROUP * 3
    gate_tiles = [jnp.pad(gate[:, g * per_g:(g + 1) * per_g], ((0, 0), (0, LANES - per_g)))
                  for g in range(N_KV_GROUPS)]
    return jnp.concatenate([q, glu, merge, kv] + gate_tiles, axis=1).astype(BF16)


def _mixer(x2d, B, S, w_in, cmp_pe, cmp_w1, cmp_b1, cmp_w2, cmp_b2, w_nsa_proj,
           conv_w, conv_b, conv_ln_g, conv_ln_b, w_conv_proj):
    T = B * S
    G = N_KV_GROUPS
    proj = _inproj(x2d, _prep_w_in(w_in))

    kcvc = _compress(proj, cmp_pe.reshape(2, 1, CMP_BLOCK * HEAD_DIM),
                     cmp_w1.astype(BF16), cmp_b1.reshape(2, 1, HEAD_DIM),
                     cmp_w2.astype(BF16), cmp_b2.reshape(2, 1, HEAD_DIM), B, S)

    C = S // Q_BLOCK
    alibi = _alibi_rows().astype(BF16)
    ocmpT, selb = _nsa_cmp(proj, kcvc, alibi, B, S)
    nsp = selb.shape[2]
    fl = jnp.max(selb, axis=3) > 0.5 * NEG_INF
    fl = fl.reshape(B, C, G, nsp).transpose(0, 2, 1, 3)
    past = (jnp.arange(nsp, dtype=jnp.int32)[None, :]
            < (Q_BLOCK // SEL_BLOCK) * jnp.arange(C, dtype=jnp.int32)[:, None])
    fl = (fl & past[None, None]).reshape(B * G * C, nsp)
    nblk = jnp.sum(fl, axis=1).astype(jnp.int32)
    blist = jnp.argsort(jnp.logical_not(fl), axis=1, stable=True).astype(jnp.int32)
    o_nsa = _nsa_slc(proj, selb, nblk, blist.reshape(-1), ocmpT, alibi, B, S)

    h_conv = _conv(proj, conv_w, conv_b, conv_ln_g, conv_ln_b, B, S)
    return _merge(o_nsa, h_conv, w_nsa_proj.astype(BF16), w_conv_proj.astype(BF16), proj)


def _moe(x1b, top_i, top_rank, expert_counts, w_gate, b_gate, w_up, b_up, w_down, b_down):
    T = x1b.shape[0]
    A = T * TOP_K
    bm = MOE_BLOCK
    flat_e = top_i[:, :TOP_K].reshape(A)
    rank = top_rank[:, :TOP_K].reshape(A)
    counts = expert_counts[0, :N_EXPERTS].astype(jnp.int32)
    padded = (counts + bm - 1) // bm * bm
    pad_end = jnp.cumsum(padded)
    pad_start = pad_end - padded
    dest = pad_start[flat_e] + rank
    P = -(-A // bm) * bm + N_EXPERTS * bm
    nblk = P // bm
    flat_tok = jnp.arange(A, dtype=jnp.int32) // TOP_K
    buf_tok = (jnp.arange(P, dtype=jnp.int32) % T).at[dest].set(flat_tok)
    blk_start = jnp.arange(nblk, dtype=jnp.int32) * bm
    blk_e = jnp.minimum(jnp.sum(blk_start[:, None] >= pad_end[None, :], axis=1),
                        N_EXPERTS - 1).astype(jnp.int32)
    n_used = (pad_end[-1] // bm).astype(jnp.int32).reshape(1)
    seg_rows = P // MOE_SEGMENTS
    assert seg_rows % bm == 0
    h = None
    for seg in range(MOE_SEGMENTS):
        xs_seg = x1b[buf_tok[seg * seg_rows:(seg + 1) * seg_rows]]
        h = _moe_up(blk_e, n_used, xs_seg, seg, nblk, h, w_gate, w_up, b_gate, b_up)
    out = _moe_down(blk_e, n_used, h, w_down, b_down)
    dest_k = dest.reshape(T, TOP_K)
    return [out[dest_k[:, k]] for k in range(TOP_K)]


def kernel(x, w_in, cmp_pe, cmp_w1, cmp_b1, cmp_w2, cmp_b2, w_nsa_proj, conv_w, conv_b, conv_ln_g, conv_ln_b, w_conv_proj, w_out, ln1_g, ln1_b, router_w, router_b, w_gate, b_gate, w_up, b_up, w_down, b_down, ln2_g, ln2_b):
    B, S, D = x.shape
    T = B * S
    x2d = x.reshape(T, D)
    for l in range(DEPTH):
        mix = _mixer(x2d, B, S, w_in[l], cmp_pe[l], cmp_w1[l], cmp_b1[l], cmp_w2[l], cmp_b2[l],
                     w_nsa_proj[l], conv_w[l], conv_b[l], conv_ln_g[l], conv_ln_b[l],
                     w_conv_proj[l])
        rw = jnp.pad(router_w[l], ((0, 0), (0, LANES - N_EXPERTS)))
        rw_hi = rw.astype(BF16)
        rw_lo = (rw - rw_hi.astype(F32)).astype(BF16)
        rb = jnp.pad(router_b[l], (0, LANES - N_EXPERTS)).reshape(1, LANES)
        x1, x1b, top_i, top_g, top_rank, expert_counts = _outproj(
            mix, w_out[l].astype(BF16), x2d, ln1_g[l].reshape(1, D), ln1_b[l].reshape(1, D),
            rw_hi, rw_lo, rb)
        y4 = _moe(x1b, top_i, top_rank, expert_counts, w_gate[l], b_gate[l], w_up[l], b_up[l],
                  w_down[l], b_down[l])
        x2d = _final(x1, y4, top_g, ln2_g[l].reshape(1, D), ln2_b[l].reshape(1, D))
    return x2d.reshape(B, S, D)
```

```python
import functools
import math

import numpy as np
import jax
import jax.numpy as jnp
from jax import lax
from jax.experimental import pallas as pl
from jax.experimental.pallas import tpu as pltpu

D_MODEL = 2048
N_HEADS = 16
N_KV_GROUPS = 2
HEADS_PER_GROUP = N_HEADS // N_KV_GROUPS
HEAD_DIM = 128
CMP_BLOCK = 32
CMP_STRIDE = 16
SEL_BLOCK = 64
SEL_SHIFT = 6
SEL_TOP_N = 16
WINDOW = 512
Q_BLOCK = 128
N_OVERLAP = (SEL_BLOCK + CMP_BLOCK) // CMP_STRIDE - 1
FORCE_BONUS = 1.0e4
CONV_CH = D_MODEL // 2
CONV_WIDTH = 31
N_EXPERTS = 32
TOP_K = 4
D_FF = D_MODEL
SWIGLU_LIMIT = 7.0
SWIGLU_ALPHA = 1.702
LN_EPS = 1e-5
DEPTH = 1
DEEPNORM_ALPHA = (2 * DEPTH) ** 0.25
NEG_INF = -1e30
TINY = 1e-30

Q_DIM = N_HEADS * HEAD_DIM
KV_DIM = N_KV_GROUPS * HEAD_DIM
NSA_GATE_DIM = N_HEADS * 3

LANES = 128
SUBLANES = 8
VMEM_LIMIT = 56 * 1024 * 1024

COL_Q = 0
COL_GLU = COL_Q + Q_DIM
COL_MERGE = COL_GLU + 2 * CONV_CH
COL_KC = COL_MERGE + 2 * D_MODEL
COL_VC = COL_KC + KV_DIM
COL_KS = COL_VC + KV_DIM
COL_VS = COL_KS + KV_DIM
COL_KW = COL_VS + KV_DIM
COL_VW = COL_KW + KV_DIM
COL_GATE = COL_VW + KV_DIM
PROJ_W = COL_GATE + N_KV_GROUPS * LANES

BF16 = jnp.bfloat16
F32 = jnp.float32

MOE_BLOCK = 256
MOE_UP_TILE = 1024
MOE_SEGMENTS = 4

_ALIBI = np.exp2(-8.0 * np.arange(1, N_HEADS + 1, dtype=np.float32) / N_HEADS).astype(np.float32)
_ALIBI = _ALIBI.reshape(N_KV_GROUPS, HEADS_PER_GROUP)


def _cparams(sem, vmem=VMEM_LIMIT):
    return pltpu.CompilerParams(dimension_semantics=sem, vmem_limit_bytes=vmem)


def _inproj_kernel(x_ref, w_ref, o_ref, xb_ref):
    @pl.when(pl.program_id(1) == 0)
    def _():
        xb_ref[...] = x_ref[...].astype(BF16)

    o_ref[...] = jnp.dot(xb_ref[...], w_ref[...],
                         preferred_element_type=F32).astype(o_ref.dtype)


def _inproj(x2d, w_p):
    T, D = x2d.shape
    N = w_p.shape[1]
    tm = min(512, T)
    tn = N // 3
    assert T % tm == 0 and N % tn == 0 and tn % (2 * LANES) == 0
    return pl.pallas_call(
        _inproj_kernel,
        out_shape=jax.ShapeDtypeStruct((T, N), BF16),
        grid=(T // tm, N // tn),
        in_specs=[pl.BlockSpec((tm, D), lambda i, j: (i, 0)),
                  pl.BlockSpec((D, tn), lambda i, j: (0, j))],
        out_specs=pl.BlockSpec((tm, tn), lambda i, j: (i, j)),
        scratch_shapes=[pltpu.VMEM((tm, D), BF16)],
        compiler_params=_cparams(("arbitrary", "arbitrary")),
        name="inproj",
    )(x2d, w_p)


def _gelu_tanh(x):
    c = math.sqrt(2.0 / math.pi)
    return 0.5 * x * (1.0 + jnp.tanh(c * (x + 0.044715 * (x * x * x))))


def _compress_kernel(x_ref, pe_ref, w1_ref, b1_ref, w2_ref, b2_ref, o_ref, xf_ref):
    S = x_ref.shape[0]
    n16 = S // CMP_STRIDE
    xf_ref[...] = x_ref[...].astype(F32)
    top = jnp.zeros((n16, HEAD_DIM), F32)
    bot = jnp.zeros((n16, HEAD_DIM), F32)
    for j in range(CMP_STRIDE):
        xj = xf_ref[pl.ds(j, n16, stride=CMP_STRIDE), :]
        lo, hi = j * HEAD_DIM, (CMP_STRIDE + j) * HEAD_DIM
        top = top + jnp.dot((xj + pe_ref[0, :, lo:lo + HEAD_DIM]).astype(BF16),
                            w1_ref[0, lo:lo + HEAD_DIM, :], preferred_element_type=F32)
        bot = bot + jnp.dot((xj + pe_ref[0, :, hi:hi + HEAD_DIM]).astype(BF16),
                            w1_ref[0, hi:hi + HEAD_DIM, :], preferred_element_type=F32)
    pre = top + pltpu.roll(bot, n16 - 1, 0) + b1_ref[0]
    h = _gelu_tanh(pre)
    o = jnp.dot(h.astype(BF16), w2_ref[0], preferred_element_type=F32) + b2_ref[0]
    o_ref[0, 0] = o.astype(o_ref.dtype)


def _compress(proj, pe, w1, b1, w2, b2, B, S):
    n_slot = 2 * N_KV_GROUPS
    n16 = S // CMP_STRIDE
    wide = CMP_BLOCK * HEAD_DIM
    kv = lambda s, b: (s // N_KV_GROUPS, 0, 0)
    return pl.pallas_call(
        _compress_kernel,
        out_shape=jax.ShapeDtypeStruct((n_slot, B, n16, HEAD_DIM), BF16),
        grid=(n_slot, B),
        in_specs=[pl.BlockSpec((S, HEAD_DIM), lambda s, b: (b, COL_KC // HEAD_DIM + s)),
                  pl.BlockSpec((1, 1, wide), kv),
                  pl.BlockSpec((1, wide, HEAD_DIM), kv),
                  pl.BlockSpec((1, 1, HEAD_DIM), kv),
                  pl.BlockSpec((1, HEAD_DIM, HEAD_DIM), kv),
                  pl.BlockSpec((1, 1, HEAD_DIM), kv)],
        out_specs=pl.BlockSpec((1, 1, n16, HEAD_DIM), lambda s, b: (s, b, 0, 0)),
        scratch_shapes=[pltpu.VMEM((S, HEAD_DIM), F32)],
        compiler_params=_cparams(("arbitrary", "arbitrary")),
        name="compress",
    )(proj, pe, w1, b1, w2, b2)


LOG2E = 1.4426950408889634
KAUG = 2 * HEAD_DIM
ROW_ALIBI = 0
ROW_SLOT = 16
SLC_GROUP = 8
SLC_TAIL = 4
assert 2 * SLC_GROUP == 16 and SLC_TAIL == 4
GROUP_W = HEADS_PER_GROUP * Q_BLOCK
WIN_TILES = WINDOW // Q_BLOCK
WIN_KEYS = WINDOW + Q_BLOCK


def _window_pos_cols():
    rel = np.arange(WIN_KEYS) - WINDOW
    a = (rel // SEL_BLOCK) * SEL_BLOCK
    r = rel - a
    cols = np.zeros((WIN_KEYS, LANES), np.float32)
    cols[:, ROW_ALIBI:ROW_ALIBI + 3] = a[:, None]
    cols[:, ROW_ALIBI + 3:ROW_ALIBI + 6] = r[:, None]
    return jnp.asarray(cols, BF16)


def _alibi_rows():
    s2 = jnp.asarray(_ALIBI * np.float32(LOG2E), F32)
    hi = s2.astype(BF16)
    r1 = s2 - hi.astype(F32)
    mid = r1.astype(BF16)
    lo = (r1 - mid.astype(F32)).astype(BF16)
    trip = jnp.stack([hi, mid, lo, hi, mid, lo], axis=1)
    rows = jnp.pad(trip, ((0, 0), (0, 16 - 6), (0, 0)))
    return jnp.repeat(rows, Q_BLOCK, axis=2)


def _dot_ta(a, b):
    return lax.dot_general(a, b, (((0,), (0,)), ((), ())), preferred_element_type=F32)


def _fill_qaug(qaug, q_ref, alibi_ref):
    for h in range(HEADS_PER_GROUP):
        qaug[pl.ds(0, HEAD_DIM), pl.ds(h * Q_BLOCK, Q_BLOCK)] = (
            q_ref[:, h * HEAD_DIM:(h + 1) * HEAD_DIM].T)
    qaug[pl.ds(HEAD_DIM, 16), :] = alibi_ref[0]
    qaug[pl.ds(HEAD_DIM + 16, HEAD_DIM - 16), :] = jnp.zeros((HEAD_DIM - 16, GROUP_W), BF16)


def _pos_cols(a_val, r_val, lane, slot_lane=None):
    base = jnp.where((lane >= ROW_ALIBI + 3) & (lane < ROW_ALIBI + 6), r_val, 0.0)
    if slot_lane is not None:
        base = jnp.where(lane == slot_lane, 1.0, base)
    return jnp.where(lane < ROW_ALIBI + 3, a_val, base)


def _nsa_cmp_kernel(q_ref, kc_ref, vc_ref, alibi_ref, ocmpT_ref, selb_ref, flag_ref, qaug, *,
                    n_selb):
    c = pl.program_id(2)
    _fill_qaug(qaug, q_ref, alibi_ref)
    n_vis = c * (Q_BLOCK // CMP_STRIDE) + (Q_BLOCK - CMP_BLOCK) // CMP_STRIDE + 1
    tiles = kc_ref.shape[2] // LANES
    for i in range(1, tiles + 1):
        cond = n_vis > (i - 1) * LANES
        if i < tiles:
            cond = cond & (n_vis <= i * LANES)
        pl.when(cond)(functools.partial(
            _nsa_cmp_rows, c, i * LANES, kc_ref, vc_ref, ocmpT_ref, selb_ref, flag_ref, qaug,
            n_selb))


def _nsa_cmp_rows(c, ncp, kc_ref, vc_ref, ocmpT_ref, selb_ref, flag_ref, qaug, n_selb):
    nsp = selb_ref.shape[2]

    n_i = lax.broadcasted_iota(jnp.int32, (ncp, LANES), 0)
    lane = lax.broadcasted_iota(jnp.int32, (ncp, LANES), 1)
    end_rel = n_i * CMP_STRIDE + (CMP_BLOCK - 1) - c * Q_BLOCK
    a_val = jnp.left_shift(jnp.right_shift(end_rel, SEL_SHIFT), SEL_SHIFT).astype(F32)
    r_val = (end_rel & (SEL_BLOCK - 1)).astype(F32)
    pc = _pos_cols(a_val, r_val, lane).astype(BF16)
    s = jnp.dot(jnp.concatenate([kc_ref[0, 0, pl.ds(0, ncp), :], pc], axis=1), qaug[...],
                preferred_element_type=F32)

    tq = lax.broadcasted_iota(jnp.int32, (1, Q_BLOCK), 1)
    t_row = c * Q_BLOCK + tq
    n_col = lax.broadcasted_iota(jnp.int32, (ncp, Q_BLOCK), 0)
    mask_bias = jnp.where(n_col * CMP_STRIDE + (CMP_BLOCK - 1) <= t_row, 0.0, NEG_INF)
    has_valid = t_row >= CMP_BLOCK - 1

    imp = jnp.zeros((ncp, Q_BLOCK), F32)
    es, rls = [], []
    for h in range(HEADS_PER_GROUP):
        sh = s[:, h * Q_BLOCK:(h + 1) * Q_BLOCK] + mask_bias
        m = jnp.max(sh, axis=0, keepdims=True)
        e = jnp.exp2(sh - m)
        l = jnp.sum(e, axis=0, keepdims=True)
        rl = jnp.where(has_valid, 1.0 / jnp.maximum(l, TINY), 0.0)
        imp = imp + e * rl
        es.append(e.astype(BF16))
        rls.append(rl)
    oT = _dot_ta(vc_ref[0, 0, pl.ds(0, ncp), :], jnp.concatenate(es, axis=1))
    ocmpT_ref[0, 0] = oT * jnp.concatenate(rls, axis=1)

    j_i = lax.broadcasted_iota(jnp.int32, (nsp, ncp), 0)
    n_j = lax.broadcasted_iota(jnp.int32, (nsp, ncp), 1)
    lo = j_i * (SEL_BLOCK // CMP_STRIDE) - (CMP_BLOCK // CMP_STRIDE) + 1
    ovl = ((n_j >= lo) & (n_j < lo + N_OVERLAP)).astype(BF16)
    i_hi = imp.astype(BF16)
    r1 = imp - i_hi.astype(F32)
    i_mid = r1.astype(BF16)
    i_lo = (r1 - i_mid.astype(F32)).astype(BF16)
    imp_sel = (jnp.dot(ovl, i_hi, preferred_element_type=F32)
               + jnp.dot(ovl, i_mid, preferred_element_type=F32)
               + jnp.dot(ovl, i_lo, preferred_element_type=F32))

    blk = lax.broadcasted_iota(jnp.int32, (nsp, Q_BLOCK), 0)
    cur = jnp.right_shift(t_row, SEL_SHIFT)
    valid = blk * SEL_BLOCK <= t_row
    forced = (blk == 0) | (blk == cur) | (blk == cur - 1)
    score = jnp.where(valid, imp_sel + jnp.where(forced, FORCE_BONUS, 0.0), -1.0)
    score = jnp.where(blk < n_selb, score, -3.0)
    selb = jnp.full((nsp, Q_BLOCK), NEG_INF, F32)
    for _ in range(min(SEL_TOP_N, n_selb)):
        mx = jnp.max(score, axis=0, keepdims=True)
        first = jnp.min(jnp.where(score == mx, blk, nsp), axis=0, keepdims=True)
        pick = blk == first
        selb = jnp.where(pick, 0.0, selb)
        score = jnp.where(pick, -2.0, score)
    selb_ref[0, 0] = selb
    flag_ref[0, 0] = jnp.max(selb.T, axis=0, keepdims=True)


def _nsa_cmp(proj, kcvc, alibi, B, S):
    C = S // Q_BLOCK
    G = N_KV_GROUPS
    ncp = kcvc.shape[2]
    n_selb = S // SEL_BLOCK
    nsp = max(LANES, n_selb)
    kern = functools.partial(_nsa_cmp_kernel, n_selb=n_selb)
    return pl.pallas_call(
        kern,
        out_shape=(jax.ShapeDtypeStruct((B * C, G, HEAD_DIM, GROUP_W), F32),
                   jax.ShapeDtypeStruct((B * C, G, nsp, Q_BLOCK), F32),
                   jax.ShapeDtypeStruct((B * C, G, 1, nsp), F32)),
        grid=(B, G, C),
        in_specs=[pl.BlockSpec((Q_BLOCK, HEADS_PER_GROUP * HEAD_DIM),
                               lambda b, g, c: (b * C + c, COL_Q // (HEADS_PER_GROUP * HEAD_DIM) + g)),
                  pl.BlockSpec((1, 1, ncp, HEAD_DIM), lambda b, g, c: (g, b, 0, 0)),
                  pl.BlockSpec((1, 1, ncp, HEAD_DIM), lambda b, g, c: (G + g, b, 0, 0)),
                  pl.BlockSpec((1, 16, GROUP_W), lambda b, g, c: (g, 0, 0))],
        out_specs=(pl.BlockSpec((1, 1, HEAD_DIM, GROUP_W), lambda b, g, c: (b * C + c, g, 0, 0)),
                   pl.BlockSpec((1, 1, nsp, Q_BLOCK), lambda b, g, c: (b * C + c, g, 0, 0)),
                   pl.BlockSpec((1, 1, 1, nsp), lambda b, g, c: (b * C + c, g, 0, 0))),
        scratch_shapes=[pltpu.VMEM((KAUG, GROUP_W), BF16)],
        compiler_params=_cparams(("arbitrary", "arbitrary", "arbitrary")),
        name="nsa_cmp",
    )(proj, kcvc, kcvc, alibi)


def _tile_scores(qaug, k_tile, pc):
    kaug = jnp.concatenate([k_tile, pc.astype(BF16)], axis=1)
    return jnp.dot(kaug, qaug[...], preferred_element_type=F32)


def _softmax_tile(st, qaug, k_tile, pc, v_tile, bias, first):
    _softmax_update(st, _tile_scores(qaug, k_tile, pc), v_tile, bias, first)


def _softmax_update(st, s, v_tile, bias, first):
    m_ref, l_ref, acc_ref = st
    if bias is not None:
        s = jnp.concatenate([s[:, h * Q_BLOCK:(h + 1) * Q_BLOCK] + bias
                             for h in range(HEADS_PER_GROUP)], axis=1)
    mx = jnp.max(s, axis=0, keepdims=True)
    if first:
        m_new = mx
    else:
        m_old = m_ref[pl.ds(0, 1), :]
        m_new = jnp.maximum(m_old, mx)
    p = jnp.exp2(s - m_new)
    ps = jnp.sum(p, axis=0, keepdims=True)
    pv = _dot_ta(v_tile, p.astype(BF16))
    if first:
        l_ref[pl.ds(0, 1), :] = ps
        acc_ref[...] = pv
    else:
        alpha = jnp.exp2(m_old - m_new)
        l_ref[pl.ds(0, 1), :] = alpha * l_ref[pl.ds(0, 1), :] + ps
        acc_ref[...] = alpha * acc_ref[...] + pv
    m_ref[pl.ds(0, 1), :] = m_new


def _bias_rows(s, bias):
    return jnp.concatenate([s[:, h * Q_BLOCK:(h + 1) * Q_BLOCK] + bias
                            for h in range(HEADS_PER_GROUP)], axis=1)


def _nsa_slc_kernel(nblk_ref, blist_ref, q_ref, ks_ref, vs_ref, kw_ref, vw_ref,
                    selb_ref, gate_ref, ocmpT_ref, alibi_ref, wpc_ref, o_ref,
                    qaug, m_ref, l_ref, acc_ref, owin_ref):
    b = pl.program_id(0)
    g = pl.program_id(1)
    c = pl.program_id(2)
    n_chunk = pl.num_programs(2)
    nsp = selb_ref.shape[2]
    row = (b * N_KV_GROUPS + g) * n_chunk + c
    n_act = nblk_ref[row]
    st = (m_ref, l_ref, acc_ref)
    _fill_qaug(qaug, q_ref, alibi_ref)

    lane = lax.broadcasted_iota(jnp.int32, (SEL_BLOCK, LANES), 1)
    r_val = lax.broadcasted_iota(jnp.int32, (SEL_BLOCK, LANES), 0).astype(F32)
    ki = lax.broadcasted_iota(jnp.int32, (Q_BLOCK, Q_BLOCK), 0)
    qi = lax.broadcasted_iota(jnp.int32, (Q_BLOCK, Q_BLOCK), 1)
    causal_bias = jnp.where(ki <= qi, 0.0, NEG_INF)
    tail_bias = jnp.where(ki > qi, 0.0, NEG_INF)

    def pair_cols(first_block_rel):
        return jnp.concatenate(
            [_pos_cols(float(SEL_BLOCK * (first_block_rel + i)), r_val, lane) for i in range(2)], axis=0)

    def finish():
        return acc_ref[...] / jnp.maximum(l_ref[pl.ds(0, 1), :], TINY)

    n_wt = WIN_TILES
    base = pl.multiple_of(c * Q_BLOCK, Q_BLOCK)

    @pl.when(c >= n_wt)
    def _():
        start = pl.multiple_of((c - n_wt) * Q_BLOCK, Q_BLOCK)
        kaug = jnp.concatenate([kw_ref[pl.ds(start, WIN_KEYS), :], wpc_ref[...]], axis=1)
        s = jnp.dot(kaug, qaug[...], preferred_element_type=F32)
        s_diag = _tile_scores(qaug, ks_ref[pl.ds(base, Q_BLOCK), :], pair_cols(0))
        s = jnp.concatenate([_bias_rows(s[:Q_BLOCK], tail_bias), s[Q_BLOCK:WINDOW],
                             _bias_rows(s[WINDOW:], causal_bias)], axis=0)
        p = jnp.exp2(s - jnp.max(s, axis=0, keepdims=True))
        l = jnp.sum(p, axis=0, keepdims=True)
        pv = _dot_ta(vw_ref[pl.ds(start, WIN_KEYS), :], p.astype(BF16))
        owin_ref[...] = pv / jnp.maximum(l, TINY)
        _softmax_update(st, s_diag, vs_ref[pl.ds(base, Q_BLOCK), :], causal_bias, True)

    @pl.when(c < n_wt)
    def _():
        _softmax_tile(st, qaug, kw_ref[pl.ds(base, Q_BLOCK), :], pair_cols(0),
                      vw_ref[pl.ds(base, Q_BLOCK), :], causal_bias, True)
        for w in range(1, n_wt):
            @pl.when(c - n_wt + w >= 0)
            def _(w=w):
                start = pl.multiple_of((c - n_wt + w) * Q_BLOCK, Q_BLOCK)
                _softmax_tile(st, qaug, kw_ref[pl.ds(start, Q_BLOCK), :],
                              pair_cols(2 * (w - n_wt)), vw_ref[pl.ds(start, Q_BLOCK), :],
                              None, False)
        owin_ref[...] = finish()
        _softmax_tile(st, qaug, ks_ref[pl.ds(base, Q_BLOCK), :], pair_cols(0),
                      vs_ref[pl.ds(base, Q_BLOCK), :], causal_bias, True)

    sub = lax.broadcasted_iota(jnp.int32, (16, GROUP_W), 0)

    def slc_tiles(pos0, n_tiles, per_tile):
        slot_rows = jnp.zeros((16, GROUP_W), F32)
        tiles = []
        for ti in range(n_tiles):
            ks_t, vs_t, pcs = [], [], []
            for k in range(per_tile):
                slot = ti * per_tile + k
                pos = pos0 + slot
                live = pos < n_act
                j = jnp.where(live, blist_ref[row * nsp + jnp.minimum(pos, nsp - 1)], 0)
                off = pl.multiple_of(j * SEL_BLOCK, SEL_BLOCK)
                ks_t.append(ks_ref[pl.ds(off, SEL_BLOCK), :])
                vs_t.append(vs_ref[pl.ds(off, SEL_BLOCK), :])
                a_val = ((j - 2 * c) * SEL_BLOCK).astype(F32)
                pcs.append(_pos_cols(a_val, r_val, lane, slot_lane=ROW_SLOT + slot))
                rowk = selb_ref[0, 0, pl.ds(j, 1), :]
                rowk = jnp.where(live, rowk, NEG_INF)
                slot_rows = jnp.where(sub == slot, jnp.tile(rowk, (1, HEADS_PER_GROUP)), slot_rows)
            tiles.append((jnp.concatenate(ks_t, axis=0), jnp.concatenate(pcs, axis=0),
                          jnp.concatenate(vs_t, axis=0)))
        qaug[pl.ds(HEAD_DIM + ROW_SLOT, 16), :] = slot_rows.astype(BF16)
        scores = [_tile_scores(qaug, k_tile, pc) for k_tile, pc, _ in tiles]
        for s, (_, _, v_tile) in zip(scores, tiles):
            _softmax_update(st, s, v_tile, None, False)

    pair = 2 * SLC_GROUP
    rem = n_act & (pair - 1)
    n_pairs = jnp.right_shift(n_act, 4) + jnp.where(rem > SLC_GROUP, 1, 0)
    n_tail = jnp.where(rem > SLC_GROUP, 0, jnp.right_shift(rem + SLC_TAIL - 1, 2))

    def pair_body(i, carry):
        slc_tiles(i * pair, 2, SLC_GROUP)
        return carry

    def tail_body(i, carry):
        slc_tiles(n_pairs * pair + i * SLC_TAIL, 1, SLC_TAIL)
        return carry

    lax.fori_loop(0, n_pairs, pair_body, 0)
    lax.fori_loop(0, n_tail, tail_body, 0)
    o_slcT = finish()

    gT = jax.nn.sigmoid(gate_ref[...].astype(F32).T)
    o_cmpT = ocmpT_ref[0, 0]
    o_winT = owin_ref[...]
    outs = []
    for h in range(HEADS_PER_GROUP):
        hs = slice(h * Q_BLOCK, (h + 1) * Q_BLOCK)
        oT = (gT[3 * h:3 * h + 1, :] * o_cmpT[:, hs] + gT[3 * h + 1:3 * h + 2, :] * o_slcT[:, hs]
              + gT[3 * h + 2:3 * h + 3, :] * o_winT[:, hs])
        outs.append(oT.T)
    o_ref[...] = jnp.concatenate(outs, axis=1).astype(o_ref.dtype)


def _nsa_slc(proj, selb, nblk, blist, ocmpT, alibi, B, S):
    T = B * S
    C = S // Q_BLOCK
    G = N_KV_GROUPS
    nsp = selb.shape[2]
    gw = HEADS_PER_GROUP * HEAD_DIM
    kvspec = lambda col: pl.BlockSpec(
        (S, HEAD_DIM), lambda b, g, c, *_: (b, col // HEAD_DIM + g))
    tspec = lambda rows: pl.BlockSpec((1, 1, rows, GROUP_W), lambda b, g, c, *_: (b * C + c, g, 0, 0))
    grid_spec = pltpu.PrefetchScalarGridSpec(
        num_scalar_prefetch=2,
        grid=(B, G, C),
        in_specs=[pl.BlockSpec((Q_BLOCK, gw), lambda b, g, c, *_: (b * C + c, COL_Q // gw + g)),
                  kvspec(COL_KS), kvspec(COL_VS), kvspec(COL_KW), kvspec(COL_VW),
                  pl.BlockSpec((1, 1, nsp, Q_BLOCK), lambda b, g, c, *_: (b * C + c, g, 0, 0)),
                  pl.BlockSpec((Q_BLOCK, LANES), lambda b, g, c, *_: (b * C + c, COL_GATE // LANES + g)),
                  tspec(HEAD_DIM),
                  pl.BlockSpec((1, 16, GROUP_W), lambda b, g, c, *_: (g, 0, 0)),
                  pl.BlockSpec((WIN_KEYS, LANES), lambda b, g, c, *_: (0, 0))],
        out_specs=pl.BlockSpec((Q_BLOCK, gw), lambda b, g, c, *_: (b * C + c, g)),
        scratch_shapes=[pltpu.VMEM((KAUG, GROUP_W), BF16),
                        pltpu.VMEM((SUBLANES, GROUP_W), F32),
                        pltpu.VMEM((SUBLANES, GROUP_W), F32),
                        pltpu.VMEM((HEAD_DIM, GROUP_W), F32),
                        pltpu.VMEM((HEAD_DIM, GROUP_W), F32)],
    )
    return pl.pallas_call(
        _nsa_slc_kernel,
        out_shape=jax.ShapeDtypeStruct((T, Q_DIM), BF16),
        grid_spec=grid_spec,
        compiler_params=_cparams(("arbitrary", "arbitrary", "arbitrary")),
        name="nsa_slc",
    )(nblk, blist, proj, proj, proj, proj, proj, selb, proj, ocmpT, alibi, _window_pos_cols())


CONV_TS = 512
CONV_HALO = 32
CONV_RC = 64
CONV_CC = 256


def _conv_kernel(a_ref, b_ref, w_ref, cb_ref, g_ref, beta_ref, o_ref, hbuf, ybuf, wrep):
    si = pl.program_id(1)

    @pl.when(si == 0)
    def _():
        hbuf[pl.ds(0, CONV_HALO), :] = jnp.zeros((CONV_HALO, CONV_CH), F32)

    @pl.when(si > 0)
    def _():
        hbuf[pl.ds(0, CONV_HALO), :] = hbuf[pl.ds(CONV_TS, CONV_HALO), :]

    a = a_ref[...].astype(F32)
    bb = b_ref[...].astype(F32)
    hbuf[pl.ds(CONV_HALO, CONV_TS), :] = a * jax.nn.sigmoid(bb)

    for k in range(CONV_WIDTH):
        wrep[pl.ds(k * SUBLANES, SUBLANES), :] = jnp.broadcast_to(w_ref[k:k + 1, :],
                                                                  (SUBLANES, CONV_CH))

    off = CONV_HALO - (CONV_WIDTH - 1)
    groups = CONV_RC // SUBLANES
    for cc in range(CONV_CH // CONV_CC):
        cols = slice(cc * CONV_CC, (cc + 1) * CONV_CC)

        def row_body(r, carry, cols=cols):
            r0 = pl.multiple_of(r * CONV_RC, CONV_RC)
            win = hbuf[pl.ds(r0, CONV_RC + CONV_HALO), cols]
            acc = jnp.zeros((groups, SUBLANES, CONV_CC), F32)
            for res in range(SUBLANES):
                taps = [k for k in range(CONV_WIDTH) if (off + k) % SUBLANES == res]
                span = max(off + k for k in taps) - res + CONV_RC
                shifted = win[res:res + span, :]
                for k in taps:
                    a0 = off + k - res
                    rows = shifted[a0:a0 + CONV_RC, :].reshape(groups, SUBLANES, CONV_CC)
                    acc = acc + rows * wrep[pl.ds(k * SUBLANES, SUBLANES), cols][None]
            ybuf[pl.ds(r0, CONV_RC), cols] = acc.reshape(CONV_RC, CONV_CC)
            return carry

        lax.fori_loop(0, CONV_TS // CONV_RC, row_body, 0)

    y = ybuf[...] + cb_ref[...]
    mu = jnp.mean(y, axis=-1, keepdims=True)
    yc = y - mu
    var = jnp.mean(yc * yc, axis=-1, keepdims=True)
    z = yc * lax.rsqrt(var + LN_EPS) * g_ref[...] + beta_ref[...]
    o_ref[...] = (z * jax.nn.sigmoid(z)).astype(o_ref.dtype)


def _conv(proj, conv_w, conv_b, ln_g, ln_b, B, S):
    T = B * S
    ts = CONV_TS
    assert S % ts == 0
    nS = S // ts
    ca = COL_GLU // CONV_CH
    vec = lambda: pl.BlockSpec((1, CONV_CH), lambda b, s: (0, 0))
    return pl.pallas_call(
        _conv_kernel,
        out_shape=jax.ShapeDtypeStruct((T, CONV_CH), BF16),
        grid=(B, nS),
        in_specs=[pl.BlockSpec((ts, CONV_CH), lambda b, s: (b * nS + s, ca)),
                  pl.BlockSpec((ts, CONV_CH), lambda b, s: (b * nS + s, ca + 1)),
                  pl.BlockSpec((CONV_WIDTH, CONV_CH), lambda b, s: (0, 0)),
                  vec(), vec(), vec()],
        out_specs=pl.BlockSpec((ts, CONV_CH), lambda b, s: (b * nS + s, 0)),
        scratch_shapes=[pltpu.VMEM((CONV_HALO + ts, CONV_CH), F32),
                        pltpu.VMEM((ts, CONV_CH), F32),
                        pltpu.VMEM((CONV_WIDTH * SUBLANES, CONV_CH), F32)],
        compiler_params=_cparams(("arbitrary", "arbitrary")),
        name="conformer_conv",
    )(proj, proj, conv_w, conv_b.reshape(1, -1), ln_g.reshape(1, -1), ln_b.reshape(1, -1))


def _merge_kernel(o_ref, h_ref, wa_ref, wb_ref, ga_ref, gb_ref, out_ref):
    ya = jnp.dot(o_ref[...], wa_ref[...], preferred_element_type=F32)
    yb = jnp.dot(h_ref[...], wb_ref[...], preferred_element_type=F32)
    ga = jax.nn.sigmoid(ga_ref[...].astype(F32))
    gb = jax.nn.sigmoid(gb_ref[...].astype(F32))
    out_ref[...] = (ga * ya + gb * yb).astype(out_ref.dtype)


def _merge(o_nsa, h_conv, wa, wb, proj):
    T = o_nsa.shape[0]
    tm = min(512, T)
    tn = D_MODEL
    nN = D_MODEL // tn
    ga0 = COL_MERGE // tn
    return pl.pallas_call(
        _merge_kernel,
        out_shape=jax.ShapeDtypeStruct((T, D_MODEL), BF16),
        grid=(T // tm, nN),
        in_specs=[pl.BlockSpec((tm, Q_DIM), lambda i, j: (i, 0)),
                  pl.BlockSpec((tm, CONV_CH), lambda i, j: (i, 0)),
                  pl.BlockSpec((Q_DIM, tn), lambda i, j: (0, j)),
                  pl.BlockSpec((CONV_CH, tn), lambda i, j: (0, j)),
                  pl.BlockSpec((tm, tn), lambda i, j: (i, ga0 + j)),
                  pl.BlockSpec((tm, tn), lambda i, j: (i, ga0 + nN + j))],
        out_specs=pl.BlockSpec((tm, tn), lambda i, j: (i, j)),
        compiler_params=_cparams(("arbitrary", "arbitrary")),
        name="merge",
    )(o_nsa, h_conv, wa, wb, proj, proj)


def _layer_norm_rows(y, g, b):
    mu = jnp.mean(y, axis=-1, keepdims=True)
    yc = y - mu
    var = jnp.mean(yc * yc, axis=-1, keepdims=True)
    return yc * lax.rsqrt(var + LN_EPS) * g + b


def _split2(v):
    hi = v.astype(BF16)
    return hi, (v - hi.astype(F32)).astype(BF16)


def _outproj_kernel(mix_ref, w_ref, x_ref, g_ref, b_ref, rw_hi_ref, rw_lo_ref, rb_ref,
                    x1_ref, x1b_ref, ti_ref, tg_ref, rk_ref, cnt_ref):
    m = jnp.dot(mix_ref[...], w_ref[...], preferred_element_type=F32)
    x1 = _layer_norm_rows(DEEPNORM_ALPHA * x_ref[...] + m, g_ref[...], b_ref[...])
    x1_ref[...] = x1
    x1b_ref[...] = x1.astype(BF16)
    x_hi, x_lo = _split2(x1)
    logits = (jnp.dot(x_hi, rw_hi_ref[...], preferred_element_type=F32)
              + jnp.dot(x_hi, rw_lo_ref[...], preferred_element_type=F32)
              + jnp.dot(x_lo, rw_hi_ref[...], preferred_element_type=F32)) + rb_ref[...]
    lane = lax.broadcasted_iota(jnp.int32, logits.shape, 1)
    logits = jnp.where(lane < N_EXPERTS, logits, -jnp.inf)
    ti = jnp.zeros(logits.shape, jnp.int32)
    tv = jnp.zeros(logits.shape, F32)
    top0 = None
    den = jnp.zeros((logits.shape[0], 1), F32)
    picks = []
    for k in range(TOP_K):
        mx = jnp.max(logits, axis=-1, keepdims=True)
        idx = jnp.min(jnp.where(logits == mx, lane, LANES), axis=-1, keepdims=True)
        if top0 is None:
            top0 = mx
        e = jnp.exp(mx - top0)
        den = den + e
        ti = jnp.where(lane == k, idx, ti)
        tv = jnp.where(lane == k, e, tv)
        picks.append(lane == idx)
        logits = jnp.where(picks[-1], -jnp.inf, logits)
    ti_ref[...] = ti
    tg_ref[...] = tv / den

    @pl.when(pl.program_id(0) == 0)
    def _():
        cnt_ref[...] = jnp.zeros(cnt_ref.shape, F32)

    tm = logits.shape[0]
    chosen = functools.reduce(jnp.logical_or, picks).astype(BF16)
    tri = (lax.broadcasted_iota(jnp.int32, (tm, tm), 1)
           <= lax.broadcasted_iota(jnp.int32, (tm, tm), 0)).astype(BF16)
    upto = jnp.dot(tri, chosen, preferred_element_type=F32) + cnt_ref[pl.ds(0, 1), :]
    rk = jnp.zeros(logits.shape, jnp.int32)
    for k in range(TOP_K):
        r_k = jnp.sum(jnp.where(picks[k], upto, 0.0), axis=-1, keepdims=True) - 1.0
        rk = jnp.where(lane == k, r_k.astype(jnp.int32), rk)
    rk_ref[...] = rk
    cnt_ref[...] = jnp.broadcast_to(upto[tm - 1:tm, :], cnt_ref.shape)


def _outproj(mix, w_out, x2d, g, b, rw_hi, rw_lo, rb):
    T = mix.shape[0]
    tm = min(256, T)
    full = lambda shape: pl.BlockSpec(shape, lambda i: (0, 0))
    rowb = lambda w: pl.BlockSpec((tm, w), lambda i: (i, 0))
    return pl.pallas_call(
        _outproj_kernel,
        out_shape=(jax.ShapeDtypeStruct((T, D_MODEL), F32),
                   jax.ShapeDtypeStruct((T, D_MODEL), BF16),
                   jax.ShapeDtypeStruct((T, LANES), jnp.int32),
                   jax.ShapeDtypeStruct((T, LANES), F32),
                   jax.ShapeDtypeStruct((T, LANES), jnp.int32),
                   jax.ShapeDtypeStruct((SUBLANES, LANES), F32)),
        grid=(T // tm,),
        in_specs=[rowb(D_MODEL), full((D_MODEL, D_MODEL)), rowb(D_MODEL),
                  full((1, D_MODEL)), full((1, D_MODEL)),
                  full((D_MODEL, LANES)), full((D_MODEL, LANES)), full((1, LANES))],
        out_specs=(rowb(D_MODEL), rowb(D_MODEL), rowb(LANES), rowb(LANES), rowb(LANES),
                   full((SUBLANES, LANES))),
        compiler_params=_cparams(("arbitrary",)),
        name="outproj_ln_router",
    )(mix, w_out, x2d, g, b, rw_hi, rw_lo, rb)


def _moe_up_kernel(be_ref, nused_ref, x_ref, wg_ref, wu_ref, bg_ref, bu_ref, *rest, base):
    h_ref, wgb_ref, wub_ref = rest[-3:]
    local = pl.program_id(1)
    r = base + local
    prev = be_ref[jnp.maximum(r - 1, 0)]
    fresh = (local == 0) | (be_ref[r] != prev)

    @pl.when(fresh)
    def _():
        wgb_ref[...] = wg_ref[0].astype(BF16)
        wub_ref[...] = wu_ref[0].astype(BF16)

    @pl.when(r < nused_ref[0])
    def _():
        x = x_ref[...]
        gt = jnp.dot(x, wgb_ref[...], preferred_element_type=F32) + bg_ref[0]
        up = jnp.dot(x, wub_ref[...], preferred_element_type=F32) + bu_ref[0]
        gt = jnp.minimum(gt, SWIGLU_LIMIT)
        up = jnp.clip(up, -SWIGLU_LIMIT, SWIGLU_LIMIT)
        h = gt * jax.nn.sigmoid(SWIGLU_ALPHA * gt) * (up + 1.0)
        h_ref[...] = h.astype(h_ref.dtype)


def _moe_row_maps(nblk):
    rd = lambda r, nu: jnp.minimum(r, nu[0] - 1)
    wr = lambda r, nu: jnp.where(r < nu[0], r, nblk)
    return rd, wr


def _moe_up(blk_e, n_used, xs_seg, seg, nblk, hbuf, w_gate, w_up, b_gate, b_up):
    bm, tf = MOE_BLOCK, MOE_UP_TILE
    seg_blk = xs_seg.shape[0] // bm
    base = seg * seg_blk
    rd = lambda r, nu: jnp.clip(nu[0] - 1 - base, 0, r)
    wr = lambda r, nu: jnp.where(base + r < nu[0], base + r, nblk)
    wspec = pl.BlockSpec((1, D_MODEL, tf), lambda f, r, be, nu: (be[base + r], 0, f))
    bspec = pl.BlockSpec((1, 1, tf), lambda f, r, be, nu: (be[base + r], 0, f))
    in_specs = [pl.BlockSpec((bm, D_MODEL), lambda f, r, be, nu: (rd(r, nu), 0)),
                wspec, wspec, bspec, bspec]
    args = [blk_e, n_used, xs_seg, w_gate, w_up, b_gate.reshape(N_EXPERTS, 1, D_FF),
            b_up.reshape(N_EXPERTS, 1, D_FF)]
    aliases = {}
    if hbuf is not None:
        in_specs.append(pl.BlockSpec(memory_space=pl.ANY))
        aliases = {len(args): 0}
        args.append(hbuf)
    grid_spec = pltpu.PrefetchScalarGridSpec(
        num_scalar_prefetch=2,
        grid=(D_FF // tf, seg_blk),
        in_specs=in_specs,
        out_specs=pl.BlockSpec((bm, tf), lambda f, r, be, nu: (wr(r, nu), f)),
        scratch_shapes=[pltpu.VMEM((D_MODEL, tf), BF16), pltpu.VMEM((D_MODEL, tf), BF16)],
    )
    return pl.pallas_call(
        functools.partial(_moe_up_kernel, base=base),
        out_shape=jax.ShapeDtypeStruct(((nblk + 1) * bm, D_FF), BF16),
        grid_spec=grid_spec,
        input_output_aliases=aliases,
        compiler_params=_cparams(("arbitrary", "arbitrary")),
        name="moe_up",
    )(*args)


def _moe_down_kernel(be_ref, nused_ref, h_ref, wd_ref, bd_ref, o_ref, wdb_ref):
    r = pl.program_id(0)
    prev = be_ref[jnp.maximum(r - 1, 0)]
    fresh = (r == 0) | (be_ref[r] != prev)

    @pl.when(fresh)
    def _():
        wdb_ref[...] = wd_ref[0].astype(BF16)

    @pl.when(r < nused_ref[0])
    def _():
        y = jnp.dot(h_ref[...], wdb_ref[...], preferred_element_type=F32) + bd_ref[0]
        o_ref[...] = y.astype(o_ref.dtype)


def _moe_down(blk_e, n_used, h, w_down, b_down):
    bm = MOE_BLOCK
    nblk = h.shape[0] // bm - 1
    rd, wr = _moe_row_maps(nblk)
    grid_spec = pltpu.PrefetchScalarGridSpec(
        num_scalar_prefetch=2,
        grid=(nblk,),
        in_specs=[pl.BlockSpec((bm, D_FF), lambda r, be, nu: (rd(r, nu), 0)),
                  pl.BlockSpec((1, D_FF, D_MODEL), lambda r, be, nu: (be[r], 0, 0)),
                  pl.BlockSpec((1, 1, D_MODEL), lambda r, be, nu: (be[r], 0, 0))],
        out_specs=pl.BlockSpec((bm, D_MODEL), lambda r, be, nu: (wr(r, nu), 0)),
        scratch_shapes=[pltpu.VMEM((D_FF, D_MODEL), BF16)],
    )
    return pl.pallas_call(
        _moe_down_kernel,
        out_shape=jax.ShapeDtypeStruct(((nblk + 1) * bm, D_MODEL), BF16),
        grid_spec=grid_spec,
        compiler_params=_cparams(("arbitrary",)),
        name="moe_down",
    )(blk_e, n_used, h, w_down, b_down.reshape(N_EXPERTS, 1, D_MODEL))


def _final_kernel(x1_ref, *refs):
    y_refs, (tg_ref, g_ref, b_ref, o_ref) = refs[:TOP_K], refs[TOP_K:]
    tg = tg_ref[...]
    f = tg[:, 0:1] * y_refs[0][...].astype(F32)
    for k in range(1, TOP_K):
        f = f + tg[:, k:k + 1] * y_refs[k][...].astype(F32)
    o_ref[...] = _layer_norm_rows(DEEPNORM_ALPHA * x1_ref[...] + f, g_ref[...], b_ref[...])


def _final(x1, ys, tg, g, b):
    T = x1.shape[0]
    tm = min(512, T)
    rows = pl.BlockSpec((tm, D_MODEL), lambda i: (i, 0))
    vec = pl.BlockSpec((1, D_MODEL), lambda i: (0, 0))
    return pl.pallas_call(
        _final_kernel,
        out_shape=jax.ShapeDtypeStruct((T, D_MODEL), F32),
        grid=(T // tm,),
        in_specs=[rows] + [rows] * TOP_K + [pl.BlockSpec((tm, LANES), lambda i: (i, 0)), vec, vec],
        out_specs=rows,
        compiler_params=_cparams(("arbitrary",)),
        name="combine_ln",
    )(x1, *ys, tg, g, b)


def _prep_w_in(w_in):
    splits = np.cumsum([Q_DIM] + [KV_DIM] * 6 + [NSA_GATE_DIM, 2 * CONV_CH, 2 * D_MODEL])
    q = w_in[:, :splits[0]] * (LOG2E / math.sqrt(HEAD_DIM))
    kv = w_in[:, splits[0]:splits[6]]
    gate = w_in[:, splits[6]:splits[7]]
    glu = w_in[:, splits[7]:splits[8]]
    merge = w_in[:, splits[8]:splits[9]]
    per_g = HEADS_PER_GROUP * 3
    gate_tiles = [jnp.pad(gate[:, g * per_g:(g + 1) * per_g], ((0, 0), (0, LANES - per_g)))
                  for g in range(N_KV_GROUPS)]
    return jnp.concatenate([q, glu, merge, kv] + gate_tiles, axis=1).astype(BF16)


def _mixer(x2d, B, S, w_in, cmp_pe, cmp_w1, cmp_b1, cmp_w2, cmp_b2, w_nsa_proj,
           conv_w, conv_b, conv_ln_g, conv_ln_b, w_conv_proj):
    T = B * S
    G = N_KV_GROUPS
    proj = _inproj(x2d, _prep_w_in(w_in))

    kcvc = _compress(proj, cmp_pe.reshape(2, 1, CMP_BLOCK * HEAD_DIM),
                     cmp_w1.astype(BF16), cmp_b1.reshape(2, 1, HEAD_DIM),
                     cmp_w2.astype(BF16), cmp_b2.reshape(2, 1, HEAD_DIM), B, S)

    C = S // Q_BLOCK
    alibi = _alibi_rows().astype(BF16)
    ocmpT, selb, flags = _nsa_cmp(proj, kcvc, alibi, B, S)
    nsp = selb.shape[2]
    fl = flags[:, :, 0, :] > 0.5 * NEG_INF
    fl = fl.reshape(B, C, G, nsp).transpose(0, 2, 1, 3)
    past = (jnp.arange(nsp, dtype=jnp.int32)[None, :]
            < (Q_BLOCK // SEL_BLOCK) * jnp.arange(C, dtype=jnp.int32)[:, None])
    fl = (fl & past[None, None]).reshape(B * G * C, nsp)
    nblk = jnp.sum(fl, axis=1).astype(jnp.int32)
    blist = jnp.argsort(jnp.logical_not(fl), axis=1, stable=True).astype(jnp.int32)
    o_nsa = _nsa_slc(proj, selb, nblk, blist.reshape(-1), ocmpT, alibi, B, S)

    h_conv = _conv(proj, conv_w, conv_b, conv_ln_g, conv_ln_b, B, S)
    return _merge(o_nsa, h_conv, w_nsa_proj.astype(BF16), w_conv_proj.astype(BF16), proj)


def _moe(x1b, top_i, top_rank, expert_counts, w_gate, b_gate, w_up, b_up, w_down, b_down):
    T = x1b.shape[0]
    A = T * TOP_K
    bm = MOE_BLOCK
    flat_e = top_i[:, :TOP_K].reshape(A)
    rank = top_rank[:, :TOP_K].reshape(A)
    counts = expert_counts[0, :N_EXPERTS].astype(jnp.int32)
    padded = (counts + bm - 1) // bm * bm
    pad_end = jnp.cumsum(padded)
    pad_start = pad_end - padded
    dest = pad_start[flat_e] + rank
    P = -(-A // bm) * bm + N_EXPERTS * bm
    nblk = P // bm
    flat_tok = jnp.arange(A, dtype=jnp.int32) // TOP_K
    buf_tok = (jnp.arange(P, dtype=jnp.int32) % T).at[dest].set(flat_tok)
    blk_start = jnp.arange(nblk, dtype=jnp.int32) * bm
    blk_e = jnp.minimum(jnp.sum(blk_start[:, None] >= pad_end[None, :], axis=1),
                        N_EXPERTS - 1).astype(jnp.int32)
    n_used = (pad_end[-1] // bm).astype(jnp.int32).reshape(1)
    seg_rows = P // MOE_SEGMENTS
    assert seg_rows % bm == 0
    h = None
    for seg in range(MOE_SEGMENTS):
        xs_seg = x1b[buf_tok[seg * seg_rows:(seg + 1) * seg_rows]]
        h = _moe_up(blk_e, n_used, xs_seg, seg, nblk, h, w_gate, w_up, b_gate, b_up)
    out = _moe_down(blk_e, n_used, h, w_down, b_down)
    dest_k = dest.reshape(T, TOP_K)
    return [out[dest_k[:, k]] for k in range(TOP_K)]


def kernel(x, w_in, cmp_pe, cmp_w1, cmp_b1, cmp_w2, cmp_b2, w_nsa_proj, conv_w, conv_b, conv_ln_g, conv_ln_b, w_conv_proj, w_out, ln1_g, ln1_b, router_w, router_b, w_gate, b_gate, w_up, b_up, w_down, b_down, ln2_g, ln2_b):
    B, S, D = x.shape
    T = B * S
    x2d = x.reshape(T, D)
    for l in range(DEPTH):
        mix = _mixer(x2d, B, S, w_in[l], cmp_pe[l], cmp_w1[l], cmp_b1[l], cmp_w2[l], cmp_b2[l],
                     w_nsa_proj[l], conv_w[l], conv_b[l], conv_ln_g[l], conv_ln_b[l],
                     w_conv_proj[l])
        rw = jnp.pad(router_w[l], ((0, 0), (0, LANES - N_EXPERTS)))
        rw_hi = rw.astype(BF16)
        rw_lo = (rw - rw_hi.astype(F32)).astype(BF16)
        rb = jnp.pad(router_b[l], (0, LANES - N_EXPERTS)).reshape(1, LANES)
        x1, x1b, top_i, top_g, top_rank, expert_counts = _outproj(
            mix, w_out[l].astype(BF16), x2d, ln1_g[l].reshape(1, D), ln1_b[l].reshape(1, D),
            rw_hi, rw_lo, rb)
        y4 = _moe(x1b, top_i, top_rank, expert_counts, w_gate[l], b_gate[l], w_up[l], b_up[l],
                  w_down[l], b_down[l])
        x2d = _final(x1, y4, top_g, ln2_g[l].reshape(1, D), ln2_b[l].reshape(1, D))
    return x2d.reshape(B, S, D)
```

```python
import functools
import math

import numpy as np
import jax
import jax.numpy as jnp
from jax import lax
from jax.experimental import pallas as pl
from jax.experimental.pallas import tpu as pltpu

D_MODEL = 2048
N_HEADS = 16
N_KV_GROUPS = 2
HEADS_PER_GROUP = N_HEADS // N_KV_GROUPS
HEAD_DIM = 128
CMP_BLOCK = 32
CMP_STRIDE = 16
SEL_BLOCK = 64
SEL_SHIFT = 6
SEL_TOP_N = 16
WINDOW = 512
Q_BLOCK = 128
N_OVERLAP = (SEL_BLOCK + CMP_BLOCK) // CMP_STRIDE - 1
FORCE_BONUS = 1.0e4
CONV_CH = D_MODEL // 2
CONV_WIDTH = 31
N_EXPERTS = 32
TOP_K = 4
D_FF = D_MODEL
SWIGLU_LIMIT = 7.0
SWIGLU_ALPHA = 1.702
LN_EPS = 1e-5
DEPTH = 1
DEEPNORM_ALPHA = (2 * DEPTH) ** 0.25
NEG_INF = -1e30
TINY = 1e-30

Q_DIM = N_HEADS * HEAD_DIM
KV_DIM = N_KV_GROUPS * HEAD_DIM
NSA_GATE_DIM = N_HEADS * 3

LANES = 128
SUBLANES = 8
VMEM_LIMIT = 56 * 1024 * 1024

COL_Q = 0
COL_GLU = COL_Q + Q_DIM
COL_MERGE = COL_GLU + 2 * CONV_CH
COL_KC = COL_MERGE + 2 * D_MODEL
COL_VC = COL_KC + KV_DIM
COL_KS = COL_VC + KV_DIM
COL_VS = COL_KS + KV_DIM
COL_KW = COL_VS + KV_DIM
COL_VW = COL_KW + KV_DIM
COL_GATE = COL_VW + KV_DIM
PROJ_W = COL_GATE + N_KV_GROUPS * LANES

BF16 = jnp.bfloat16
F32 = jnp.float32

MOE_BLOCK = 256
MOE_UP_TILE = 1024
MOE_SEGMENTS = 4

_ALIBI = np.exp2(-8.0 * np.arange(1, N_HEADS + 1, dtype=np.float32) / N_HEADS).astype(np.float32)
_ALIBI = _ALIBI.reshape(N_KV_GROUPS, HEADS_PER_GROUP)


def _cparams(sem, vmem=VMEM_LIMIT):
    return pltpu.CompilerParams(dimension_semantics=sem, vmem_limit_bytes=vmem)


def _inproj_kernel(x_ref, w_ref, o_ref, xb_ref):
    @pl.when(pl.program_id(1) == 0)
    def _():
        xb_ref[...] = x_ref[...].astype(BF16)

    o_ref[...] = jnp.dot(xb_ref[...], w_ref[...],
                         preferred_element_type=F32).astype(o_ref.dtype)


def _inproj(x2d, w_p):
    T, D = x2d.shape
    N = w_p.shape[1]
    tm = min(512, T)
    tn = N // 3
    assert T % tm == 0 and N % tn == 0 and tn % (2 * LANES) == 0
    return pl.pallas_call(
        _inproj_kernel,
        out_shape=jax.ShapeDtypeStruct((T, N), BF16),
        grid=(T // tm, N // tn),
        in_specs=[pl.BlockSpec((tm, D), lambda i, j: (i, 0)),
                  pl.BlockSpec((D, tn), lambda i, j: (0, j))],
        out_specs=pl.BlockSpec((tm, tn), lambda i, j: (i, j)),
        scratch_shapes=[pltpu.VMEM((tm, D), BF16)],
        compiler_params=_cparams(("arbitrary", "arbitrary")),
        name="inproj",
    )(x2d, w_p)


def _gelu_tanh(x):
    c = math.sqrt(2.0 / math.pi)
    return 0.5 * x * (1.0 + jnp.tanh(c * (x + 0.044715 * (x * x * x))))


def _compress_kernel(x_ref, pe_ref, w1_ref, b1_ref, w2_ref, b2_ref, o_ref, xf_ref):
    S = x_ref.shape[0]
    n16 = S // CMP_STRIDE
    xf_ref[...] = x_ref[...].astype(F32)
    top = jnp.zeros((n16, HEAD_DIM), F32)
    bot = jnp.zeros((n16, HEAD_DIM), F32)
    for j in range(CMP_STRIDE):
        xj = xf_ref[pl.ds(j, n16, stride=CMP_STRIDE), :]
        lo, hi = j * HEAD_DIM, (CMP_STRIDE + j) * HEAD_DIM
        top = top + jnp.dot((xj + pe_ref[0, :, lo:lo + HEAD_DIM]).astype(BF16),
                            w1_ref[0, lo:lo + HEAD_DIM, :], preferred_element_type=F32)
        bot = bot + jnp.dot((xj + pe_ref[0, :, hi:hi + HEAD_DIM]).astype(BF16),
                            w1_ref[0, hi:hi + HEAD_DIM, :], preferred_element_type=F32)
    pre = top + pltpu.roll(bot, n16 - 1, 0) + b1_ref[0]
    h = _gelu_tanh(pre)
    o = jnp.dot(h.astype(BF16), w2_ref[0], preferred_element_type=F32) + b2_ref[0]
    o_ref[0, 0] = o.astype(o_ref.dtype)


def _compress(proj, pe, w1, b1, w2, b2, B, S):
    n_slot = 2 * N_KV_GROUPS
    n16 = S // CMP_STRIDE
    wide = CMP_BLOCK * HEAD_DIM
    kv = lambda s, b: (s // N_KV_GROUPS, 0, 0)
    return pl.pallas_call(
        _compress_kernel,
        out_shape=jax.ShapeDtypeStruct((n_slot, B, n16, HEAD_DIM), BF16),
        grid=(n_slot, B),
        in_specs=[pl.BlockSpec((S, HEAD_DIM), lambda s, b: (b, COL_KC // HEAD_DIM + s)),
                  pl.BlockSpec((1, 1, wide), kv),
                  pl.BlockSpec((1, wide, HEAD_DIM), kv),
                  pl.BlockSpec((1, 1, HEAD_DIM), kv),
                  pl.BlockSpec((1, HEAD_DIM, HEAD_DIM), kv),
                  pl.BlockSpec((1, 1, HEAD_DIM), kv)],
        out_specs=pl.BlockSpec((1, 1, n16, HEAD_DIM), lambda s, b: (s, b, 0, 0)),
        scratch_shapes=[pltpu.VMEM((S, HEAD_DIM), F32)],
        compiler_params=_cparams(("arbitrary", "arbitrary")),
        name="compress",
    )(proj, pe, w1, b1, w2, b2)


LOG2E = 1.4426950408889634
KAUG = 2 * HEAD_DIM
ROW_ALIBI = 0
ROW_SLOT = 16
SLC_GROUP = 8
SLC_TAIL = 4
assert 2 * SLC_GROUP == 16 and SLC_TAIL == 4
GROUP_W = HEADS_PER_GROUP * Q_BLOCK
WIN_TILES = WINDOW // Q_BLOCK
WIN_KEYS = WINDOW + Q_BLOCK


def _window_pos_cols():
    rel = np.arange(WIN_KEYS) - WINDOW
    a = (rel // SEL_BLOCK) * SEL_BLOCK
    r = rel - a
    cols = np.zeros((WIN_KEYS, LANES), np.float32)
    cols[:, ROW_ALIBI:ROW_ALIBI + 3] = a[:, None]
    cols[:, ROW_ALIBI + 3:ROW_ALIBI + 6] = r[:, None]
    return jnp.asarray(cols, BF16)


def _alibi_rows():
    s2 = jnp.asarray(_ALIBI * np.float32(LOG2E), F32)
    hi = s2.astype(BF16)
    r1 = s2 - hi.astype(F32)
    mid = r1.astype(BF16)
    lo = (r1 - mid.astype(F32)).astype(BF16)
    trip = jnp.stack([hi, mid, lo, hi, mid, lo], axis=1)
    rows = jnp.pad(trip, ((0, 0), (0, 16 - 6), (0, 0)))
    return jnp.repeat(rows, Q_BLOCK, axis=2)


def _dot_ta(a, b):
    return lax.dot_general(a, b, (((0,), (0,)), ((), ())), preferred_element_type=F32)


def _fill_qaug(qaug, q_ref, alibi_ref):
    for h in range(HEADS_PER_GROUP):
        qaug[pl.ds(0, HEAD_DIM), pl.ds(h * Q_BLOCK, Q_BLOCK)] = (
            q_ref[:, h * HEAD_DIM:(h + 1) * HEAD_DIM].T)
    qaug[pl.ds(HEAD_DIM, 16), :] = alibi_ref[0]
    qaug[pl.ds(HEAD_DIM + 16, HEAD_DIM - 16), :] = jnp.zeros((HEAD_DIM - 16, GROUP_W), BF16)


def _pos_cols(a_val, r_val, lane, slot_lane=None):
    base = jnp.where((lane >= ROW_ALIBI + 3) & (lane < ROW_ALIBI + 6), r_val, 0.0)
    if slot_lane is not None:
        base = jnp.where(lane == slot_lane, 1.0, base)
    return jnp.where(lane < ROW_ALIBI + 3, a_val, base)


def _nsa_cmp_branch(c, kc_ref, vc_ref, ocmp_ref, selb_ref, jmin_ref, qaug, n_selb):
    n_vis = c * (Q_BLOCK // CMP_STRIDE) + (Q_BLOCK - CMP_BLOCK) // CMP_STRIDE + 1
    tiles = kc_ref.shape[2] // LANES
    for i in range(1, tiles + 1):
        cond = n_vis > (i - 1) * LANES
        if i < tiles:
            cond = cond & (n_vis <= i * LANES)
        pl.when(cond)(functools.partial(
            _nsa_cmp_rows, c, i * LANES, kc_ref, vc_ref, ocmp_ref, selb_ref, jmin_ref, qaug,
            n_selb))


def _nsa_cmp_rows(c, ncp, kc_ref, vc_ref, ocmp_ref, selb_ref, jmin_ref, qaug, n_selb):
    nsp = selb_ref.shape[0]

    n_i = lax.broadcasted_iota(jnp.int32, (ncp, LANES), 0)
    lane = lax.broadcasted_iota(jnp.int32, (ncp, LANES), 1)
    end_rel = n_i * CMP_STRIDE + (CMP_BLOCK - 1) - c * Q_BLOCK
    a_val = jnp.left_shift(jnp.right_shift(end_rel, SEL_SHIFT), SEL_SHIFT).astype(F32)
    r_val = (end_rel & (SEL_BLOCK - 1)).astype(F32)
    pc = _pos_cols(a_val, r_val, lane).astype(BF16)
    s = jnp.dot(jnp.concatenate([kc_ref[0, 0, pl.ds(0, ncp), :], pc], axis=1), qaug[...],
                preferred_element_type=F32)

    tq = lax.broadcasted_iota(jnp.int32, (1, Q_BLOCK), 1)
    t_row = c * Q_BLOCK + tq
    n_col = lax.broadcasted_iota(jnp.int32, (ncp, Q_BLOCK), 0)
    mask_bias = jnp.where(n_col * CMP_STRIDE + (CMP_BLOCK - 1) <= t_row, 0.0, NEG_INF)
    has_valid = t_row >= CMP_BLOCK - 1

    imp = jnp.zeros((ncp, Q_BLOCK), F32)
    es, rls = [], []
    for h in range(HEADS_PER_GROUP):
        sh = s[:, h * Q_BLOCK:(h + 1) * Q_BLOCK] + mask_bias
        m = jnp.max(sh, axis=0, keepdims=True)
        e = jnp.exp2(sh - m)
        l = jnp.sum(e, axis=0, keepdims=True)
        rl = jnp.where(has_valid, 1.0 / jnp.maximum(l, TINY), 0.0)
        imp = imp + e * rl
        es.append(e.astype(BF16))
        rls.append(rl)
    oT = _dot_ta(vc_ref[0, 0, pl.ds(0, ncp), :], jnp.concatenate(es, axis=1))
    ocmp_ref[...] = oT * jnp.concatenate(rls, axis=1)

    j_i = lax.broadcasted_iota(jnp.int32, (nsp, ncp), 0)
    n_j = lax.broadcasted_iota(jnp.int32, (nsp, ncp), 1)
    lo = j_i * (SEL_BLOCK // CMP_STRIDE) - (CMP_BLOCK // CMP_STRIDE) + 1
    ovl = ((n_j >= lo) & (n_j < lo + N_OVERLAP)).astype(BF16)
    i_hi = imp.astype(BF16)
    r1 = imp - i_hi.astype(F32)
    i_mid = r1.astype(BF16)
    i_lo = (r1 - i_mid.astype(F32)).astype(BF16)
    imp_sel = (jnp.dot(ovl, i_hi, preferred_element_type=F32)
               + jnp.dot(ovl, i_mid, preferred_element_type=F32)
               + jnp.dot(ovl, i_lo, preferred_element_type=F32))

    blk = lax.broadcasted_iota(jnp.int32, (nsp, Q_BLOCK), 0)
    cur = jnp.right_shift(t_row, SEL_SHIFT)
    valid = blk * SEL_BLOCK <= t_row
    forced = (blk == 0) | (blk == cur) | (blk == cur - 1)
    score = jnp.where(valid, imp_sel + jnp.where(forced, FORCE_BONUS, 0.0), -1.0)
    score = jnp.where(blk < n_selb, score, -3.0)
    selb = jnp.full((nsp, Q_BLOCK), NEG_INF, F32)
    for _ in range(min(SEL_TOP_N, n_selb)):
        mx = jnp.max(score, axis=0, keepdims=True)
        first = jnp.min(jnp.where(score == mx, blk, nsp), axis=0, keepdims=True)
        pick = blk == first
        selb = jnp.where(pick, 0.0, selb)
        score = jnp.where(pick, -2.0, score)
    selb_ref[...] = selb
    blk_col = lax.broadcasted_iota(jnp.int32, (nsp, 1), 0)
    any_sel = jnp.max(selb, axis=1, keepdims=True) > 0.5 * NEG_INF
    n_past = (Q_BLOCK // SEL_BLOCK) * c
    jmin_ref[0] = jnp.min(jnp.where(any_sel & (blk_col >= 1) & (blk_col < n_past),
                                    blk_col, n_past))


def _tile_scores(qaug, k_tile, pc):
    kaug = jnp.concatenate([k_tile, pc.astype(BF16)], axis=1)
    return jnp.dot(kaug, qaug[...], preferred_element_type=F32)


def _softmax_tile(st, qaug, k_tile, pc, v_tile, bias, first):
    _softmax_update(st, _tile_scores(qaug, k_tile, pc), v_tile, bias, first)


def _softmax_update(st, s, v_tile, bias, first):
    m_ref, l_ref, acc_ref = st
    if bias is not None:
        s = jnp.concatenate([s[:, h * Q_BLOCK:(h + 1) * Q_BLOCK] + bias
                             for h in range(HEADS_PER_GROUP)], axis=1)
    mx = jnp.max(s, axis=0, keepdims=True)
    if first:
        m_new = mx
    else:
        m_old = m_ref[pl.ds(0, 1), :]
        m_new = jnp.maximum(m_old, mx)
    p = jnp.exp2(s - m_new)
    ps = jnp.sum(p, axis=0, keepdims=True)
    pv = _dot_ta(v_tile, p.astype(BF16))
    if first:
        l_ref[pl.ds(0, 1), :] = ps
        acc_ref[...] = pv
    else:
        alpha = jnp.exp2(m_old - m_new)
        l_ref[pl.ds(0, 1), :] = alpha * l_ref[pl.ds(0, 1), :] + ps
        acc_ref[...] = alpha * acc_ref[...] + pv
    m_ref[pl.ds(0, 1), :] = m_new


def _bias_rows(s, bias):
    return jnp.concatenate([s[:, h * Q_BLOCK:(h + 1) * Q_BLOCK] + bias
                            for h in range(HEADS_PER_GROUP)], axis=1)


def _nsa_kernel(q_ref, kc_ref, vc_ref, ks_ref, vs_ref, kw_ref, vw_ref, gate_ref, alibi_ref,
                wpc_ref, o_ref, qaug, m_ref, l_ref, acc_ref, owin_ref, ocmp_ref, selb_ref,
                jmin_ref, *, n_selb):
    c = pl.program_id(2)
    st = (m_ref, l_ref, acc_ref)
    _fill_qaug(qaug, q_ref, alibi_ref)
    _nsa_cmp_branch(c, kc_ref, vc_ref, ocmp_ref, selb_ref, jmin_ref, qaug, n_selb)
    n_past = (Q_BLOCK // SEL_BLOCK) * c
    jmin = jmin_ref[0]
    n_act = jnp.where(c >= 1, 1, 0) + (n_past - jmin)

    lane = lax.broadcasted_iota(jnp.int32, (SEL_BLOCK, LANES), 1)
    r_val = lax.broadcasted_iota(jnp.int32, (SEL_BLOCK, LANES), 0).astype(F32)
    ki = lax.broadcasted_iota(jnp.int32, (Q_BLOCK, Q_BLOCK), 0)
    qi = lax.broadcasted_iota(jnp.int32, (Q_BLOCK, Q_BLOCK), 1)
    causal_bias = jnp.where(ki <= qi, 0.0, NEG_INF)
    tail_bias = jnp.where(ki > qi, 0.0, NEG_INF)

    def pair_cols(first_block_rel):
        return jnp.concatenate(
            [_pos_cols(float(SEL_BLOCK * (first_block_rel + i)), r_val, lane) for i in range(2)], axis=0)

    def finish():
        return acc_ref[...] / jnp.maximum(l_ref[pl.ds(0, 1), :], TINY)

    n_wt = WIN_TILES
    base = pl.multiple_of(c * Q_BLOCK, Q_BLOCK)

    @pl.when(c >= n_wt)
    def _():
        start = pl.multiple_of((c - n_wt) * Q_BLOCK, Q_BLOCK)
        kaug = jnp.concatenate([kw_ref[pl.ds(start, WIN_KEYS), :], wpc_ref[...]], axis=1)
        s = jnp.dot(kaug, qaug[...], preferred_element_type=F32)
        s_diag = _tile_scores(qaug, ks_ref[pl.ds(base, Q_BLOCK), :], pair_cols(0))
        s = jnp.concatenate([_bias_rows(s[:Q_BLOCK], tail_bias), s[Q_BLOCK:WINDOW],
                             _bias_rows(s[WINDOW:], causal_bias)], axis=0)
        p = jnp.exp2(s - jnp.max(s, axis=0, keepdims=True))
        l = jnp.sum(p, axis=0, keepdims=True)
        pv = _dot_ta(vw_ref[pl.ds(start, WIN_KEYS), :], p.astype(BF16))
        owin_ref[...] = pv / jnp.maximum(l, TINY)
        _softmax_update(st, s_diag, vs_ref[pl.ds(base, Q_BLOCK), :], causal_bias, True)

    @pl.when(c < n_wt)
    def _():
        _softmax_tile(st, qaug, kw_ref[pl.ds(base, Q_BLOCK), :], pair_cols(0),
                      vw_ref[pl.ds(base, Q_BLOCK), :], causal_bias, True)
        for w in range(1, n_wt):
            @pl.when(c - n_wt + w >= 0)
            def _(w=w):
                start = pl.multiple_of((c - n_wt + w) * Q_BLOCK, Q_BLOCK)
                _softmax_tile(st, qaug, kw_ref[pl.ds(start, Q_BLOCK), :],
                              pair_cols(2 * (w - n_wt)), vw_ref[pl.ds(start, Q_BLOCK), :],
                              None, False)
        owin_ref[...] = finish()
        _softmax_tile(st, qaug, ks_ref[pl.ds(base, Q_BLOCK), :], pair_cols(0),
                      vs_ref[pl.ds(base, Q_BLOCK), :], causal_bias, True)

    sub = lax.broadcasted_iota(jnp.int32, (16, GROUP_W), 0)

    def slc_tiles(pos0, n_tiles, per_tile):
        slot_rows = jnp.zeros((16, GROUP_W), F32)
        tiles = []
        for ti in range(n_tiles):
            ks_t, vs_t, pcs = [], [], []
            for k in range(per_tile):
                slot = ti * per_tile + k
                pos = pos0 + slot
                live = pos < n_act
                j = jnp.where(live & (pos > 0), jmin + pos - 1, 0)
                off = pl.multiple_of(j * SEL_BLOCK, SEL_BLOCK)
                ks_t.append(ks_ref[pl.ds(off, SEL_BLOCK), :])
                vs_t.append(vs_ref[pl.ds(off, SEL_BLOCK), :])
                a_val = ((j - 2 * c) * SEL_BLOCK).astype(F32)
                pcs.append(_pos_cols(a_val, r_val, lane, slot_lane=ROW_SLOT + slot))
                rowk = selb_ref[pl.ds(j, 1), :]
                rowk = jnp.where(live, rowk, NEG_INF)
                slot_rows = jnp.where(sub == slot, jnp.tile(rowk, (1, HEADS_PER_GROUP)), slot_rows)
            tiles.append((jnp.concatenate(ks_t, axis=0), jnp.concatenate(pcs, axis=0),
                          jnp.concatenate(vs_t, axis=0)))
        qaug[pl.ds(HEAD_DIM + ROW_SLOT, 16), :] = slot_rows.astype(BF16)
        scores = [_tile_scores(qaug, k_tile, pc) for k_tile, pc, _ in tiles]
        for s, (_, _, v_tile) in zip(scores, tiles):
            _softmax_update(st, s, v_tile, None, False)

    pair = 2 * SLC_GROUP
    rem = n_act & (pair - 1)
    n_pairs = jnp.right_shift(n_act, 4) + jnp.where(rem > SLC_GROUP, 1, 0)
    n_tail = jnp.where(rem > SLC_GROUP, 0, jnp.right_shift(rem + SLC_TAIL - 1, 2))

    def pair_body(i, carry):
        slc_tiles(i * pair, 2, SLC_GROUP)
        return carry

    def tail_body(i, carry):
        slc_tiles(n_pairs * pair + i * SLC_TAIL, 1, SLC_TAIL)
        return carry

    lax.fori_loop(0, n_pairs, pair_body, 0)
    lax.fori_loop(0, n_tail, tail_body, 0)
    o_slcT = finish()

    gT = jax.nn.sigmoid(gate_ref[...].astype(F32).T)
    o_cmpT = ocmp_ref[...]
    o_winT = owin_ref[...]
    outs = []
    for h in range(HEADS_PER_GROUP):
        hs = slice(h * Q_BLOCK, (h + 1) * Q_BLOCK)
        oT = (gT[3 * h:3 * h + 1, :] * o_cmpT[:, hs] + gT[3 * h + 1:3 * h + 2, :] * o_slcT[:, hs]
              + gT[3 * h + 2:3 * h + 3, :] * o_winT[:, hs])
        outs.append(oT.T)
    o_ref[...] = jnp.concatenate(outs, axis=1).astype(o_ref.dtype)


def _nsa(proj, kcvc, alibi, B, S):
    T = B * S
    C = S // Q_BLOCK
    G = N_KV_GROUPS
    ncp = kcvc.shape[2]
    n_selb = S // SEL_BLOCK
    nsp = max(LANES, n_selb)
    gw = HEADS_PER_GROUP * HEAD_DIM
    kvspec = lambda col: pl.BlockSpec((S, HEAD_DIM), lambda b, g, c: (b, col // HEAD_DIM + g))
    return pl.pallas_call(
        functools.partial(_nsa_kernel, n_selb=n_selb),
        out_shape=jax.ShapeDtypeStruct((T, Q_DIM), BF16),
        grid=(B, G, C),
        in_specs=[pl.BlockSpec((Q_BLOCK, gw), lambda b, g, c: (b * C + c, COL_Q // gw + g)),
                  pl.BlockSpec((1, 1, ncp, HEAD_DIM), lambda b, g, c: (g, b, 0, 0)),
                  pl.BlockSpec((1, 1, ncp, HEAD_DIM), lambda b, g, c: (G + g, b, 0, 0)),
                  kvspec(COL_KS), kvspec(COL_VS), kvspec(COL_KW), kvspec(COL_VW),
                  pl.BlockSpec((Q_BLOCK, LANES), lambda b, g, c: (b * C + c, COL_GATE // LANES + g)),
                  pl.BlockSpec((1, 16, GROUP_W), lambda b, g, c: (g, 0, 0)),
                  pl.BlockSpec((WIN_KEYS, LANES), lambda b, g, c: (0, 0))],
        out_specs=pl.BlockSpec((Q_BLOCK, gw), lambda b, g, c: (b * C + c, g)),
        scratch_shapes=[pltpu.VMEM((KAUG, GROUP_W), BF16),
                        pltpu.VMEM((SUBLANES, GROUP_W), F32),
                        pltpu.VMEM((SUBLANES, GROUP_W), F32),
                        pltpu.VMEM((HEAD_DIM, GROUP_W), F32),
                        pltpu.VMEM((HEAD_DIM, GROUP_W), F32),
                        pltpu.VMEM((HEAD_DIM, GROUP_W), F32),
                        pltpu.VMEM((nsp, Q_BLOCK), F32),
                        pltpu.SMEM((1,), jnp.int32)],
        compiler_params=_cparams(("arbitrary", "arbitrary", "arbitrary")),
        name="nsa",
    )(proj, kcvc, kcvc, proj, proj, proj, proj, proj, alibi, _window_pos_cols())


CONV_TS = 512
CONV_HALO = 32
CONV_RC = 64
CONV_CC = 256


def _conv_kernel(a_ref, b_ref, w_ref, cb_ref, g_ref, beta_ref, o_ref, hbuf, ybuf, wrep):
    si = pl.program_id(1)

    @pl.when(si == 0)
    def _():
        hbuf[pl.ds(0, CONV_HALO), :] = jnp.zeros((CONV_HALO, CONV_CH), F32)

    @pl.when(si > 0)
    def _():
        hbuf[pl.ds(0, CONV_HALO), :] = hbuf[pl.ds(CONV_TS, CONV_HALO), :]

    a = a_ref[...].astype(F32)
    bb = b_ref[...].astype(F32)
    hbuf[pl.ds(CONV_HALO, CONV_TS), :] = a * jax.nn.sigmoid(bb)

    for k in range(CONV_WIDTH):
        wrep[pl.ds(k * SUBLANES, SUBLANES), :] = jnp.broadcast_to(w_ref[k:k + 1, :],
                                                                  (SUBLANES, CONV_CH))

    off = CONV_HALO - (CONV_WIDTH - 1)
    groups = CONV_RC // SUBLANES
    for cc in range(CONV_CH // CONV_CC):
        cols = slice(cc * CONV_CC, (cc + 1) * CONV_CC)

        def row_body(r, carry, cols=cols):
            r0 = pl.multiple_of(r * CONV_RC, CONV_RC)
            win = hbuf[pl.ds(r0, CONV_RC + CONV_HALO), cols]
            acc = jnp.zeros((groups, SUBLANES, CONV_CC), F32)
            for res in range(SUBLANES):
                taps = [k for k in range(CONV_WIDTH) if (off + k) % SUBLANES == res]
                span = max(off + k for k in taps) - res + CONV_RC
                shifted = win[res:res + span, :]
                for k in taps:
                    a0 = off + k - res
                    rows = shifted[a0:a0 + CONV_RC, :].reshape(groups, SUBLANES, CONV_CC)
                    acc = acc + rows * wrep[pl.ds(k * SUBLANES, SUBLANES), cols][None]
            ybuf[pl.ds(r0, CONV_RC), cols] = acc.reshape(CONV_RC, CONV_CC)
            return carry

        lax.fori_loop(0, CONV_TS // CONV_RC, row_body, 0)

    y = ybuf[...] + cb_ref[...]
    mu = jnp.mean(y, axis=-1, keepdims=True)
    yc = y - mu
    var = jnp.mean(yc * yc, axis=-1, keepdims=True)
    z = yc * lax.rsqrt(var + LN_EPS) * g_ref[...] + beta_ref[...]
    o_ref[...] = (z * jax.nn.sigmoid(z)).astype(o_ref.dtype)


def _conv(proj, conv_w, conv_b, ln_g, ln_b, B, S):
    T = B * S
    ts = CONV_TS
    assert S % ts == 0
    nS = S // ts
    ca = COL_GLU // CONV_CH
    vec = lambda: pl.BlockSpec((1, CONV_CH), lambda b, s: (0, 0))
    return pl.pallas_call(
        _conv_kernel,
        out_shape=jax.ShapeDtypeStruct((T, CONV_CH), BF16),
        grid=(B, nS),
        in_specs=[pl.BlockSpec((ts, CONV_CH), lambda b, s: (b * nS + s, ca)),
                  pl.BlockSpec((ts, CONV_CH), lambda b, s: (b * nS + s, ca + 1)),
                  pl.BlockSpec((CONV_WIDTH, CONV_CH), lambda b, s: (0, 0)),
                  vec(), vec(), vec()],
        out_specs=pl.BlockSpec((ts, CONV_CH), lambda b, s: (b * nS + s, 0)),
        scratch_shapes=[pltpu.VMEM((CONV_HALO + ts, CONV_CH), F32),
                        pltpu.VMEM((ts, CONV_CH), F32),
                        pltpu.VMEM((CONV_WIDTH * SUBLANES, CONV_CH), F32)],
        compiler_params=_cparams(("arbitrary", "arbitrary")),
        name="conformer_conv",
    )(proj, proj, conv_w, conv_b.reshape(1, -1), ln_g.reshape(1, -1), ln_b.reshape(1, -1))


def _merge_kernel(o_ref, h_ref, wa_ref, wb_ref, ga_ref, gb_ref, out_ref):
    ya = jnp.dot(o_ref[...], wa_ref[...], preferred_element_type=F32)
    yb = jnp.dot(h_ref[...], wb_ref[...], preferred_element_type=F32)
    ga = jax.nn.sigmoid(ga_ref[...].astype(F32))
    gb = jax.nn.sigmoid(gb_ref[...].astype(F32))
    out_ref[...] = (ga * ya + gb * yb).astype(out_ref.dtype)


def _merge(o_nsa, h_conv, wa, wb, proj):
    T = o_nsa.shape[0]
    tm = min(512, T)
    tn = D_MODEL
    nN = D_MODEL // tn
    ga0 = COL_MERGE // tn
    return pl.pallas_call(
        _merge_kernel,
        out_shape=jax.ShapeDtypeStruct((T, D_MODEL), BF16),
        grid=(T // tm, nN),
        in_specs=[pl.BlockSpec((tm, Q_DIM), lambda i, j: (i, 0)),
                  pl.BlockSpec((tm, CONV_CH), lambda i, j: (i, 0)),
                  pl.BlockSpec((Q_DIM, tn), lambda i, j: (0, j)),
                  pl.BlockSpec((CONV_CH, tn), lambda i, j: (0, j)),
                  pl.BlockSpec((tm, tn), lambda i, j: (i, ga0 + j)),
                  pl.BlockSpec((tm, tn), lambda i, j: (i, ga0 + nN + j))],
        out_specs=pl.BlockSpec((tm, tn), lambda i, j: (i, j)),
        compiler_params=_cparams(("arbitrary", "arbitrary")),
        name="merge",
    )(o_nsa, h_conv, wa, wb, proj, proj)


def _layer_norm_rows(y, g, b):
    mu = jnp.mean(y, axis=-1, keepdims=True)
    yc = y - mu
    var = jnp.mean(yc * yc, axis=-1, keepdims=True)
    return yc * lax.rsqrt(var + LN_EPS) * g + b


def _split2(v):
    hi = v.astype(BF16)
    return hi, (v - hi.astype(F32)).astype(BF16)


def _outproj_kernel(mix_ref, w_ref, x_ref, g_ref, b_ref, rw_hi_ref, rw_lo_ref, rb_ref,
                    x1_ref, x1b_ref, ti_ref, tg_ref, rk_ref, cnt_ref):
    m = jnp.dot(mix_ref[...], w_ref[...], preferred_element_type=F32)
    x1 = _layer_norm_rows(DEEPNORM_ALPHA * x_ref[...] + m, g_ref[...], b_ref[...])
    x1_ref[...] = x1
    x1b_ref[...] = x1.astype(BF16)
    x_hi, x_lo = _split2(x1)
    logits = (jnp.dot(x_hi, rw_hi_ref[...], preferred_element_type=F32)
              + jnp.dot(x_hi, rw_lo_ref[...], preferred_element_type=F32)
              + jnp.dot(x_lo, rw_hi_ref[...], preferred_element_type=F32)) + rb_ref[...]
    lane = lax.broadcasted_iota(jnp.int32, logits.shape, 1)
    logits = jnp.where(lane < N_EXPERTS, logits, -jnp.inf)
    ti = jnp.zeros(logits.shape, jnp.int32)
    tv = jnp.zeros(logits.shape, F32)
    top0 = None
    den = jnp.zeros((logits.shape[0], 1), F32)
    picks = []
    for k in range(TOP_K):
        mx = jnp.max(logits, axis=-1, keepdims=True)
        idx = jnp.min(jnp.where(logits == mx, lane, LANES), axis=-1, keepdims=True)
        if top0 is None:
            top0 = mx
        e = jnp.exp(mx - top0)
        den = den + e
        ti = jnp.where(lane == k, idx, ti)
        tv = jnp.where(lane == k, e, tv)
        picks.append(lane == idx)
        logits = jnp.where(picks[-1], -jnp.inf, logits)
    ti_ref[...] = ti
    tg_ref[...] = tv / den

    @pl.when(pl.program_id(0) == 0)
    def _():
        cnt_ref[...] = jnp.zeros(cnt_ref.shape, F32)

    tm = logits.shape[0]
    chosen = functools.reduce(jnp.logical_or, picks).astype(BF16)
    tri = (lax.broadcasted_iota(jnp.int32, (tm, tm), 1)
           <= lax.broadcasted_iota(jnp.int32, (tm, tm), 0)).astype(BF16)
    upto = jnp.dot(tri, chosen, preferred_element_type=F32) + cnt_ref[pl.ds(0, 1), :]
    rk = jnp.zeros(logits.shape, jnp.int32)
    for k in range(TOP_K):
        r_k = jnp.sum(jnp.where(picks[k], upto, 0.0), axis=-1, keepdims=True) - 1.0
        rk = jnp.where(lane == k, r_k.astype(jnp.int32), rk)
    rk_ref[...] = rk
    cnt_ref[...] = jnp.broadcast_to(upto[tm - 1:tm, :], cnt_ref.shape)


def _outproj(mix, w_out, x2d, g, b, rw_hi, rw_lo, rb):
    T = mix.shape[0]
    tm = min(256, T)
    full = lambda shape: pl.BlockSpec(shape, lambda i: (0, 0))
    rowb = lambda w: pl.BlockSpec((tm, w), lambda i: (i, 0))
    return pl.pallas_call(
        _outproj_kernel,
        out_shape=(jax.ShapeDtypeStruct((T, D_MODEL), F32),
                   jax.ShapeDtypeStruct((T, D_MODEL), BF16),
                   jax.ShapeDtypeStruct((T, LANES), jnp.int32),
                   jax.ShapeDtypeStruct((T, LANES), F32),
                   jax.ShapeDtypeStruct((T, LANES), jnp.int32),
                   jax.ShapeDtypeStruct((SUBLANES, LANES), F32)),
        grid=(T // tm,),
        in_specs=[rowb(D_MODEL), full((D_MODEL, D_MODEL)), rowb(D_MODEL),
                  full((1, D_MODEL)), full((1, D_MODEL)),
                  full((D_MODEL, LANES)), full((D_MODEL, LANES)), full((1, LANES))],
        out_specs=(rowb(D_MODEL), rowb(D_MODEL), rowb(LANES), rowb(LANES), rowb(LANES),
                   full((SUBLANES, LANES))),
        compiler_params=_cparams(("arbitrary",)),
        name="outproj_ln_router",
    )(mix, w_out, x2d, g, b, rw_hi, rw_lo, rb)


def _moe_up_kernel(be_ref, nused_ref, x_ref, wg_ref, wu_ref, bg_ref, bu_ref, *rest, base):
    h_ref, wgb_ref, wub_ref = rest[-3:]
    local = pl.program_id(1)
    r = base + local
    prev = be_ref[jnp.maximum(r - 1, 0)]
    fresh = (local == 0) | (be_ref[r] != prev)

    @pl.when(fresh)
    def _():
        wgb_ref[...] = wg_ref[0].astype(BF16)
        wub_ref[...] = wu_ref[0].astype(BF16)

    @pl.when(r < nused_ref[0])
    def _():
        x = x_ref[...]
        gt = jnp.dot(x, wgb_ref[...], preferred_element_type=F32) + bg_ref[0]
        up = jnp.dot(x, wub_ref[...], preferred_element_type=F32) + bu_ref[0]
        gt = jnp.minimum(gt, SWIGLU_LIMIT)
        up = jnp.clip(up, -SWIGLU_LIMIT, SWIGLU_LIMIT)
        h = gt * jax.nn.sigmoid(SWIGLU_ALPHA * gt) * (up + 1.0)
        h_ref[...] = h.astype(h_ref.dtype)


def _moe_row_maps(nblk):
    rd = lambda r, nu: jnp.minimum(r, nu[0] - 1)
    wr = lambda r, nu: jnp.where(r < nu[0], r, nblk)
    return rd, wr


def _moe_up(blk_e, n_used, xs_seg, seg, nblk, hbuf, w_gate, w_up, b_gate, b_up):
    bm, tf = MOE_BLOCK, MOE_UP_TILE
    seg_blk = xs_seg.shape[0] // bm
    base = seg * seg_blk
    rd = lambda r, nu: jnp.clip(nu[0] - 1 - base, 0, r)
    wr = lambda r, nu: jnp.where(base + r < nu[0], base + r, nblk)
    wspec = pl.BlockSpec((1, D_MODEL, tf), lambda f, r, be, nu: (be[base + r], 0, f))
    bspec = pl.BlockSpec((1, 1, tf), lambda f, r, be, nu: (be[base + r], 0, f))
    in_specs = [pl.BlockSpec((bm, D_MODEL), lambda f, r, be, nu: (rd(r, nu), 0)),
                wspec, wspec, bspec, bspec]
    args = [blk_e, n_used, xs_seg, w_gate, w_up, b_gate.reshape(N_EXPERTS, 1, D_FF),
            b_up.reshape(N_EXPERTS, 1, D_FF)]
    aliases = {}
    if hbuf is not None:
        in_specs.append(pl.BlockSpec(memory_space=pl.ANY))
        aliases = {len(args): 0}
        args.append(hbuf)
    grid_spec = pltpu.PrefetchScalarGridSpec(
        num_scalar_prefetch=2,
        grid=(D_FF // tf, seg_blk),
        in_specs=in_specs,
        out_specs=pl.BlockSpec((bm, tf), lambda f, r, be, nu: (wr(r, nu), f)),
        scratch_shapes=[pltpu.VMEM((D_MODEL, tf), BF16), pltpu.VMEM((D_MODEL, tf), BF16)],
    )
    return pl.pallas_call(
        functools.partial(_moe_up_kernel, base=base),
        out_shape=jax.ShapeDtypeStruct(((nblk + 1) * bm, D_FF), BF16),
        grid_spec=grid_spec,
        input_output_aliases=aliases,
        compiler_params=_cparams(("arbitrary", "arbitrary")),
        name="moe_up",
    )(*args)


def _moe_down_kernel(be_ref, nused_ref, h_ref, wd_ref, bd_ref, o_ref, wdb_ref):
    r = pl.program_id(0)
    prev = be_ref[jnp.maximum(r - 1, 0)]
    fresh = (r == 0) | (be_ref[r] != prev)

    @pl.when(fresh)
    def _():
        wdb_ref[...] = wd_ref[0].astype(BF16)

    @pl.when(r < nused_ref[0])
    def _():
        y = jnp.dot(h_ref[...], wdb_ref[...], preferred_element_type=F32) + bd_ref[0]
        o_ref[...] = y.astype(o_ref.dtype)


def _moe_down(blk_e, n_used, h, w_down, b_down):
    bm = MOE_BLOCK
    nblk = h.shape[0] // bm - 1
    rd, wr = _moe_row_maps(nblk)
    grid_spec = pltpu.PrefetchScalarGridSpec(
        num_scalar_prefetch=2,
        grid=(nblk,),
        in_specs=[pl.BlockSpec((bm, D_FF), lambda r, be, nu: (rd(r, nu), 0)),
                  pl.BlockSpec((1, D_FF, D_MODEL), lambda r, be, nu: (be[r], 0, 0)),
                  pl.BlockSpec((1, 1, D_MODEL), lambda r, be, nu: (be[r], 0, 0))],
        out_specs=pl.BlockSpec((bm, D_MODEL), lambda r, be, nu: (wr(r, nu), 0)),
        scratch_shapes=[pltpu.VMEM((D_FF, D_MODEL), BF16)],
    )
    return pl.pallas_call(
        _moe_down_kernel,
        out_shape=jax.ShapeDtypeStruct(((nblk + 1) * bm, D_MODEL), BF16),
        grid_spec=grid_spec,
        compiler_params=_cparams(("arbitrary",)),
        name="moe_down",
    )(blk_e, n_used, h, w_down, b_down.reshape(N_EXPERTS, 1, D_MODEL))


def _final_kernel(x1_ref, *refs):
    y_refs, (tg_ref, g_ref, b_ref, o_ref) = refs[:TOP_K], refs[TOP_K:]
    tg = tg_ref[...]
    f = tg[:, 0:1] * y_refs[0][...].astype(F32)
    for k in range(1, TOP_K):
        f = f + tg[:, k:k + 1] * y_refs[k][...].astype(F32)
    o_ref[...] = _layer_norm_rows(DEEPNORM_ALPHA * x1_ref[...] + f, g_ref[...], b_ref[...])


def _final(x1, ys, tg, g, b):
    T = x1.shape[0]
    tm = min(512, T)
    rows = pl.BlockSpec((tm, D_MODEL), lambda i: (i, 0))
    vec = pl.BlockSpec((1, D_MODEL), lambda i: (0, 0))
    return pl.pallas_call(
        _final_kernel,
        out_shape=jax.ShapeDtypeStruct((T, D_MODEL), F32),
        grid=(T // tm,),
        in_specs=[rows] + [rows] * TOP_K + [pl.BlockSpec((tm, LANES), lambda i: (i, 0)), vec, vec],
        out_specs=rows,
        compiler_params=_cparams(("arbitrary",)),
        name="combine_ln",
    )(x1, *ys, tg, g, b)


def _prep_w_in(w_in):
    splits = np.cumsum([Q_DIM] + [KV_DIM] * 6 + [NSA_GATE_DIM, 2 * CONV_CH, 2 * D_MODEL])
    q = w_in[:, :splits[0]] * (LOG2E / math.sqrt(HEAD_DIM))
    kv = w_in[:, splits[0]:splits[6]]
    gate = w_in[:, splits[6]:splits[7]]
    glu = w_in[:, splits[7]:splits[8]]
    merge = w_in[:, splits[8]:splits[9]]
    per_g = HEADS_PER_GROUP * 3
    gate_tiles = [jnp.pad(gate[:, g * per_g:(g + 1) * per_g], ((0, 0), (0, LANES - per_g)))
                  for g in range(N_KV_GROUPS)]
    return jnp.concatenate([q, glu, merge, kv] + gate_tiles, axis=1).astype(BF16)


def _mixer(x2d, B, S, w_in, cmp_pe, cmp_w1, cmp_b1, cmp_w2, cmp_b2, w_nsa_proj,
           conv_w, conv_b, conv_ln_g, conv_ln_b, w_conv_proj):
    T = B * S
    G = N_KV_GROUPS
    proj = _inproj(x2d, _prep_w_in(w_in))

    kcvc = _compress(proj, cmp_pe.reshape(2, 1, CMP_BLOCK * HEAD_DIM),
                     cmp_w1.astype(BF16), cmp_b1.reshape(2, 1, HEAD_DIM),
                     cmp_w2.astype(BF16), cmp_b2.reshape(2, 1, HEAD_DIM), B, S)

    o_nsa = _nsa(proj, kcvc, _alibi_rows().astype(BF16), B, S)

    h_conv = _conv(proj, conv_w, conv_b, conv_ln_g, conv_ln_b, B, S)
    return _merge(o_nsa, h_conv, w_nsa_proj.astype(BF16), w_conv_proj.astype(BF16), proj)


def _moe(x1b, top_i, top_rank, expert_counts, w_gate, b_gate, w_up, b_up, w_down, b_down):
    T = x1b.shape[0]
    A = T * TOP_K
    bm = MOE_BLOCK
    flat_e = top_i[:, :TOP_K].reshape(A)
    rank = top_rank[:, :TOP_K].reshape(A)
    counts = expert_counts[0, :N_EXPERTS].astype(jnp.int32)
    padded = (counts + bm - 1) // bm * bm
    pad_end = jnp.cumsum(padded)
    pad_start = pad_end - padded
    dest = pad_start[flat_e] + rank
    P = -(-A // bm) * bm + N_EXPERTS * bm
    nblk = P // bm
    flat_tok = jnp.arange(A, dtype=jnp.int32) // TOP_K
    buf_tok = (jnp.arange(P, dtype=jnp.int32) % T).at[dest].set(flat_tok)
    blk_start = jnp.arange(nblk, dtype=jnp.int32) * bm
    blk_e = jnp.minimum(jnp.sum(blk_start[:, None] >= pad_end[None, :], axis=1),
                        N_EXPERTS - 1).astype(jnp.int32)
    n_used = (pad_end[-1] // bm).astype(jnp.int32).reshape(1)
    seg_rows = P // MOE_SEGMENTS
    assert seg_rows % bm == 0
    h = None
    for seg in range(MOE_SEGMENTS):
        xs_seg = x1b[buf_tok[seg * seg_rows:(seg + 1) * seg_rows]]
        h = _moe_up(blk_e, n_used, xs_seg, seg, nblk, h, w_gate, w_up, b_gate, b_up)
    out = _moe_down(blk_e, n_used, h, w_down, b_down)
    dest_k = dest.reshape(T, TOP_K)
    return [out[dest_k[:, k]] for k in range(TOP_K)]


def kernel(x, w_in, cmp_pe, cmp_w1, cmp_b1, cmp_w2, cmp_b2, w_nsa_proj, conv_w, conv_b, conv_ln_g, conv_ln_b, w_conv_proj, w_out, ln1_g, ln1_b, router_w, router_b, w_gate, b_gate, w_up, b_up, w_down, b_down, ln2_g, ln2_b):
    B, S, D = x.shape
    T = B * S
    x2d = x.reshape(T, D)
    for l in range(DEPTH):
        mix = _mixer(x2d, B, S, w_in[l], cmp_pe[l], cmp_w1[l], cmp_b1[l], cmp_w2[l], cmp_b2[l],
                     w_nsa_proj[l], conv_w[l], conv_b[l], conv_ln_g[l], conv_ln_b[l],
                     w_conv_proj[l])
        rw = jnp.pad(router_w[l], ((0, 0), (0, LANES - N_EXPERTS)))
        rw_hi = rw.astype(BF16)
        rw_lo = (rw - rw_hi.astype(F32)).astype(BF16)
        rb = jnp.pad(router_b[l], (0, LANES - N_EXPERTS)).reshape(1, LANES)
        x1, x1b, top_i, top_g, top_rank, expert_counts = _outproj(
            mix, w_out[l].astype(BF16), x2d, ln1_g[l].reshape(1, D), ln1_b[l].reshape(1, D),
            rw_hi, rw_lo, rb)
        y4 = _moe(x1b, top_i, top_rank, expert_counts, w_gate[l], b_gate[l], w_up[l], b_up[l],
                  w_down[l], b_down[l])
        x2d = _final(x1, y4, top_g, ln2_g[l].reshape(1, D), ln2_b[l].reshape(1, D))
    return x2d.reshape(B, S, D)
```

```python
import functools
import math

import numpy as np
import jax
import jax.numpy as jnp
from jax import lax
from jax.experimental import pallas as pl
from jax.experimental.pallas import tpu as pltpu

D_MODEL = 2048
N_HEADS = 16
N_KV_GROUPS = 2
HEADS_PER_GROUP = N_HEADS // N_KV_GROUPS
HEAD_DIM = 128
CMP_BLOCK = 32
CMP_STRIDE = 16
SEL_BLOCK = 64
SEL_SHIFT = 6
SEL_TOP_N = 16
WINDOW = 512
Q_BLOCK = 128
N_OVERLAP = (SEL_BLOCK + CMP_BLOCK) // CMP_STRIDE - 1
FORCE_BONUS = 1.0e4
CONV_CH = D_MODEL // 2
CONV_WIDTH = 31
N_EXPERTS = 32
TOP_K = 4
D_FF = D_MODEL
SWIGLU_LIMIT = 7.0
SWIGLU_ALPHA = 1.702
LN_EPS = 1e-5
DEPTH = 1
DEEPNORM_ALPHA = (2 * DEPTH) ** 0.25
NEG_INF = -1e30
TINY = 1e-30

Q_DIM = N_HEADS * HEAD_DIM
KV_DIM = N_KV_GROUPS * HEAD_DIM
NSA_GATE_DIM = N_HEADS * 3

LANES = 128
SUBLANES = 8
VMEM_LIMIT = 56 * 1024 * 1024

COL_Q = 0
COL_GLU = COL_Q + Q_DIM
COL_MERGE = COL_GLU + 2 * CONV_CH
COL_KC = COL_MERGE + 2 * D_MODEL
COL_VC = COL_KC + KV_DIM
COL_KS = COL_VC + KV_DIM
COL_VS = COL_KS + KV_DIM
COL_KW = COL_VS + KV_DIM
COL_VW = COL_KW + KV_DIM
COL_GATE = COL_VW + KV_DIM
PROJ_W = COL_GATE + N_KV_GROUPS * LANES

BF16 = jnp.bfloat16
F32 = jnp.float32

MOE_BLOCK = 512
MOE_UP_TILE = 1024
MOE_SEGMENTS = 4

_ALIBI = np.exp2(-8.0 * np.arange(1, N_HEADS + 1, dtype=np.float32) / N_HEADS).astype(np.float32)
_ALIBI = _ALIBI.reshape(N_KV_GROUPS, HEADS_PER_GROUP)


def _cparams(sem, vmem=VMEM_LIMIT):
    return pltpu.CompilerParams(dimension_semantics=sem, vmem_limit_bytes=vmem)


def _inproj_kernel(x_ref, w_ref, o_ref, xb_ref):
    @pl.when(pl.program_id(1) == 0)
    def _():
        xb_ref[...] = x_ref[...].astype(BF16)

    o_ref[...] = jnp.dot(xb_ref[...], w_ref[...],
                         preferred_element_type=F32).astype(o_ref.dtype)


def _inproj(x2d, w_p):
    T, D = x2d.shape
    N = w_p.shape[1]
    tm = min(512, T)
    tn = N // 3
    assert T % tm == 0 and N % tn == 0 and tn % (2 * LANES) == 0
    return pl.pallas_call(
        _inproj_kernel,
        out_shape=jax.ShapeDtypeStruct((T, N), BF16),
        grid=(T // tm, N // tn),
        in_specs=[pl.BlockSpec((tm, D), lambda i, j: (i, 0)),
                  pl.BlockSpec((D, tn), lambda i, j: (0, j))],
        out_specs=pl.BlockSpec((tm, tn), lambda i, j: (i, j)),
        scratch_shapes=[pltpu.VMEM((tm, D), BF16)],
        compiler_params=_cparams(("arbitrary", "arbitrary")),
        name="inproj",
    )(x2d, w_p)


def _gelu_tanh(x):
    c = math.sqrt(2.0 / math.pi)
    return 0.5 * x * (1.0 + jnp.tanh(c * (x + 0.044715 * (x * x * x))))


def _compress_kernel(x_ref, pe_ref, w1_ref, b1_ref, w2_ref, b2_ref, o_ref, xf_ref):
    S = x_ref.shape[0]
    n16 = S // CMP_STRIDE
    xf_ref[...] = x_ref[...].astype(F32)
    top = jnp.zeros((n16, HEAD_DIM), F32)
    bot = jnp.zeros((n16, HEAD_DIM), F32)
    for j in range(CMP_STRIDE):
        xj = xf_ref[pl.ds(j, n16, stride=CMP_STRIDE), :]
        lo, hi = j * HEAD_DIM, (CMP_STRIDE + j) * HEAD_DIM
        top = top + jnp.dot((xj + pe_ref[0, :, lo:lo + HEAD_DIM]).astype(BF16),
                            w1_ref[0, lo:lo + HEAD_DIM, :], preferred_element_type=F32)
        bot = bot + jnp.dot((xj + pe_ref[0, :, hi:hi + HEAD_DIM]).astype(BF16),
                            w1_ref[0, hi:hi + HEAD_DIM, :], preferred_element_type=F32)
    pre = top + pltpu.roll(bot, n16 - 1, 0) + b1_ref[0]
    h = _gelu_tanh(pre)
    o = jnp.dot(h.astype(BF16), w2_ref[0], preferred_element_type=F32) + b2_ref[0]
    o_ref[0, 0] = o.astype(o_ref.dtype)


def _compress(proj, pe, w1, b1, w2, b2, B, S):
    n_slot = 2 * N_KV_GROUPS
    n16 = S // CMP_STRIDE
    wide = CMP_BLOCK * HEAD_DIM
    kv = lambda s, b: (s // N_KV_GROUPS, 0, 0)
    return pl.pallas_call(
        _compress_kernel,
        out_shape=jax.ShapeDtypeStruct((n_slot, B, n16, HEAD_DIM), BF16),
        grid=(n_slot, B),
        in_specs=[pl.BlockSpec((S, HEAD_DIM), lambda s, b: (b, COL_KC // HEAD_DIM + s)),
                  pl.BlockSpec((1, 1, wide), kv),
                  pl.BlockSpec((1, wide, HEAD_DIM), kv),
                  pl.BlockSpec((1, 1, HEAD_DIM), kv),
                  pl.BlockSpec((1, HEAD_DIM, HEAD_DIM), kv),
                  pl.BlockSpec((1, 1, HEAD_DIM), kv)],
        out_specs=pl.BlockSpec((1, 1, n16, HEAD_DIM), lambda s, b: (s, b, 0, 0)),
        scratch_shapes=[pltpu.VMEM((S, HEAD_DIM), F32)],
        compiler_params=_cparams(("arbitrary", "arbitrary")),
        name="compress",
    )(proj, pe, w1, b1, w2, b2)


LOG2E = 1.4426950408889634
KAUG = 2 * HEAD_DIM
ROW_ALIBI = 0
ROW_SLOT = 16
SLC_GROUP = 8
SLC_TAIL = 4
assert 2 * SLC_GROUP == 16 and SLC_TAIL == 4
GROUP_W = HEADS_PER_GROUP * Q_BLOCK
WIN_TILES = WINDOW // Q_BLOCK
WIN_KEYS = WINDOW + Q_BLOCK


def _window_pos_cols():
    rel = np.arange(WIN_KEYS) - WINDOW
    a = (rel // SEL_BLOCK) * SEL_BLOCK
    r = rel - a
    cols = np.zeros((WIN_KEYS, LANES), np.float32)
    cols[:, ROW_ALIBI:ROW_ALIBI + 3] = a[:, None]
    cols[:, ROW_ALIBI + 3:ROW_ALIBI + 6] = r[:, None]
    return jnp.asarray(cols, BF16)


def _alibi_rows():
    s2 = jnp.asarray(_ALIBI * np.float32(LOG2E), F32)
    hi = s2.astype(BF16)
    r1 = s2 - hi.astype(F32)
    mid = r1.astype(BF16)
    lo = (r1 - mid.astype(F32)).astype(BF16)
    trip = jnp.stack([hi, mid, lo, hi, mid, lo], axis=1)
    rows = jnp.pad(trip, ((0, 0), (0, 16 - 6), (0, 0)))
    return jnp.repeat(rows, Q_BLOCK, axis=2)


def _dot_ta(a, b):
    return lax.dot_general(a, b, (((0,), (0,)), ((), ())), preferred_element_type=F32)


def _fill_qaug(qaug, q_ref, alibi_ref):
    for h in range(HEADS_PER_GROUP):
        qaug[pl.ds(0, HEAD_DIM), pl.ds(h * Q_BLOCK, Q_BLOCK)] = (
            q_ref[:, h * HEAD_DIM:(h + 1) * HEAD_DIM].T)
    qaug[pl.ds(HEAD_DIM, 16), :] = alibi_ref[0]
    qaug[pl.ds(HEAD_DIM + 16, HEAD_DIM - 16), :] = jnp.zeros((HEAD_DIM - 16, GROUP_W), BF16)


def _pos_cols(a_val, r_val, lane, slot_lane=None):
    base = jnp.where((lane >= ROW_ALIBI + 3) & (lane < ROW_ALIBI + 6), r_val, 0.0)
    if slot_lane is not None:
        base = jnp.where(lane == slot_lane, 1.0, base)
    return jnp.where(lane < ROW_ALIBI + 3, a_val, base)


def _nsa_cmp_branch(c, kc_ref, vc_ref, ocmp_ref, selb_ref, jmin_ref, qaug, n_selb):
    n_vis = c * (Q_BLOCK // CMP_STRIDE) + (Q_BLOCK - CMP_BLOCK) // CMP_STRIDE + 1
    tiles = kc_ref.shape[2] // LANES
    for i in range(1, tiles + 1):
        cond = n_vis > (i - 1) * LANES
        if i < tiles:
            cond = cond & (n_vis <= i * LANES)
        pl.when(cond)(functools.partial(
            _nsa_cmp_rows, c, i * LANES, kc_ref, vc_ref, ocmp_ref, selb_ref, jmin_ref, qaug,
            n_selb))


def _nsa_cmp_rows(c, ncp, kc_ref, vc_ref, ocmp_ref, selb_ref, jmin_ref, qaug, n_selb):
    nsp = selb_ref.shape[0]

    n_i = lax.broadcasted_iota(jnp.int32, (ncp, LANES), 0)
    lane = lax.broadcasted_iota(jnp.int32, (ncp, LANES), 1)
    end_rel = n_i * CMP_STRIDE + (CMP_BLOCK - 1) - c * Q_BLOCK
    a_val = jnp.left_shift(jnp.right_shift(end_rel, SEL_SHIFT), SEL_SHIFT).astype(F32)
    r_val = (end_rel & (SEL_BLOCK - 1)).astype(F32)
    pc = _pos_cols(a_val, r_val, lane).astype(BF16)
    s = jnp.dot(jnp.concatenate([kc_ref[0, 0, pl.ds(0, ncp), :], pc], axis=1), qaug[...],
                preferred_element_type=F32)

    tq = lax.broadcasted_iota(jnp.int32, (1, Q_BLOCK), 1)
    t_row = c * Q_BLOCK + tq
    n_col = lax.broadcasted_iota(jnp.int32, (ncp, Q_BLOCK), 0)
    mask_bias = jnp.where(n_col * CMP_STRIDE + (CMP_BLOCK - 1) <= t_row, 0.0, NEG_INF)
    has_valid = t_row >= CMP_BLOCK - 1

    imp = jnp.zeros((ncp, Q_BLOCK), F32)
    es, rls = [], []
    for h in range(HEADS_PER_GROUP):
        sh = s[:, h * Q_BLOCK:(h + 1) * Q_BLOCK] + mask_bias
        m = jnp.max(sh, axis=0, keepdims=True)
        e = jnp.exp2(sh - m)
        l = jnp.sum(e, axis=0, keepdims=True)
        rl = jnp.where(has_valid, 1.0 / jnp.maximum(l, TINY), 0.0)
        imp = imp + e * rl
        es.append(e.astype(BF16))
        rls.append(rl)
    oT = _dot_ta(vc_ref[0, 0, pl.ds(0, ncp), :], jnp.concatenate(es, axis=1))
    ocmp_ref[...] = oT * jnp.concatenate(rls, axis=1)

    j_i = lax.broadcasted_iota(jnp.int32, (nsp, ncp), 0)
    n_j = lax.broadcasted_iota(jnp.int32, (nsp, ncp), 1)
    lo = j_i * (SEL_BLOCK // CMP_STRIDE) - (CMP_BLOCK // CMP_STRIDE) + 1
    ovl = ((n_j >= lo) & (n_j < lo + N_OVERLAP)).astype(BF16)
    i_hi = imp.astype(BF16)
    r1 = imp - i_hi.astype(F32)
    i_mid = r1.astype(BF16)
    i_lo = (r1 - i_mid.astype(F32)).astype(BF16)
    imp_sel = (jnp.dot(ovl, i_hi, preferred_element_type=F32)
               + jnp.dot(ovl, i_mid, preferred_element_type=F32)
               + jnp.dot(ovl, i_lo, preferred_element_type=F32))

    blk = lax.broadcasted_iota(jnp.int32, (nsp, Q_BLOCK), 0)
    cur = jnp.right_shift(t_row, SEL_SHIFT)
    valid = blk * SEL_BLOCK <= t_row
    forced = (blk == 0) | (blk == cur) | (blk == cur - 1)
    score = jnp.where(valid, imp_sel + jnp.where(forced, FORCE_BONUS, 0.0), -1.0)
    score = jnp.where(blk < n_selb, score, -3.0)
    selb = jnp.full((nsp, Q_BLOCK), NEG_INF, F32)
    for _ in range(min(SEL_TOP_N, n_selb)):
        mx = jnp.max(score, axis=0, keepdims=True)
        first = jnp.min(jnp.where(score == mx, blk, nsp), axis=0, keepdims=True)
        pick = blk == first
        selb = jnp.where(pick, 0.0, selb)
        score = jnp.where(pick, -2.0, score)
    selb_ref[...] = selb
    blk_col = lax.broadcasted_iota(jnp.int32, (nsp, 1), 0)
    any_sel = jnp.max(selb, axis=1, keepdims=True) > 0.5 * NEG_INF
    n_past = (Q_BLOCK // SEL_BLOCK) * c
    jmin_ref[0] = jnp.min(jnp.where(any_sel & (blk_col >= 1) & (blk_col < n_past),
                                    blk_col, n_past))


def _tile_scores(qaug, k_tile, pc):
    kaug = jnp.concatenate([k_tile, pc.astype(BF16)], axis=1)
    return jnp.dot(kaug, qaug[...], preferred_element_type=F32)


def _softmax_tile(st, qaug, k_tile, pc, v_tile, bias, first):
    _softmax_update(st, _tile_scores(qaug, k_tile, pc), v_tile, bias, first)


def _softmax_update(st, s, v_tile, bias, first):
    m_ref, l_ref, acc_ref = st
    if bias is not None:
        s = jnp.concatenate([s[:, h * Q_BLOCK:(h + 1) * Q_BLOCK] + bias
                             for h in range(HEADS_PER_GROUP)], axis=1)
    mx = jnp.max(s, axis=0, keepdims=True)
    if first:
        m_new = mx
    else:
        m_old = m_ref[pl.ds(0, 1), :]
        m_new = jnp.maximum(m_old, mx)
    p = jnp.exp2(s - m_new)
    ps = jnp.sum(p, axis=0, keepdims=True)
    pv = _dot_ta(v_tile, p.astype(BF16))
    if first:
        l_ref[pl.ds(0, 1), :] = ps
        acc_ref[...] = pv
    else:
        alpha = jnp.exp2(m_old - m_new)
        l_ref[pl.ds(0, 1), :] = alpha * l_ref[pl.ds(0, 1), :] + ps
        acc_ref[...] = alpha * acc_ref[...] + pv
    m_ref[pl.ds(0, 1), :] = m_new


def _bias_rows(s, bias):
    return jnp.concatenate([s[:, h * Q_BLOCK:(h + 1) * Q_BLOCK] + bias
                            for h in range(HEADS_PER_GROUP)], axis=1)


def _nsa_kernel(q_ref, kc_ref, vc_ref, ks_ref, vs_ref, kw_ref, vw_ref, gate_ref, alibi_ref,
                wpc_ref, o_ref, qaug, m_ref, l_ref, acc_ref, owin_ref, ocmp_ref, selb_ref,
                jmin_ref, *, n_selb):
    c = pl.program_id(2)
    st = (m_ref, l_ref, acc_ref)
    _fill_qaug(qaug, q_ref, alibi_ref)
    _nsa_cmp_branch(c, kc_ref, vc_ref, ocmp_ref, selb_ref, jmin_ref, qaug, n_selb)
    n_past = (Q_BLOCK // SEL_BLOCK) * c
    jmin = jmin_ref[0]
    n_act = jnp.where(c >= 1, 1, 0) + (n_past - jmin)

    lane = lax.broadcasted_iota(jnp.int32, (SEL_BLOCK, LANES), 1)
    r_val = lax.broadcasted_iota(jnp.int32, (SEL_BLOCK, LANES), 0).astype(F32)
    ki = lax.broadcasted_iota(jnp.int32, (Q_BLOCK, Q_BLOCK), 0)
    qi = lax.broadcasted_iota(jnp.int32, (Q_BLOCK, Q_BLOCK), 1)
    causal_bias = jnp.where(ki <= qi, 0.0, NEG_INF)
    tail_bias = jnp.where(ki > qi, 0.0, NEG_INF)

    def pair_cols(first_block_rel):
        return jnp.concatenate(
            [_pos_cols(float(SEL_BLOCK * (first_block_rel + i)), r_val, lane) for i in range(2)], axis=0)

    def finish():
        return acc_ref[...] / jnp.maximum(l_ref[pl.ds(0, 1), :], TINY)

    n_wt = WIN_TILES
    base = pl.multiple_of(c * Q_BLOCK, Q_BLOCK)

    @pl.when(c >= n_wt)
    def _():
        start = pl.multiple_of((c - n_wt) * Q_BLOCK, Q_BLOCK)
        kaug = jnp.concatenate([kw_ref[pl.ds(start, WIN_KEYS), :], wpc_ref[...]], axis=1)
        s = jnp.dot(kaug, qaug[...], preferred_element_type=F32)
        s_diag = _tile_scores(qaug, ks_ref[pl.ds(base, Q_BLOCK), :], pair_cols(0))
        s = jnp.concatenate([_bias_rows(s[:Q_BLOCK], tail_bias), s[Q_BLOCK:WINDOW],
                             _bias_rows(s[WINDOW:], causal_bias)], axis=0)
        p = jnp.exp2(s - jnp.max(s, axis=0, keepdims=True))
        l = jnp.sum(p, axis=0, keepdims=True)
        pv = _dot_ta(vw_ref[pl.ds(start, WIN_KEYS), :], p.astype(BF16))
        owin_ref[...] = pv / jnp.maximum(l, TINY)
        _softmax_update(st, s_diag, vs_ref[pl.ds(base, Q_BLOCK), :], causal_bias, True)

    @pl.when(c < n_wt)
    def _():
        _softmax_tile(st, qaug, kw_ref[pl.ds(base, Q_BLOCK), :], pair_cols(0),
                      vw_ref[pl.ds(base, Q_BLOCK), :], causal_bias, True)
        for w in range(1, n_wt):
            @pl.when(c - n_wt + w >= 0)
            def _(w=w):
                start = pl.multiple_of((c - n_wt + w) * Q_BLOCK, Q_BLOCK)
                _softmax_tile(st, qaug, kw_ref[pl.ds(start, Q_BLOCK), :],
                              pair_cols(2 * (w - n_wt)), vw_ref[pl.ds(start, Q_BLOCK), :],
                              None, False)
        owin_ref[...] = finish()
        _softmax_tile(st, qaug, ks_ref[pl.ds(base, Q_BLOCK), :], pair_cols(0),
                      vs_ref[pl.ds(base, Q_BLOCK), :], causal_bias, True)

    sub = lax.broadcasted_iota(jnp.int32, (16, GROUP_W), 0)

    def slc_tiles(pos0, n_tiles, per_tile):
        slot_rows = jnp.zeros((16, GROUP_W), F32)
        tiles = []
        for ti in range(n_tiles):
            ks_t, vs_t, pcs = [], [], []
            for k in range(per_tile):
                slot = ti * per_tile + k
                pos = pos0 + slot
                live = pos < n_act
                j = jnp.where(live & (pos > 0), jmin + pos - 1, 0)
                off = pl.multiple_of(j * SEL_BLOCK, SEL_BLOCK)
                ks_t.append(ks_ref[pl.ds(off, SEL_BLOCK), :])
                vs_t.append(vs_ref[pl.ds(off, SEL_BLOCK), :])
                a_val = ((j - 2 * c) * SEL_BLOCK).astype(F32)
                pcs.append(_pos_cols(a_val, r_val, lane, slot_lane=ROW_SLOT + slot))
                rowk = selb_ref[pl.ds(j, 1), :]
                rowk = jnp.where(live, rowk, NEG_INF)
                slot_rows = jnp.where(sub == slot, jnp.tile(rowk, (1, HEADS_PER_GROUP)), slot_rows)
            tiles.append((jnp.concatenate(ks_t, axis=0), jnp.concatenate(pcs, axis=0),
                          jnp.concatenate(vs_t, axis=0)))
        qaug[pl.ds(HEAD_DIM + ROW_SLOT, 16), :] = slot_rows.astype(BF16)
        scores = [_tile_scores(qaug, k_tile, pc) for k_tile, pc, _ in tiles]
        for s, (_, _, v_tile) in zip(scores, tiles):
            _softmax_update(st, s, v_tile, None, False)

    pair = 2 * SLC_GROUP
    rem = n_act & (pair - 1)
    n_pairs = jnp.right_shift(n_act, 4) + jnp.where(rem > SLC_GROUP, 1, 0)
    n_tail = jnp.where(rem > SLC_GROUP, 0, jnp.right_shift(rem + SLC_TAIL - 1, 2))

    def pair_body(i, carry):
        slc_tiles(i * pair, 2, SLC_GROUP)
        return carry

    def tail_body(i, carry):
        slc_tiles(n_pairs * pair + i * SLC_TAIL, 1, SLC_TAIL)
        return carry

    lax.fori_loop(0, n_pairs, pair_body, 0)
    lax.fori_loop(0, n_tail, tail_body, 0)
    o_slcT = finish()

    gT = jax.nn.sigmoid(gate_ref[...].astype(F32).T)
    o_cmpT = ocmp_ref[...]
    o_winT = owin_ref[...]
    outs = []
    for h in range(HEADS_PER_GROUP):
        hs = slice(h * Q_BLOCK, (h + 1) * Q_BLOCK)
        oT = (gT[3 * h:3 * h + 1, :] * o_cmpT[:, hs] + gT[3 * h + 1:3 * h + 2, :] * o_slcT[:, hs]
              + gT[3 * h + 2:3 * h + 3, :] * o_winT[:, hs])
        outs.append(oT.T)
    o_ref[...] = jnp.concatenate(outs, axis=1).astype(o_ref.dtype)


def _nsa(proj, kcvc, alibi, B, S):
    T = B * S
    C = S // Q_BLOCK
    G = N_KV_GROUPS
    ncp = kcvc.shape[2]
    n_selb = S // SEL_BLOCK
    nsp = max(LANES, n_selb)
    gw = HEADS_PER_GROUP * HEAD_DIM
    kvspec = lambda col: pl.BlockSpec((S, HEAD_DIM), lambda b, g, c: (b, col // HEAD_DIM + g))
    return pl.pallas_call(
        functools.partial(_nsa_kernel, n_selb=n_selb),
        out_shape=jax.ShapeDtypeStruct((T, Q_DIM), BF16),
        grid=(B, G, C),
        in_specs=[pl.BlockSpec((Q_BLOCK, gw), lambda b, g, c: (b * C + c, COL_Q // gw + g)),
                  pl.BlockSpec((1, 1, ncp, HEAD_DIM), lambda b, g, c: (g, b, 0, 0)),
                  pl.BlockSpec((1, 1, ncp, HEAD_DIM), lambda b, g, c: (G + g, b, 0, 0)),
                  kvspec(COL_KS), kvspec(COL_VS), kvspec(COL_KW), kvspec(COL_VW),
                  pl.BlockSpec((Q_BLOCK, LANES), lambda b, g, c: (b * C + c, COL_GATE // LANES + g)),
                  pl.BlockSpec((1, 16, GROUP_W), lambda b, g, c: (g, 0, 0)),
                  pl.BlockSpec((WIN_KEYS, LANES), lambda b, g, c: (0, 0))],
        out_specs=pl.BlockSpec((Q_BLOCK, gw), lambda b, g, c: (b * C + c, g)),
        scratch_shapes=[pltpu.VMEM((KAUG, GROUP_W), BF16),
                        pltpu.VMEM((SUBLANES, GROUP_W), F32),
                        pltpu.VMEM((SUBLANES, GROUP_W), F32),
                        pltpu.VMEM((HEAD_DIM, GROUP_W), F32),
                        pltpu.VMEM((HEAD_DIM, GROUP_W), F32),
                        pltpu.VMEM((HEAD_DIM, GROUP_W), F32),
                        pltpu.VMEM((nsp, Q_BLOCK), F32),
                        pltpu.SMEM((1,), jnp.int32)],
        compiler_params=_cparams(("arbitrary", "arbitrary", "arbitrary")),
        name="nsa",
    )(proj, kcvc, kcvc, proj, proj, proj, proj, proj, alibi, _window_pos_cols())


CONV_TS = 512
CONV_HALO = 32
CONV_RC = 64
CONV_CC = 256


def _conv_kernel(a_ref, b_ref, w_ref, cb_ref, g_ref, beta_ref, o_ref, hbuf, ybuf, wrep):
    si = pl.program_id(1)

    @pl.when(si == 0)
    def _():
        hbuf[pl.ds(0, CONV_HALO), :] = jnp.zeros((CONV_HALO, CONV_CH), F32)

    @pl.when(si > 0)
    def _():
        hbuf[pl.ds(0, CONV_HALO), :] = hbuf[pl.ds(CONV_TS, CONV_HALO), :]

    a = a_ref[...].astype(F32)
    bb = b_ref[...].astype(F32)
    hbuf[pl.ds(CONV_HALO, CONV_TS), :] = a * jax.nn.sigmoid(bb)

    for k in range(CONV_WIDTH):
        wrep[pl.ds(k * SUBLANES, SUBLANES), :] = jnp.broadcast_to(w_ref[k:k + 1, :],
                                                                  (SUBLANES, CONV_CH))

    off = CONV_HALO - (CONV_WIDTH - 1)
    groups = CONV_RC // SUBLANES
    for cc in range(CONV_CH // CONV_CC):
        cols = slice(cc * CONV_CC, (cc + 1) * CONV_CC)

        def row_body(r, carry, cols=cols):
            r0 = pl.multiple_of(r * CONV_RC, CONV_RC)
            win = hbuf[pl.ds(r0, CONV_RC + CONV_HALO), cols]
            acc = jnp.zeros((groups, SUBLANES, CONV_CC), F32)
            for res in range(SUBLANES):
                taps = [k for k in range(CONV_WIDTH) if (off + k) % SUBLANES == res]
                span = max(off + k for k in taps) - res + CONV_RC
                shifted = win[res:res + span, :]
                for k in taps:
                    a0 = off + k - res
                    rows = shifted[a0:a0 + CONV_RC, :].reshape(groups, SUBLANES, CONV_CC)
                    acc = acc + rows * wrep[pl.ds(k * SUBLANES, SUBLANES), cols][None]
            ybuf[pl.ds(r0, CONV_RC), cols] = acc.reshape(CONV_RC, CONV_CC)
            return carry

        lax.fori_loop(0, CONV_TS // CONV_RC, row_body, 0)

    y = ybuf[...] + cb_ref[...]
    mu = jnp.mean(y, axis=-1, keepdims=True)
    yc = y - mu
    var = jnp.mean(yc * yc, axis=-1, keepdims=True)
    z = yc * lax.rsqrt(var + LN_EPS) * g_ref[...] + beta_ref[...]
    o_ref[...] = (z * jax.nn.sigmoid(z)).astype(o_ref.dtype)


def _conv(proj, conv_w, conv_b, ln_g, ln_b, B, S):
    T = B * S
    ts = CONV_TS
    assert S % ts == 0
    nS = S // ts
    ca = COL_GLU // CONV_CH
    vec = lambda: pl.BlockSpec((1, CONV_CH), lambda b, s: (0, 0))
    return pl.pallas_call(
        _conv_kernel,
        out_shape=jax.ShapeDtypeStruct((T, CONV_CH), BF16),
        grid=(B, nS),
        in_specs=[pl.BlockSpec((ts, CONV_CH), lambda b, s: (b * nS + s, ca)),
                  pl.BlockSpec((ts, CONV_CH), lambda b, s: (b * nS + s, ca + 1)),
                  pl.BlockSpec((CONV_WIDTH, CONV_CH), lambda b, s: (0, 0)),
                  vec(), vec(), vec()],
        out_specs=pl.BlockSpec((ts, CONV_CH), lambda b, s: (b * nS + s, 0)),
        scratch_shapes=[pltpu.VMEM((CONV_HALO + ts, CONV_CH), F32),
                        pltpu.VMEM((ts, CONV_CH), F32),
                        pltpu.VMEM((CONV_WIDTH * SUBLANES, CONV_CH), F32)],
        compiler_params=_cparams(("arbitrary", "arbitrary")),
        name="conformer_conv",
    )(proj, proj, conv_w, conv_b.reshape(1, -1), ln_g.reshape(1, -1), ln_b.reshape(1, -1))


def _merge_kernel(o_ref, h_ref, wa_ref, wb_ref, ga_ref, gb_ref, out_ref):
    ya = jnp.dot(o_ref[...], wa_ref[...], preferred_element_type=F32)
    yb = jnp.dot(h_ref[...], wb_ref[...], preferred_element_type=F32)
    ga = jax.nn.sigmoid(ga_ref[...].astype(F32))
    gb = jax.nn.sigmoid(gb_ref[...].astype(F32))
    out_ref[...] = (ga * ya + gb * yb).astype(out_ref.dtype)


def _merge(o_nsa, h_conv, wa, wb, proj):
    T = o_nsa.shape[0]
    tm = min(512, T)
    tn = D_MODEL
    nN = D_MODEL // tn
    ga0 = COL_MERGE // tn
    return pl.pallas_call(
        _merge_kernel,
        out_shape=jax.ShapeDtypeStruct((T, D_MODEL), BF16),
        grid=(T // tm, nN),
        in_specs=[pl.BlockSpec((tm, Q_DIM), lambda i, j: (i, 0)),
                  pl.BlockSpec((tm, CONV_CH), lambda i, j: (i, 0)),
                  pl.BlockSpec((Q_DIM, tn), lambda i, j: (0, j)),
                  pl.BlockSpec((CONV_CH, tn), lambda i, j: (0, j)),
                  pl.BlockSpec((tm, tn), lambda i, j: (i, ga0 + j)),
                  pl.BlockSpec((tm, tn), lambda i, j: (i, ga0 + nN + j))],
        out_specs=pl.BlockSpec((tm, tn), lambda i, j: (i, j)),
        compiler_params=_cparams(("arbitrary", "arbitrary")),
        name="merge",
    )(o_nsa, h_conv, wa, wb, proj, proj)


def _layer_norm_rows(y, g, b):
    mu = jnp.mean(y, axis=-1, keepdims=True)
    yc = y - mu
    var = jnp.mean(yc * yc, axis=-1, keepdims=True)
    return yc * lax.rsqrt(var + LN_EPS) * g + b


def _split2(v):
    hi = v.astype(BF16)
    return hi, (v - hi.astype(F32)).astype(BF16)


def _outproj_kernel(mix_ref, w_ref, x_ref, g_ref, b_ref, rw_hi_ref, rw_lo_ref, rb_ref,
                    x1_ref, x1b_ref, ti_ref, tg_ref, rk_ref, cnt_ref):
    m = jnp.dot(mix_ref[...], w_ref[...], preferred_element_type=F32)
    x1 = _layer_norm_rows(DEEPNORM_ALPHA * x_ref[...] + m, g_ref[...], b_ref[...])
    x1_ref[...] = x1
    x1b_ref[...] = x1.astype(BF16)
    x_hi, x_lo = _split2(x1)
    logits = (jnp.dot(x_hi, rw_hi_ref[...], preferred_element_type=F32)
              + jnp.dot(x_hi, rw_lo_ref[...], preferred_element_type=F32)
              + jnp.dot(x_lo, rw_hi_ref[...], preferred_element_type=F32)) + rb_ref[...]
    lane = lax.broadcasted_iota(jnp.int32, logits.shape, 1)
    logits = jnp.where(lane < N_EXPERTS, logits, -jnp.inf)
    ti = jnp.zeros(logits.shape, jnp.int32)
    tv = jnp.zeros(logits.shape, F32)
    top0 = None
    den = jnp.zeros((logits.shape[0], 1), F32)
    picks = []
    for k in range(TOP_K):
        mx = jnp.max(logits, axis=-1, keepdims=True)
        idx = jnp.min(jnp.where(logits == mx, lane, LANES), axis=-1, keepdims=True)
        if top0 is None:
            top0 = mx
        e = jnp.exp(mx - top0)
        den = den + e
        ti = jnp.where(lane == k, idx, ti)
        tv = jnp.where(lane == k, e, tv)
        picks.append(lane == idx)
        logits = jnp.where(picks[-1], -jnp.inf, logits)
    ti_ref[...] = ti
    tg_ref[...] = tv / den

    @pl.when(pl.program_id(0) == 0)
    def _():
        cnt_ref[...] = jnp.zeros(cnt_ref.shape, F32)

    tm = logits.shape[0]
    chosen = functools.reduce(jnp.logical_or, picks).astype(BF16)
    tri = (lax.broadcasted_iota(jnp.int32, (tm, tm), 1)
           <= lax.broadcasted_iota(jnp.int32, (tm, tm), 0)).astype(BF16)
    upto = jnp.dot(tri, chosen, preferred_element_type=F32) + cnt_ref[pl.ds(0, 1), :]
    rk = jnp.zeros(logits.shape, jnp.int32)
    for k in range(TOP_K):
        r_k = jnp.sum(jnp.where(picks[k], upto, 0.0), axis=-1, keepdims=True) - 1.0
        rk = jnp.where(lane == k, r_k.astype(jnp.int32), rk)
    rk_ref[...] = rk
    cnt_ref[...] = jnp.broadcast_to(upto[tm - 1:tm, :], cnt_ref.shape)


def _outproj(mix, w_out, x2d, g, b, rw_hi, rw_lo, rb):
    T = mix.shape[0]
    tm = min(256, T)
    full = lambda shape: pl.BlockSpec(shape, lambda i: (0, 0))
    rowb = lambda w: pl.BlockSpec((tm, w), lambda i: (i, 0))
    return pl.pallas_call(
        _outproj_kernel,
        out_shape=(jax.ShapeDtypeStruct((T, D_MODEL), F32),
                   jax.ShapeDtypeStruct((T, D_MODEL), BF16),
                   jax.ShapeDtypeStruct((T, LANES), jnp.int32),
                   jax.ShapeDtypeStruct((T, LANES), F32),
                   jax.ShapeDtypeStruct((T, LANES), jnp.int32),
                   jax.ShapeDtypeStruct((SUBLANES, LANES), F32)),
        grid=(T // tm,),
        in_specs=[rowb(D_MODEL), full((D_MODEL, D_MODEL)), rowb(D_MODEL),
                  full((1, D_MODEL)), full((1, D_MODEL)),
                  full((D_MODEL, LANES)), full((D_MODEL, LANES)), full((1, LANES))],
        out_specs=(rowb(D_MODEL), rowb(D_MODEL), rowb(LANES), rowb(LANES), rowb(LANES),
                   full((SUBLANES, LANES))),
        compiler_params=_cparams(("arbitrary",)),
        name="outproj_ln_router",
    )(mix, w_out, x2d, g, b, rw_hi, rw_lo, rb)


def _moe_up_kernel(be_ref, nused_ref, x_ref, wg_ref, wu_ref, bg_ref, bu_ref, *rest, base):
    h_ref, wgb_ref, wub_ref = rest[-3:]
    local = pl.program_id(1)
    r = base + local
    prev = be_ref[jnp.maximum(r - 1, 0)]
    fresh = (local == 0) | (be_ref[r] != prev)

    @pl.when(fresh)
    def _():
        wgb_ref[...] = wg_ref[0].astype(BF16)
        wub_ref[...] = wu_ref[0].astype(BF16)

    @pl.when(r < nused_ref[0])
    def _():
        x = x_ref[...]
        gt = jnp.dot(x, wgb_ref[...], preferred_element_type=F32) + bg_ref[0]
        up = jnp.dot(x, wub_ref[...], preferred_element_type=F32) + bu_ref[0]
        gt = jnp.minimum(gt, SWIGLU_LIMIT)
        up = jnp.clip(up, -SWIGLU_LIMIT, SWIGLU_LIMIT)
        h = gt * jax.nn.sigmoid(SWIGLU_ALPHA * gt) * (up + 1.0)
        h_ref[...] = h.astype(h_ref.dtype)


def _moe_row_maps(nblk):
    rd = lambda r, nu: jnp.minimum(r, nu[0] - 1)
    wr = lambda r, nu: jnp.where(r < nu[0], r, nblk)
    return rd, wr


def _moe_up(blk_e, n_used, xs_seg, seg, nblk, hbuf, w_gate, w_up, b_gate, b_up):
    bm, tf = MOE_BLOCK, MOE_UP_TILE
    seg_blk = xs_seg.shape[0] // bm
    base = seg * seg_blk
    rd = lambda r, nu: jnp.clip(nu[0] - 1 - base, 0, r)
    wr = lambda r, nu: jnp.where(base + r < nu[0], base + r, nblk)
    wspec = pl.BlockSpec((1, D_MODEL, tf), lambda f, r, be, nu: (be[base + r], 0, f))
    bspec = pl.BlockSpec((1, 1, tf), lambda f, r, be, nu: (be[base + r], 0, f))
    in_specs = [pl.BlockSpec((bm, D_MODEL), lambda f, r, be, nu: (rd(r, nu), 0)),
                wspec, wspec, bspec, bspec]
    args = [blk_e, n_used, xs_seg, w_gate, w_up, b_gate.reshape(N_EXPERTS, 1, D_FF),
            b_up.reshape(N_EXPERTS, 1, D_FF)]
    aliases = {}
    if hbuf is not None:
        in_specs.append(pl.BlockSpec(memory_space=pl.ANY))
        aliases = {len(args): 0}
        args.append(hbuf)
    grid_spec = pltpu.PrefetchScalarGridSpec(
        num_scalar_prefetch=2,
        grid=(D_FF // tf, seg_blk),
        in_specs=in_specs,
        out_specs=pl.BlockSpec((bm, tf), lambda f, r, be, nu: (wr(r, nu), f)),
        scratch_shapes=[pltpu.VMEM((D_MODEL, tf), BF16), pltpu.VMEM((D_MODEL, tf), BF16)],
    )
    return pl.pallas_call(
        functools.partial(_moe_up_kernel, base=base),
        out_shape=jax.ShapeDtypeStruct(((nblk + 1) * bm, D_FF), BF16),
        grid_spec=grid_spec,
        input_output_aliases=aliases,
        compiler_params=_cparams(("arbitrary", "arbitrary")),
        name="moe_up",
    )(*args)


def _moe_down_kernel(be_ref, nused_ref, h_ref, wd_ref, bd_ref, o_ref, wdb_ref):
    r = pl.program_id(0)
    prev = be_ref[jnp.maximum(r - 1, 0)]
    fresh = (r == 0) | (be_ref[r] != prev)

    @pl.when(fresh)
    def _():
        wdb_ref[...] = wd_ref[0].astype(BF16)

    @pl.when(r < nused_ref[0])
    def _():
        y = jnp.dot(h_ref[...], wdb_ref[...], preferred_element_type=F32) + bd_ref[0]
        o_ref[...] = y.astype(o_ref.dtype)


def _moe_down(blk_e, n_used, h, w_down, b_down):
    bm = MOE_BLOCK
    nblk = h.shape[0] // bm - 1
    rd, wr = _moe_row_maps(nblk)
    grid_spec = pltpu.PrefetchScalarGridSpec(
        num_scalar_prefetch=2,
        grid=(nblk,),
        in_specs=[pl.BlockSpec((bm, D_FF), lambda r, be, nu: (rd(r, nu), 0)),
                  pl.BlockSpec((1, D_FF, D_MODEL), lambda r, be, nu: (be[r], 0, 0)),
                  pl.BlockSpec((1, 1, D_MODEL), lambda r, be, nu: (be[r], 0, 0))],
        out_specs=pl.BlockSpec((bm, D_MODEL), lambda r, be, nu: (wr(r, nu), 0)),
        scratch_shapes=[pltpu.VMEM((D_FF, D_MODEL), BF16)],
    )
    return pl.pallas_call(
        _moe_down_kernel,
        out_shape=jax.ShapeDtypeStruct(((nblk + 1) * bm, D_MODEL), BF16),
        grid_spec=grid_spec,
        compiler_params=_cparams(("arbitrary",)),
        name="moe_down",
    )(blk_e, n_used, h, w_down, b_down.reshape(N_EXPERTS, 1, D_MODEL))


def _final_kernel(x1_ref, *refs):
    y_refs, (tg_ref, g_ref, b_ref, o_ref) = refs[:TOP_K], refs[TOP_K:]
    tg = tg_ref[...]
    f = tg[:, 0:1] * y_refs[0][...].astype(F32)
    for k in range(1, TOP_K):
        f = f + tg[:, k:k + 1] * y_refs[k][...].astype(F32)
    o_ref[...] = _layer_norm_rows(DEEPNORM_ALPHA * x1_ref[...] + f, g_ref[...], b_ref[...])


def _final(x1, ys, tg, g, b):
    T = x1.shape[0]
    tm = min(512, T)
    rows = pl.BlockSpec((tm, D_MODEL), lambda i: (i, 0))
    vec = pl.BlockSpec((1, D_MODEL), lambda i: (0, 0))
    return pl.pallas_call(
        _final_kernel,
        out_shape=jax.ShapeDtypeStruct((T, D_MODEL), F32),
        grid=(T // tm,),
        in_specs=[rows] + [rows] * TOP_K + [pl.BlockSpec((tm, LANES), lambda i: (i, 0)), vec, vec],
        out_specs=rows,
        compiler_params=_cparams(("arbitrary",)),
        name="combine_ln",
    )(x1, *ys, tg, g, b)


_W_IN_SPLITS = np.cumsum([Q_DIM] + [KV_DIM] * 6 + [NSA_GATE_DIM, 2 * CONV_CH, 2 * D_MODEL])
IN_DIM = int(_W_IN_SPLITS[-1])
WPREP_ROWS = 256


def _wprep_kernel(w_ref, o_ref):
    sp = _W_IN_SPLITS
    o_ref[:, COL_Q:COL_Q + Q_DIM] = (
        w_ref[0, :, 0:sp[0]] * (LOG2E / math.sqrt(HEAD_DIM))).astype(BF16)
    o_ref[:, COL_GLU:COL_GLU + 2 * CONV_CH] = w_ref[0, :, sp[7]:sp[8]].astype(BF16)
    o_ref[:, COL_MERGE:COL_MERGE + 2 * D_MODEL] = w_ref[0, :, sp[8]:sp[9]].astype(BF16)
    o_ref[:, COL_KC:COL_KC + 6 * KV_DIM] = w_ref[0, :, sp[0]:sp[6]].astype(BF16)
    per_g = HEADS_PER_GROUP * 3
    zeros = jnp.zeros((w_ref.shape[1], LANES - per_g), BF16)
    for g in range(N_KV_GROUPS):
        gate = w_ref[0, :, sp[6] + g * per_g:sp[6] + (g + 1) * per_g].astype(BF16)
        o_ref[:, COL_GATE + g * LANES:COL_GATE + (g + 1) * LANES] = jnp.concatenate(
            [gate, zeros], axis=1)


def _prep_w_in(w_in3):
    D = w_in3.shape[1]
    rows = min(WPREP_ROWS, D)
    return pl.pallas_call(
        _wprep_kernel,
        out_shape=jax.ShapeDtypeStruct((D, PROJ_W), BF16),
        grid=(D // rows,),
        in_specs=[pl.BlockSpec((1, rows, IN_DIM), lambda i: (0, i, 0))],
        out_specs=pl.BlockSpec((rows, PROJ_W), lambda i: (i, 0)),
        compiler_params=_cparams(("arbitrary",)),
        name="w_in_layout",
    )(w_in3)


def _mixer(x2d, B, S, w_in, cmp_pe, cmp_w1, cmp_b1, cmp_w2, cmp_b2, w_nsa_proj,
           conv_w, conv_b, conv_ln_g, conv_ln_b, w_conv_proj):
    T = B * S
    G = N_KV_GROUPS
    proj = _inproj(x2d, _prep_w_in(w_in))

    kcvc = _compress(proj, cmp_pe.reshape(2, 1, CMP_BLOCK * HEAD_DIM),
                     cmp_w1.astype(BF16), cmp_b1.reshape(2, 1, HEAD_DIM),
                     cmp_w2.astype(BF16), cmp_b2.reshape(2, 1, HEAD_DIM), B, S)

    o_nsa = _nsa(proj, kcvc, _alibi_rows().astype(BF16), B, S)

    h_conv = _conv(proj, conv_w, conv_b, conv_ln_g, conv_ln_b, B, S)
    return _merge(o_nsa, h_conv, w_nsa_proj.astype(BF16), w_conv_proj.astype(BF16), proj)


def _moe(x1b, top_i, top_rank, expert_counts, w_gate, b_gate, w_up, b_up, w_down, b_down):
    T = x1b.shape[0]
    A = T * TOP_K
    bm = MOE_BLOCK
    flat_e = top_i[:, :TOP_K].reshape(A)
    rank = top_rank[:, :TOP_K].reshape(A)
    counts = expert_counts[0, :N_EXPERTS].astype(jnp.int32)
    padded = (counts + bm - 1) // bm * bm
    pad_end = jnp.cumsum(padded)
    pad_start = pad_end - padded
    dest = pad_start[flat_e] + rank
    P = -(-A // bm) * bm + N_EXPERTS * bm
    nblk = P // bm
    flat_tok = jnp.arange(A, dtype=jnp.int32) // TOP_K
    buf_tok = (jnp.arange(P, dtype=jnp.int32) % T).at[dest].set(flat_tok)
    blk_start = jnp.arange(nblk, dtype=jnp.int32) * bm
    blk_e = jnp.minimum(jnp.sum(blk_start[:, None] >= pad_end[None, :], axis=1),
                        N_EXPERTS - 1).astype(jnp.int32)
    n_used = (pad_end[-1] // bm).astype(jnp.int32).reshape(1)
    seg_rows = P // MOE_SEGMENTS
    assert seg_rows % bm == 0
    h = None
    for seg in range(MOE_SEGMENTS):
        xs_seg = x1b[buf_tok[seg * seg_rows:(seg + 1) * seg_rows]]
        h = _moe_up(blk_e, n_used, xs_seg, seg, nblk, h, w_gate, w_up, b_gate, b_up)
    out = _moe_down(blk_e, n_used, h, w_down, b_down)
    dest_k = dest.reshape(T, TOP_K)
    return [out[dest_k[:, k]] for k in range(TOP_K)]


def kernel(x, w_in, cmp_pe, cmp_w1, cmp_b1, cmp_w2, cmp_b2, w_nsa_proj, conv_w, conv_b, conv_ln_g, conv_ln_b, w_conv_proj, w_out, ln1_g, ln1_b, router_w, router_b, w_gate, b_gate, w_up, b_up, w_down, b_down, ln2_g, ln2_b):
    B, S, D = x.shape
    T = B * S
    x2d = x.reshape(T, D)
    for l in range(DEPTH):
        mix = _mixer(x2d, B, S, w_in[l:l + 1], cmp_pe[l], cmp_w1[l], cmp_b1[l], cmp_w2[l], cmp_b2[l],
                     w_nsa_proj[l], conv_w[l], conv_b[l], conv_ln_g[l], conv_ln_b[l],
                     w_conv_proj[l])
        rw = jnp.pad(router_w[l], ((0, 0), (0, LANES - N_EXPERTS)))
        rw_hi = rw.astype(BF16)
        rw_lo = (rw - rw_hi.astype(F32)).astype(BF16)
        rb = jnp.pad(router_b[l], (0, LANES - N_EXPERTS)).reshape(1, LANES)
        x1, x1b, top_i, top_g, top_rank, expert_counts = _outproj(
            mix, w_out[l].astype(BF16), x2d, ln1_g[l].reshape(1, D), ln1_b[l].reshape(1, D),
            rw_hi, rw_lo, rb)
        y4 = _moe(x1b, top_i, top_rank, expert_counts, w_gate[l], b_gate[l], w_up[l], b_up[l],
                  w_down[l], b_down[l])
        x2d = _final(x1, y4, top_g, ln2_g[l].reshape(1, D), ln2_b[l].reshape(1, D))
    return x2d.reshape(B, S, D)
```

```python
import functools
import math

import numpy as np
import jax
import jax.numpy as jnp
from jax import lax
from jax.experimental import pallas as pl
from jax.experimental.pallas import tpu as pltpu

D_MODEL = 2048
N_HEADS = 16
N_KV_GROUPS = 2
HEADS_PER_GROUP = N_HEADS // N_KV_GROUPS
HEAD_DIM = 128
CMP_BLOCK = 32
CMP_STRIDE = 16
SEL_BLOCK = 64
SEL_SHIFT = 6
SEL_TOP_N = 16
WINDOW = 512
Q_BLOCK = 128
N_OVERLAP = (SEL_BLOCK + CMP_BLOCK) // CMP_STRIDE - 1
FORCE_BONUS = 1.0e4
CONV_CH = D_MODEL // 2
CONV_WIDTH = 31
N_EXPERTS = 32
TOP_K = 4
D_FF = D_MODEL
SWIGLU_LIMIT = 7.0
SWIGLU_ALPHA = 1.702
LN_EPS = 1e-5
DEPTH = 1
DEEPNORM_ALPHA = (2 * DEPTH) ** 0.25
NEG_INF = -1e30
TINY = 1e-30

Q_DIM = N_HEADS * HEAD_DIM
KV_DIM = N_KV_GROUPS * HEAD_DIM
NSA_GATE_DIM = N_HEADS * 3

LANES = 128
SUBLANES = 8
VMEM_LIMIT = 56 * 1024 * 1024

COL_Q = 0
COL_GLU = COL_Q + Q_DIM
COL_MERGE = COL_GLU + 2 * CONV_CH
COL_KC = COL_MERGE + 2 * D_MODEL
COL_VC = COL_KC + KV_DIM
COL_KS = COL_VC + KV_DIM
COL_VS = COL_KS + KV_DIM
COL_KW = COL_VS + KV_DIM
COL_VW = COL_KW + KV_DIM
COL_GATE = COL_VW + KV_DIM
PROJ_W = COL_GATE + N_KV_GROUPS * LANES

BF16 = jnp.bfloat16
F32 = jnp.float32

MOE_BLOCK = 512
MOE_UP_TILE = 1024
MOE_SEGMENTS = 8

_ALIBI = np.exp2(-8.0 * np.arange(1, N_HEADS + 1, dtype=np.float32) / N_HEADS).astype(np.float32)
_ALIBI = _ALIBI.reshape(N_KV_GROUPS, HEADS_PER_GROUP)


def _cparams(sem, vmem=VMEM_LIMIT):
    return pltpu.CompilerParams(dimension_semantics=sem, vmem_limit_bytes=vmem)


def _inproj_kernel(x_ref, w_ref, o_ref, xb_ref):
    @pl.when(pl.program_id(1) == 0)
    def _():
        xb_ref[...] = x_ref[...].astype(BF16)

    o_ref[...] = jnp.dot(xb_ref[...], w_ref[...],
                         preferred_element_type=F32).astype(o_ref.dtype)


def _inproj(x2d, w_p):
    T, D = x2d.shape
    N = w_p.shape[1]
    tm = min(512, T)
    tn = N // 3
    assert T % tm == 0 and N % tn == 0 and tn % (2 * LANES) == 0
    return pl.pallas_call(
        _inproj_kernel,
        out_shape=jax.ShapeDtypeStruct((T, N), BF16),
        grid=(T // tm, N // tn),
        in_specs=[pl.BlockSpec((tm, D), lambda i, j: (i, 0)),
                  pl.BlockSpec((D, tn), lambda i, j: (0, j))],
        out_specs=pl.BlockSpec((tm, tn), lambda i, j: (i, j)),
        scratch_shapes=[pltpu.VMEM((tm, D), BF16)],
        compiler_params=_cparams(("arbitrary", "arbitrary")),
        name="inproj",
    )(x2d, w_p)


def _gelu_tanh(x):
    c = math.sqrt(2.0 / math.pi)
    return 0.5 * x * (1.0 + jnp.tanh(c * (x + 0.044715 * (x * x * x))))


def _compress_kernel(x_ref, pe_ref, w1_ref, b1_ref, w2_ref, b2_ref, o_ref, xf_ref):
    S = x_ref.shape[0]
    n16 = S // CMP_STRIDE
    xf_ref[...] = x_ref[...].astype(F32)
    top = jnp.zeros((n16, HEAD_DIM), F32)
    bot = jnp.zeros((n16, HEAD_DIM), F32)
    for j in range(CMP_STRIDE):
        xj = xf_ref[pl.ds(j, n16, stride=CMP_STRIDE), :]
        lo, hi = j * HEAD_DIM, (CMP_STRIDE + j) * HEAD_DIM
        top = top + jnp.dot((xj + pe_ref[0, :, lo:lo + HEAD_DIM]).astype(BF16),
                            w1_ref[0, lo:lo + HEAD_DIM, :], preferred_element_type=F32)
        bot = bot + jnp.dot((xj + pe_ref[0, :, hi:hi + HEAD_DIM]).astype(BF16),
                            w1_ref[0, hi:hi + HEAD_DIM, :], preferred_element_type=F32)
    pre = top + pltpu.roll(bot, n16 - 1, 0) + b1_ref[0]
    h = _gelu_tanh(pre)
    o = jnp.dot(h.astype(BF16), w2_ref[0], preferred_element_type=F32) + b2_ref[0]
    o_ref[0, 0] = o.astype(o_ref.dtype)


def _compress(proj, pe, w1, b1, w2, b2, B, S):
    n_slot = 2 * N_KV_GROUPS
    n16 = S // CMP_STRIDE
    wide = CMP_BLOCK * HEAD_DIM
    kv = lambda s, b: (s // N_KV_GROUPS, 0, 0)
    return pl.pallas_call(
        _compress_kernel,
        out_shape=jax.ShapeDtypeStruct((n_slot, B, n16, HEAD_DIM), BF16),
        grid=(n_slot, B),
        in_specs=[pl.BlockSpec((S, HEAD_DIM), lambda s, b: (b, COL_KC // HEAD_DIM + s)),
                  pl.BlockSpec((1, 1, wide), kv),
                  pl.BlockSpec((1, wide, HEAD_DIM), kv),
                  pl.BlockSpec((1, 1, HEAD_DIM), kv),
                  pl.BlockSpec((1, HEAD_DIM, HEAD_DIM), kv),
                  pl.BlockSpec((1, 1, HEAD_DIM), kv)],
        out_specs=pl.BlockSpec((1, 1, n16, HEAD_DIM), lambda s, b: (s, b, 0, 0)),
        scratch_shapes=[pltpu.VMEM((S, HEAD_DIM), F32)],
        compiler_params=_cparams(("arbitrary", "arbitrary")),
        name="compress",
    )(proj, pe, w1, b1, w2, b2)


LOG2E = 1.4426950408889634
KAUG = 2 * HEAD_DIM
ROW_ALIBI = 0
ROW_SLOT = 16
SLC_GROUP = 8
SLC_TAIL = 4
assert 2 * SLC_GROUP == 16 and SLC_TAIL == 4
GROUP_W = HEADS_PER_GROUP * Q_BLOCK
WIN_TILES = WINDOW // Q_BLOCK
WIN_KEYS = WINDOW + Q_BLOCK


def _window_pos_cols():
    rel = np.arange(WIN_KEYS) - WINDOW
    a = (rel // SEL_BLOCK) * SEL_BLOCK
    r = rel - a
    cols = np.zeros((WIN_KEYS, LANES), np.float32)
    cols[:, ROW_ALIBI:ROW_ALIBI + 3] = a[:, None]
    cols[:, ROW_ALIBI + 3:ROW_ALIBI + 6] = r[:, None]
    return jnp.asarray(cols, BF16)


def _alibi_rows():
    s2 = jnp.asarray(_ALIBI * np.float32(LOG2E), F32)
    hi = s2.astype(BF16)
    r1 = s2 - hi.astype(F32)
    mid = r1.astype(BF16)
    lo = (r1 - mid.astype(F32)).astype(BF16)
    trip = jnp.stack([hi, mid, lo, hi, mid, lo], axis=1)
    rows = jnp.pad(trip, ((0, 0), (0, 16 - 6), (0, 0)))
    return jnp.repeat(rows, Q_BLOCK, axis=2)


def _dot_ta(a, b):
    return lax.dot_general(a, b, (((0,), (0,)), ((), ())), preferred_element_type=F32)


def _fill_qaug(qaug, q_ref, alibi_ref):
    for h in range(HEADS_PER_GROUP):
        qaug[pl.ds(0, HEAD_DIM), pl.ds(h * Q_BLOCK, Q_BLOCK)] = (
            q_ref[:, h * HEAD_DIM:(h + 1) * HEAD_DIM].T)
    qaug[pl.ds(HEAD_DIM, 16), :] = alibi_ref[0]
    qaug[pl.ds(HEAD_DIM + 16, HEAD_DIM - 16), :] = jnp.zeros((HEAD_DIM - 16, GROUP_W), BF16)


def _pos_cols(a_val, r_val, lane, slot_lane=None):
    base = jnp.where((lane >= ROW_ALIBI + 3) & (lane < ROW_ALIBI + 6), r_val, 0.0)
    if slot_lane is not None:
        base = jnp.where(lane == slot_lane, 1.0, base)
    return jnp.where(lane < ROW_ALIBI + 3, a_val, base)


def _nsa_cmp_branch(c, kc_ref, vc_ref, ocmp_ref, selb_ref, jmin_ref, qaug, n_selb):
    n_vis = c * (Q_BLOCK // CMP_STRIDE) + (Q_BLOCK - CMP_BLOCK) // CMP_STRIDE + 1
    tiles = kc_ref.shape[2] // LANES
    for i in range(1, tiles + 1):
        cond = n_vis > (i - 1) * LANES
        if i < tiles:
            cond = cond & (n_vis <= i * LANES)
        pl.when(cond)(functools.partial(
            _nsa_cmp_rows, c, i * LANES, kc_ref, vc_ref, ocmp_ref, selb_ref, jmin_ref, qaug,
            n_selb))


def _nsa_cmp_rows(c, ncp, kc_ref, vc_ref, ocmp_ref, selb_ref, jmin_ref, qaug, n_selb):
    nsp = selb_ref.shape[0]

    n_i = lax.broadcasted_iota(jnp.int32, (ncp, LANES), 0)
    lane = lax.broadcasted_iota(jnp.int32, (ncp, LANES), 1)
    end_rel = n_i * CMP_STRIDE + (CMP_BLOCK - 1) - c * Q_BLOCK
    a_val = jnp.left_shift(jnp.right_shift(end_rel, SEL_SHIFT), SEL_SHIFT).astype(F32)
    r_val = (end_rel & (SEL_BLOCK - 1)).astype(F32)
    pc = _pos_cols(a_val, r_val, lane).astype(BF16)
    s = jnp.dot(jnp.concatenate([kc_ref[0, 0, pl.ds(0, ncp), :], pc], axis=1), qaug[...],
                preferred_element_type=F32)

    tq = lax.broadcasted_iota(jnp.int32, (1, Q_BLOCK), 1)
    t_row = c * Q_BLOCK + tq
    n_col = lax.broadcasted_iota(jnp.int32, (ncp, Q_BLOCK), 0)
    mask_bias = jnp.where(n_col * CMP_STRIDE + (CMP_BLOCK - 1) <= t_row, 0.0, NEG_INF)
    has_valid = t_row >= CMP_BLOCK - 1

    imp = jnp.zeros((ncp, Q_BLOCK), F32)
    es, rls = [], []
    for h in range(HEADS_PER_GROUP):
        sh = s[:, h * Q_BLOCK:(h + 1) * Q_BLOCK] + mask_bias
        m = jnp.max(sh, axis=0, keepdims=True)
        e = jnp.exp2(sh - m)
        l = jnp.sum(e, axis=0, keepdims=True)
        rl = jnp.where(has_valid, 1.0 / jnp.maximum(l, TINY), 0.0)
        imp = imp + e * rl
        es.append(e.astype(BF16))
        rls.append(rl)
    oT = _dot_ta(vc_ref[0, 0, pl.ds(0, ncp), :], jnp.concatenate(es, axis=1))
    ocmp_ref[...] = oT * jnp.concatenate(rls, axis=1)

    j_i = lax.broadcasted_iota(jnp.int32, (nsp, ncp), 0)
    n_j = lax.broadcasted_iota(jnp.int32, (nsp, ncp), 1)
    lo = j_i * (SEL_BLOCK // CMP_STRIDE) - (CMP_BLOCK // CMP_STRIDE) + 1
    ovl = ((n_j >= lo) & (n_j < lo + N_OVERLAP)).astype(BF16)
    i_hi = imp.astype(BF16)
    r1 = imp - i_hi.astype(F32)
    i_mid = r1.astype(BF16)
    i_lo = (r1 - i_mid.astype(F32)).astype(BF16)
    imp_sel = (jnp.dot(ovl, i_hi, preferred_element_type=F32)
               + jnp.dot(ovl, i_mid, preferred_element_type=F32)
               + jnp.dot(ovl, i_lo, preferred_element_type=F32))

    blk = lax.broadcasted_iota(jnp.int32, (nsp, Q_BLOCK), 0)
    cur = jnp.right_shift(t_row, SEL_SHIFT)
    valid = blk * SEL_BLOCK <= t_row
    forced = (blk == 0) | (blk == cur) | (blk == cur - 1)
    score = jnp.where(valid, imp_sel + jnp.where(forced, FORCE_BONUS, 0.0), -1.0)
    score = jnp.where(blk < n_selb, score, -3.0)
    selb = jnp.full((nsp, Q_BLOCK), NEG_INF, F32)
    for _ in range(min(SEL_TOP_N, n_selb)):
        mx = jnp.max(score, axis=0, keepdims=True)
        first = jnp.min(jnp.where(score == mx, blk, nsp), axis=0, keepdims=True)
        pick = blk == first
        selb = jnp.where(pick, 0.0, selb)
        score = jnp.where(pick, -2.0, score)
    selb_ref[...] = selb
    blk_col = lax.broadcasted_iota(jnp.int32, (nsp, 1), 0)
    any_sel = jnp.max(selb, axis=1, keepdims=True) > 0.5 * NEG_INF
    n_past = (Q_BLOCK // SEL_BLOCK) * c
    jmin_ref[0] = jnp.min(jnp.where(any_sel & (blk_col >= 1) & (blk_col < n_past),
                                    blk_col, n_past))


def _tile_scores(qaug, k_tile, pc):
    kaug = jnp.concatenate([k_tile, pc.astype(BF16)], axis=1)
    return jnp.dot(kaug, qaug[...], preferred_element_type=F32)


def _softmax_tile(st, qaug, k_tile, pc, v_tile, bias, first):
    _softmax_update(st, _tile_scores(qaug, k_tile, pc), v_tile, bias, first)


def _softmax_update(st, s, v_tile, bias, first):
    m_ref, l_ref, acc_ref = st
    if bias is not None:
        s = jnp.concatenate([s[:, h * Q_BLOCK:(h + 1) * Q_BLOCK] + bias
                             for h in range(HEADS_PER_GROUP)], axis=1)
    mx = jnp.max(s, axis=0, keepdims=True)
    if first:
        m_new = mx
    else:
        m_old = m_ref[pl.ds(0, 1), :]
        m_new = jnp.maximum(m_old, mx)
    p = jnp.exp2(s - m_new)
    ps = jnp.sum(p, axis=0, keepdims=True)
    pv = _dot_ta(v_tile, p.astype(BF16))
    if first:
        l_ref[pl.ds(0, 1), :] = ps
        acc_ref[...] = pv
    else:
        alpha = jnp.exp2(m_old - m_new)
        l_ref[pl.ds(0, 1), :] = alpha * l_ref[pl.ds(0, 1), :] + ps
        acc_ref[...] = alpha * acc_ref[...] + pv
    m_ref[pl.ds(0, 1), :] = m_new


def _bias_rows(s, bias):
    return jnp.concatenate([s[:, h * Q_BLOCK:(h + 1) * Q_BLOCK] + bias
                            for h in range(HEADS_PER_GROUP)], axis=1)


def _nsa_kernel(q_ref, kc_ref, vc_ref, ks_ref, vs_ref, kw_ref, vw_ref, gate_ref, alibi_ref,
                wpc_ref, o_ref, qaug, m_ref, l_ref, acc_ref, owin_ref, ocmp_ref, selb_ref,
                jmin_ref, *, n_selb):
    c = pl.program_id(2)
    st = (m_ref, l_ref, acc_ref)
    _fill_qaug(qaug, q_ref, alibi_ref)
    _nsa_cmp_branch(c, kc_ref, vc_ref, ocmp_ref, selb_ref, jmin_ref, qaug, n_selb)
    n_past = (Q_BLOCK // SEL_BLOCK) * c
    jmin = jmin_ref[0]
    n_act = jnp.where(c >= 1, 1, 0) + (n_past - jmin)

    lane = lax.broadcasted_iota(jnp.int32, (SEL_BLOCK, LANES), 1)
    r_val = lax.broadcasted_iota(jnp.int32, (SEL_BLOCK, LANES), 0).astype(F32)
    ki = lax.broadcasted_iota(jnp.int32, (Q_BLOCK, Q_BLOCK), 0)
    qi = lax.broadcasted_iota(jnp.int32, (Q_BLOCK, Q_BLOCK), 1)
    causal_bias = jnp.where(ki <= qi, 0.0, NEG_INF)
    tail_bias = jnp.where(ki > qi, 0.0, NEG_INF)

    def pair_cols(first_block_rel):
        return jnp.concatenate(
            [_pos_cols(float(SEL_BLOCK * (first_block_rel + i)), r_val, lane) for i in range(2)], axis=0)

    def finish():
        return acc_ref[...] / jnp.maximum(l_ref[pl.ds(0, 1), :], TINY)

    n_wt = WIN_TILES
    base = pl.multiple_of(c * Q_BLOCK, Q_BLOCK)

    @pl.when(c >= n_wt)
    def _():
        start = pl.multiple_of((c - n_wt) * Q_BLOCK, Q_BLOCK)
        kaug = jnp.concatenate([kw_ref[pl.ds(start, WIN_KEYS), :], wpc_ref[...]], axis=1)
        s = jnp.dot(kaug, qaug[...], preferred_element_type=F32)
        s_diag = _tile_scores(qaug, ks_ref[pl.ds(base, Q_BLOCK), :], pair_cols(0))
        s = jnp.concatenate([_bias_rows(s[:Q_BLOCK], tail_bias), s[Q_BLOCK:WINDOW],
                             _bias_rows(s[WINDOW:], causal_bias)], axis=0)
        p = jnp.exp2(s - jnp.max(s, axis=0, keepdims=True))
        l = jnp.sum(p, axis=0, keepdims=True)
        pv = _dot_ta(vw_ref[pl.ds(start, WIN_KEYS), :], p.astype(BF16))
        owin_ref[...] = pv / jnp.maximum(l, TINY)
        _softmax_update(st, s_diag, vs_ref[pl.ds(base, Q_BLOCK), :], causal_bias, True)

    @pl.when(c < n_wt)
    def _():
        _softmax_tile(st, qaug, kw_ref[pl.ds(base, Q_BLOCK), :], pair_cols(0),
                      vw_ref[pl.ds(base, Q_BLOCK), :], causal_bias, True)
        for w in range(1, n_wt):
            @pl.when(c - n_wt + w >= 0)
            def _(w=w):
                start = pl.multiple_of((c - n_wt + w) * Q_BLOCK, Q_BLOCK)
                _softmax_tile(st, qaug, kw_ref[pl.ds(start, Q_BLOCK), :],
                              pair_cols(2 * (w - n_wt)), vw_ref[pl.ds(start, Q_BLOCK), :],
                              None, False)
        owin_ref[...] = finish()
        _softmax_tile(st, qaug, ks_ref[pl.ds(base, Q_BLOCK), :], pair_cols(0),
                      vs_ref[pl.ds(base, Q_BLOCK), :], causal_bias, True)

    sub = lax.broadcasted_iota(jnp.int32, (16, GROUP_W), 0)

    def slc_tiles(pos0, n_tiles, per_tile):
        slot_rows = jnp.zeros((16, GROUP_W), F32)
        tiles = []
        for ti in range(n_tiles):
            ks_t, vs_t, pcs = [], [], []
            for k in range(per_tile):
                slot = ti * per_tile + k
                pos = pos0 + slot
                live = pos < n_act
                j = jnp.where(live & (pos > 0), jmin + pos - 1, 0)
                off = pl.multiple_of(j * SEL_BLOCK, SEL_BLOCK)
                ks_t.append(ks_ref[pl.ds(off, SEL_BLOCK), :])
                vs_t.append(vs_ref[pl.ds(off, SEL_BLOCK), :])
                a_val = ((j - 2 * c) * SEL_BLOCK).astype(F32)
                pcs.append(_pos_cols(a_val, r_val, lane, slot_lane=ROW_SLOT + slot))
                rowk = selb_ref[pl.ds(j, 1), :]
                rowk = jnp.where(live, rowk, NEG_INF)
                slot_rows = jnp.where(sub == slot, jnp.tile(rowk, (1, HEADS_PER_GROUP)), slot_rows)
            tiles.append((jnp.concatenate(ks_t, axis=0), jnp.concatenate(pcs, axis=0),
                          jnp.concatenate(vs_t, axis=0)))
        qaug[pl.ds(HEAD_DIM + ROW_SLOT, 16), :] = slot_rows.astype(BF16)
        scores = [_tile_scores(qaug, k_tile, pc) for k_tile, pc, _ in tiles]
        for s, (_, _, v_tile) in zip(scores, tiles):
            _softmax_update(st, s, v_tile, None, False)

    pair = 2 * SLC_GROUP
    rem = n_act & (pair - 1)
    n_pairs = jnp.right_shift(n_act, 4) + jnp.where(rem > SLC_GROUP, 1, 0)
    n_tail = jnp.where(rem > SLC_GROUP, 0, jnp.right_shift(rem + SLC_TAIL - 1, 2))

    def pair_body(i, carry):
        slc_tiles(i * pair, 2, SLC_GROUP)
        return carry

    def tail_body(i, carry):
        slc_tiles(n_pairs * pair + i * SLC_TAIL, 1, SLC_TAIL)
        return carry

    lax.fori_loop(0, n_pairs, pair_body, 0)
    lax.fori_loop(0, n_tail, tail_body, 0)
    o_slcT = finish()

    gT = jax.nn.sigmoid(gate_ref[...].astype(F32).T)
    o_cmpT = ocmp_ref[...]
    o_winT = owin_ref[...]
    outs = []
    for h in range(HEADS_PER_GROUP):
        hs = slice(h * Q_BLOCK, (h + 1) * Q_BLOCK)
        oT = (gT[3 * h:3 * h + 1, :] * o_cmpT[:, hs] + gT[3 * h + 1:3 * h + 2, :] * o_slcT[:, hs]
              + gT[3 * h + 2:3 * h + 3, :] * o_winT[:, hs])
        outs.append(oT.T)
    o_ref[...] = jnp.concatenate(outs, axis=1).astype(o_ref.dtype)


def _nsa(proj, kcvc, alibi, B, S):
    T = B * S
    C = S // Q_BLOCK
    G = N_KV_GROUPS
    ncp = kcvc.shape[2]
    n_selb = S // SEL_BLOCK
    nsp = max(LANES, n_selb)
    gw = HEADS_PER_GROUP * HEAD_DIM
    kvspec = lambda col: pl.BlockSpec((S, HEAD_DIM), lambda b, g, c: (b, col // HEAD_DIM + g))
    return pl.pallas_call(
        functools.partial(_nsa_kernel, n_selb=n_selb),
        out_shape=jax.ShapeDtypeStruct((T, Q_DIM), BF16),
        grid=(B, G, C),
        in_specs=[pl.BlockSpec((Q_BLOCK, gw), lambda b, g, c: (b * C + c, COL_Q // gw + g)),
                  pl.BlockSpec((1, 1, ncp, HEAD_DIM), lambda b, g, c: (g, b, 0, 0)),
                  pl.BlockSpec((1, 1, ncp, HEAD_DIM), lambda b, g, c: (G + g, b, 0, 0)),
                  kvspec(COL_KS), kvspec(COL_VS), kvspec(COL_KW), kvspec(COL_VW),
                  pl.BlockSpec((Q_BLOCK, LANES), lambda b, g, c: (b * C + c, COL_GATE // LANES + g)),
                  pl.BlockSpec((1, 16, GROUP_W), lambda b, g, c: (g, 0, 0)),
                  pl.BlockSpec((WIN_KEYS, LANES), lambda b, g, c: (0, 0))],
        out_specs=pl.BlockSpec((Q_BLOCK, gw), lambda b, g, c: (b * C + c, g)),
        scratch_shapes=[pltpu.VMEM((KAUG, GROUP_W), BF16),
                        pltpu.VMEM((SUBLANES, GROUP_W), F32),
                        pltpu.VMEM((SUBLANES, GROUP_W), F32),
                        pltpu.VMEM((HEAD_DIM, GROUP_W), F32),
                        pltpu.VMEM((HEAD_DIM, GROUP_W), F32),
                        pltpu.VMEM((HEAD_DIM, GROUP_W), F32),
                        pltpu.VMEM((nsp, Q_BLOCK), F32),
                        pltpu.SMEM((1,), jnp.int32)],
        compiler_params=_cparams(("arbitrary", "arbitrary", "arbitrary")),
        name="nsa",
    )(proj, kcvc, kcvc, proj, proj, proj, proj, proj, alibi, _window_pos_cols())


CONV_TS = 512
CONV_HALO = 32
CONV_RC = 64
CONV_CC = 256


def _conv_kernel(a_ref, b_ref, w_ref, cb_ref, g_ref, beta_ref, o_ref, hbuf, ybuf, wrep):
    si = pl.program_id(1)

    @pl.when(si == 0)
    def _():
        hbuf[pl.ds(0, CONV_HALO), :] = jnp.zeros((CONV_HALO, CONV_CH), F32)

    @pl.when(si > 0)
    def _():
        hbuf[pl.ds(0, CONV_HALO), :] = hbuf[pl.ds(CONV_TS, CONV_HALO), :]

    a = a_ref[...].astype(F32)
    bb = b_ref[...].astype(F32)
    hbuf[pl.ds(CONV_HALO, CONV_TS), :] = a * jax.nn.sigmoid(bb)

    for k in range(CONV_WIDTH):
        wrep[pl.ds(k * SUBLANES, SUBLANES), :] = jnp.broadcast_to(w_ref[k:k + 1, :],
                                                                  (SUBLANES, CONV_CH))

    off = CONV_HALO - (CONV_WIDTH - 1)
    groups = CONV_RC // SUBLANES
    for cc in range(CONV_CH // CONV_CC):
        cols = slice(cc * CONV_CC, (cc + 1) * CONV_CC)

        def row_body(r, carry, cols=cols):
            r0 = pl.multiple_of(r * CONV_RC, CONV_RC)
            win = hbuf[pl.ds(r0, CONV_RC + CONV_HALO), cols]
            acc = jnp.zeros((groups, SUBLANES, CONV_CC), F32)
            for res in range(SUBLANES):
                taps = [k for k in range(CONV_WIDTH) if (off + k) % SUBLANES == res]
                span = max(off + k for k in taps) - res + CONV_RC
                shifted = win[res:res + span, :]
                for k in taps:
                    a0 = off + k - res
                    rows = shifted[a0:a0 + CONV_RC, :].reshape(groups, SUBLANES, CONV_CC)
                    acc = acc + rows * wrep[pl.ds(k * SUBLANES, SUBLANES), cols][None]
            ybuf[pl.ds(r0, CONV_RC), cols] = acc.reshape(CONV_RC, CONV_CC)
            return carry

        lax.fori_loop(0, CONV_TS // CONV_RC, row_body, 0)

    y = ybuf[...] + cb_ref[...]
    mu = jnp.mean(y, axis=-1, keepdims=True)
    yc = y - mu
    var = jnp.mean(yc * yc, axis=-1, keepdims=True)
    z = yc * lax.rsqrt(var + LN_EPS) * g_ref[...] + beta_ref[...]
    o_ref[...] = (z * jax.nn.sigmoid(z)).astype(o_ref.dtype)


def _conv(proj, conv_w, conv_b, ln_g, ln_b, B, S):
    T = B * S
    ts = CONV_TS
    assert S % ts == 0
    nS = S // ts
    ca = COL_GLU // CONV_CH
    vec = lambda: pl.BlockSpec((1, CONV_CH), lambda b, s: (0, 0))
    return pl.pallas_call(
        _conv_kernel,
        out_shape=jax.ShapeDtypeStruct((T, CONV_CH), BF16),
        grid=(B, nS),
        in_specs=[pl.BlockSpec((ts, CONV_CH), lambda b, s: (b * nS + s, ca)),
                  pl.BlockSpec((ts, CONV_CH), lambda b, s: (b * nS + s, ca + 1)),
                  pl.BlockSpec((CONV_WIDTH, CONV_CH), lambda b, s: (0, 0)),
                  vec(), vec(), vec()],
        out_specs=pl.BlockSpec((ts, CONV_CH), lambda b, s: (b * nS + s, 0)),
        scratch_shapes=[pltpu.VMEM((CONV_HALO + ts, CONV_CH), F32),
                        pltpu.VMEM((ts, CONV_CH), F32),
                        pltpu.VMEM((CONV_WIDTH * SUBLANES, CONV_CH), F32)],
        compiler_params=_cparams(("arbitrary", "arbitrary")),
        name="conformer_conv",
    )(proj, proj, conv_w, conv_b.reshape(1, -1), ln_g.reshape(1, -1), ln_b.reshape(1, -1))


def _merge_kernel(o_ref, h_ref, wa_ref, wb_ref, ga_ref, gb_ref, out_ref):
    ya = jnp.dot(o_ref[...], wa_ref[...], preferred_element_type=F32)
    yb = jnp.dot(h_ref[...], wb_ref[...], preferred_element_type=F32)
    ga = jax.nn.sigmoid(ga_ref[...].astype(F32))
    gb = jax.nn.sigmoid(gb_ref[...].astype(F32))
    out_ref[...] = (ga * ya + gb * yb).astype(out_ref.dtype)


def _merge(o_nsa, h_conv, wa, wb, proj):
    T = o_nsa.shape[0]
    tm = min(512, T)
    tn = D_MODEL
    nN = D_MODEL // tn
    ga0 = COL_MERGE // tn
    return pl.pallas_call(
        _merge_kernel,
        out_shape=jax.ShapeDtypeStruct((T, D_MODEL), BF16),
        grid=(T // tm, nN),
        in_specs=[pl.BlockSpec((tm, Q_DIM), lambda i, j: (i, 0)),
                  pl.BlockSpec((tm, CONV_CH), lambda i, j: (i, 0)),
                  pl.BlockSpec((Q_DIM, tn), lambda i, j: (0, j)),
                  pl.BlockSpec((CONV_CH, tn), lambda i, j: (0, j)),
                  pl.BlockSpec((tm, tn), lambda i, j: (i, ga0 + j)),
                  pl.BlockSpec((tm, tn), lambda i, j: (i, ga0 + nN + j))],
        out_specs=pl.BlockSpec((tm, tn), lambda i, j: (i, j)),
        compiler_params=_cparams(("arbitrary", "arbitrary")),
        name="merge",
    )(o_nsa, h_conv, wa, wb, proj, proj)


def _layer_norm_rows(y, g, b):
    mu = jnp.mean(y, axis=-1, keepdims=True)
    yc = y - mu
    var = jnp.mean(yc * yc, axis=-1, keepdims=True)
    return yc * lax.rsqrt(var + LN_EPS) * g + b


def _split2(v):
    hi = v.astype(BF16)
    return hi, (v - hi.astype(F32)).astype(BF16)


def _outproj_kernel(mix_ref, w_ref, x_ref, g_ref, b_ref, rw_hi_ref, rw_lo_ref, rb_ref,
                    x1_ref, x1b_ref, ti_ref, tg_ref, rk_ref, cnt_ref):
    m = jnp.dot(mix_ref[...], w_ref[...], preferred_element_type=F32)
    x1 = _layer_norm_rows(DEEPNORM_ALPHA * x_ref[...] + m, g_ref[...], b_ref[...])
    x1_ref[...] = x1
    x1b_ref[...] = x1.astype(BF16)
    x_hi, x_lo = _split2(x1)
    logits = (jnp.dot(x_hi, rw_hi_ref[...], preferred_element_type=F32)
              + jnp.dot(x_hi, rw_lo_ref[...], preferred_element_type=F32)
              + jnp.dot(x_lo, rw_hi_ref[...], preferred_element_type=F32)) + rb_ref[...]
    lane = lax.broadcasted_iota(jnp.int32, logits.shape, 1)
    logits = jnp.where(lane < N_EXPERTS, logits, -jnp.inf)
    ti = jnp.zeros(logits.shape, jnp.int32)
    tv = jnp.zeros(logits.shape, F32)
    top0 = None
    den = jnp.zeros((logits.shape[0], 1), F32)
    picks = []
    for k in range(TOP_K):
        mx = jnp.max(logits, axis=-1, keepdims=True)
        idx = jnp.min(jnp.where(logits == mx, lane, LANES), axis=-1, keepdims=True)
        if top0 is None:
            top0 = mx
        e = jnp.exp(mx - top0)
        den = den + e
        ti = jnp.where(lane == k, idx, ti)
        tv = jnp.where(lane == k, e, tv)
        picks.append(lane == idx)
        logits = jnp.where(picks[-1], -jnp.inf, logits)
    ti_ref[...] = ti
    tg_ref[...] = tv / den

    @pl.when(pl.program_id(0) == 0)
    def _():
        cnt_ref[...] = jnp.zeros(cnt_ref.shape, F32)

    tm = logits.shape[0]
    chosen = functools.reduce(jnp.logical_or, picks).astype(BF16)
    tri = (lax.broadcasted_iota(jnp.int32, (tm, tm), 1)
           <= lax.broadcasted_iota(jnp.int32, (tm, tm), 0)).astype(BF16)
    upto = jnp.dot(tri, chosen, preferred_element_type=F32) + cnt_ref[pl.ds(0, 1), :]
    rk = jnp.zeros(logits.shape, jnp.int32)
    for k in range(TOP_K):
        r_k = jnp.sum(jnp.where(picks[k], upto, 0.0), axis=-1, keepdims=True) - 1.0
        rk = jnp.where(lane == k, r_k.astype(jnp.int32), rk)
    rk_ref[...] = rk
    cnt_ref[...] = jnp.broadcast_to(upto[tm - 1:tm, :], cnt_ref.shape)


def _outproj(mix, w_out, x2d, g, b, rw_hi, rw_lo, rb):
    T = mix.shape[0]
    tm = min(512, T)
    full = lambda shape: pl.BlockSpec(shape, lambda i: (0, 0))
    rowb = lambda w: pl.BlockSpec((tm, w), lambda i: (i, 0))
    return pl.pallas_call(
        _outproj_kernel,
        out_shape=(jax.ShapeDtypeStruct((T, D_MODEL), F32),
                   jax.ShapeDtypeStruct((T, D_MODEL), BF16),
                   jax.ShapeDtypeStruct((T, LANES), jnp.int32),
                   jax.ShapeDtypeStruct((T, LANES), F32),
                   jax.ShapeDtypeStruct((T, LANES), jnp.int32),
                   jax.ShapeDtypeStruct((SUBLANES, LANES), F32)),
        grid=(T // tm,),
        in_specs=[rowb(D_MODEL), full((D_MODEL, D_MODEL)), rowb(D_MODEL),
                  full((1, D_MODEL)), full((1, D_MODEL)),
                  full((D_MODEL, LANES)), full((D_MODEL, LANES)), full((1, LANES))],
        out_specs=(rowb(D_MODEL), rowb(D_MODEL), rowb(LANES), rowb(LANES), rowb(LANES),
                   full((SUBLANES, LANES))),
        compiler_params=_cparams(("arbitrary",)),
        name="outproj_ln_router",
    )(mix, w_out, x2d, g, b, rw_hi, rw_lo, rb)


def _moe_up_kernel(be_ref, nused_ref, x_ref, wg_ref, wu_ref, bg_ref, bu_ref, *rest, base):
    h_ref, wgb_ref, wub_ref = rest[-3:]
    local = pl.program_id(1)
    r = base + local
    prev = be_ref[jnp.maximum(r - 1, 0)]
    fresh = (local == 0) | (be_ref[r] != prev)

    @pl.when(fresh)
    def _():
        wgb_ref[...] = wg_ref[0].astype(BF16)
        wub_ref[...] = wu_ref[0].astype(BF16)

    @pl.when(r < nused_ref[0])
    def _():
        x = x_ref[...]
        gt = jnp.dot(x, wgb_ref[...], preferred_element_type=F32) + bg_ref[0]
        up = jnp.dot(x, wub_ref[...], preferred_element_type=F32) + bu_ref[0]
        gt = jnp.minimum(gt, SWIGLU_LIMIT)
        up = jnp.clip(up, -SWIGLU_LIMIT, SWIGLU_LIMIT)
        h = gt * jax.nn.sigmoid(SWIGLU_ALPHA * gt) * (up + 1.0)
        h_ref[...] = h.astype(h_ref.dtype)


def _moe_row_maps(nblk):
    rd = lambda r, nu: jnp.minimum(r, nu[0] - 1)
    wr = lambda r, nu: jnp.where(r < nu[0], r, nblk)
    return rd, wr


def _moe_up(blk_e, n_used, xs_seg, seg, nblk, hbuf, w_gate, w_up, b_gate, b_up):
    bm, tf = MOE_BLOCK, MOE_UP_TILE
    seg_blk = xs_seg.shape[0] // bm
    base = seg * seg_blk
    rd = lambda r, nu: jnp.clip(nu[0] - 1 - base, 0, r)
    wr = lambda r, nu: jnp.where(base + r < nu[0], base + r, nblk)
    wspec = pl.BlockSpec((1, D_MODEL, tf), lambda f, r, be, nu: (be[base + r], 0, f))
    bspec = pl.BlockSpec((1, 1, tf), lambda f, r, be, nu: (be[base + r], 0, f))
    in_specs = [pl.BlockSpec((bm, D_MODEL), lambda f, r, be, nu: (rd(r, nu), 0)),
                wspec, wspec, bspec, bspec]
    args = [blk_e, n_used, xs_seg, w_gate, w_up, b_gate.reshape(N_EXPERTS, 1, D_FF),
            b_up.reshape(N_EXPERTS, 1, D_FF)]
    aliases = {}
    if hbuf is not None:
        in_specs.append(pl.BlockSpec(memory_space=pl.ANY))
        aliases = {len(args): 0}
        args.append(hbuf)
    grid_spec = pltpu.PrefetchScalarGridSpec(
        num_scalar_prefetch=2,
        grid=(D_FF // tf, seg_blk),
        in_specs=in_specs,
        out_specs=pl.BlockSpec((bm, tf), lambda f, r, be, nu: (wr(r, nu), f)),
        scratch_shapes=[pltpu.VMEM((D_MODEL, tf), BF16), pltpu.VMEM((D_MODEL, tf), BF16)],
    )
    return pl.pallas_call(
        functools.partial(_moe_up_kernel, base=base),
        out_shape=jax.ShapeDtypeStruct(((nblk + 1) * bm, D_FF), BF16),
        grid_spec=grid_spec,
        input_output_aliases=aliases,
        compiler_params=_cparams(("arbitrary", "arbitrary")),
        name="moe_up",
    )(*args)


def _moe_down_kernel(be_ref, nused_ref, h_ref, wd_ref, bd_ref, o_ref, wdb_ref):
    r = pl.program_id(0)
    prev = be_ref[jnp.maximum(r - 1, 0)]
    fresh = (r == 0) | (be_ref[r] != prev)

    @pl.when(fresh)
    def _():
        wdb_ref[...] = wd_ref[0].astype(BF16)

    @pl.when(r < nused_ref[0])
    def _():
        y = jnp.dot(h_ref[...], wdb_ref[...], preferred_element_type=F32) + bd_ref[0]
        o_ref[...] = y.astype(o_ref.dtype)


def _moe_down(blk_e, n_used, h, w_down, b_down):
    bm = MOE_BLOCK
    nblk = h.shape[0] // bm - 1
    rd, wr = _moe_row_maps(nblk)
    grid_spec = pltpu.PrefetchScalarGridSpec(
        num_scalar_prefetch=2,
        grid=(nblk,),
        in_specs=[pl.BlockSpec((bm, D_FF), lambda r, be, nu: (rd(r, nu), 0)),
                  pl.BlockSpec((1, D_FF, D_MODEL), lambda r, be, nu: (be[r], 0, 0)),
                  pl.BlockSpec((1, 1, D_MODEL), lambda r, be, nu: (be[r], 0, 0))],
        out_specs=pl.BlockSpec((bm, D_MODEL), lambda r, be, nu: (wr(r, nu), 0)),
        scratch_shapes=[pltpu.VMEM((D_FF, D_MODEL), BF16)],
    )
    return pl.pallas_call(
        _moe_down_kernel,
        out_shape=jax.ShapeDtypeStruct(((nblk + 1) * bm, D_MODEL), BF16),
        grid_spec=grid_spec,
        compiler_params=_cparams(("arbitrary",)),
        name="moe_down",
    )(blk_e, n_used, h, w_down, b_down.reshape(N_EXPERTS, 1, D_MODEL))


def _final_kernel(x1_ref, *refs):
    y_refs, (tg_ref, g_ref, b_ref, o_ref) = refs[:TOP_K], refs[TOP_K:]
    tg = tg_ref[...]
    f = tg[:, 0:1] * y_refs[0][...].astype(F32)
    for k in range(1, TOP_K):
        f = f + tg[:, k:k + 1] * y_refs[k][...].astype(F32)
    o_ref[...] = _layer_norm_rows(DEEPNORM_ALPHA * x1_ref[...] + f, g_ref[...], b_ref[...])


def _final(x1, ys, tg, g, b):
    T = x1.shape[0]
    tm = min(512, T)
    rows = pl.BlockSpec((tm, D_MODEL), lambda i: (i, 0))
    vec = pl.BlockSpec((1, D_MODEL), lambda i: (0, 0))
    return pl.pallas_call(
        _final_kernel,
        out_shape=jax.ShapeDtypeStruct((T, D_MODEL), F32),
        grid=(T // tm,),
        in_specs=[rows] + [rows] * TOP_K + [pl.BlockSpec((tm, LANES), lambda i: (i, 0)), vec, vec],
        out_specs=rows,
        compiler_params=_cparams(("arbitrary",)),
        name="combine_ln",
    )(x1, *ys, tg, g, b)


_W_IN_SPLITS = np.cumsum([Q_DIM] + [KV_DIM] * 6 + [NSA_GATE_DIM, 2 * CONV_CH, 2 * D_MODEL])
IN_DIM = int(_W_IN_SPLITS[-1])
WPREP_ROWS = 256


def _wprep_kernel(w_ref, o_ref):
    sp = _W_IN_SPLITS
    o_ref[:, COL_Q:COL_Q + Q_DIM] = (
        w_ref[0, :, 0:sp[0]] * (LOG2E / math.sqrt(HEAD_DIM))).astype(BF16)
    o_ref[:, COL_GLU:COL_GLU + 2 * CONV_CH] = w_ref[0, :, sp[7]:sp[8]].astype(BF16)
    o_ref[:, COL_MERGE:COL_MERGE + 2 * D_MODEL] = w_ref[0, :, sp[8]:sp[9]].astype(BF16)
    o_ref[:, COL_KC:COL_KC + 6 * KV_DIM] = w_ref[0, :, sp[0]:sp[6]].astype(BF16)
    per_g = HEADS_PER_GROUP * 3
    zeros = jnp.zeros((w_ref.shape[1], LANES - per_g), BF16)
    for g in range(N_KV_GROUPS):
        gate = w_ref[0, :, sp[6] + g * per_g:sp[6] + (g + 1) * per_g].astype(BF16)
        o_ref[:, COL_GATE + g * LANES:COL_GATE + (g + 1) * LANES] = jnp.concatenate(
            [gate, zeros], axis=1)


def _prep_w_in(w_in3):
    D = w_in3.shape[1]
    rows = min(WPREP_ROWS, D)
    return pl.pallas_call(
        _wprep_kernel,
        out_shape=jax.ShapeDtypeStruct((D, PROJ_W), BF16),
        grid=(D // rows,),
        in_specs=[pl.BlockSpec((1, rows, IN_DIM), lambda i: (0, i, 0))],
        out_specs=pl.BlockSpec((rows, PROJ_W), lambda i: (i, 0)),
        compiler_params=_cparams(("arbitrary",)),
        name="w_in_layout",
    )(w_in3)


def _mixer(x2d, B, S, w_in, cmp_pe, cmp_w1, cmp_b1, cmp_w2, cmp_b2, w_nsa_proj,
           conv_w, conv_b, conv_ln_g, conv_ln_b, w_conv_proj):
    T = B * S
    G = N_KV_GROUPS
    proj = _inproj(x2d, _prep_w_in(w_in))

    kcvc = _compress(proj, cmp_pe.reshape(2, 1, CMP_BLOCK * HEAD_DIM),
                     cmp_w1.astype(BF16), cmp_b1.reshape(2, 1, HEAD_DIM),
                     cmp_w2.astype(BF16), cmp_b2.reshape(2, 1, HEAD_DIM), B, S)

    o_nsa = _nsa(proj, kcvc, _alibi_rows().astype(BF16), B, S)

    h_conv = _conv(proj, conv_w, conv_b, conv_ln_g, conv_ln_b, B, S)
    return _merge(o_nsa, h_conv, w_nsa_proj.astype(BF16), w_conv_proj.astype(BF16), proj)


def _moe(x1b, top_i, top_rank, expert_counts, w_gate, b_gate, w_up, b_up, w_down, b_down):
    T = x1b.shape[0]
    A = T * TOP_K
    bm = MOE_BLOCK
    flat_e = top_i[:, :TOP_K].reshape(A)
    rank = top_rank[:, :TOP_K].reshape(A)
    counts = expert_counts[0, :N_EXPERTS].astype(jnp.int32)
    padded = (counts + bm - 1) // bm * bm
    pad_end = jnp.cumsum(padded)
    pad_start = pad_end - padded
    dest = pad_start[flat_e] + rank
    P = -(-A // bm) * bm + N_EXPERTS * bm
    nblk = P // bm
    flat_tok = jnp.arange(A, dtype=jnp.int32) // TOP_K
    buf_tok = (jnp.arange(P, dtype=jnp.int32) % T).at[dest].set(flat_tok)
    blk_start = jnp.arange(nblk, dtype=jnp.int32) * bm
    blk_e = jnp.minimum(jnp.sum(blk_start[:, None] >= pad_end[None, :], axis=1),
                        N_EXPERTS - 1).astype(jnp.int32)
    n_used = (pad_end[-1] // bm).astype(jnp.int32).reshape(1)
    seg_rows = P // MOE_SEGMENTS
    assert seg_rows % bm == 0
    h = None
    for seg in range(MOE_SEGMENTS):
        xs_seg = x1b[buf_tok[seg * seg_rows:(seg + 1) * seg_rows]]
        h = _moe_up(blk_e, n_used, xs_seg, seg, nblk, h, w_gate, w_up, b_gate, b_up)
    out = _moe_down(blk_e, n_used, h, w_down, b_down)
    dest_k = dest.reshape(T, TOP_K)
    return [out[dest_k[:, k]] for k in range(TOP_K)]


def kernel(x, w_in, cmp_pe, cmp_w1, cmp_b1, cmp_w2, cmp_b2, w_nsa_proj, conv_w, conv_b, conv_ln_g, conv_ln_b, w_conv_proj, w_out, ln1_g, ln1_b, router_w, router_b, w_gate, b_gate, w_up, b_up, w_down, b_down, ln2_g, ln2_b):
    B, S, D = x.shape
    T = B * S
    x2d = x.reshape(T, D)
    for l in range(DEPTH):
        mix = _mixer(x2d, B, S, w_in[l:l + 1], cmp_pe[l], cmp_w1[l], cmp_b1[l], cmp_w2[l], cmp_b2[l],
                     w_nsa_proj[l], conv_w[l], conv_b[l], conv_ln_g[l], conv_ln_b[l],
                     w_conv_proj[l])
        rw = jnp.pad(router_w[l], ((0, 0), (0, LANES - N_EXPERTS)))
        rw_hi = rw.astype(BF16)
        rw_lo = (rw - rw_hi.astype(F32)).astype(BF16)
        rb = jnp.pad(router_b[l], (0, LANES - N_EXPERTS)).reshape(1, LANES)
        x1, x1b, top_i, top_g, top_rank, expert_counts = _outproj(
            mix, w_out[l].astype(BF16), x2d, ln1_g[l].reshape(1, D), ln1_b[l].reshape(1, D),
            rw_hi, rw_lo, rb)
        y4 = _moe(x1b, top_i, top_rank, expert_counts, w_gate[l], b_gate[l], w_up[l], b_up[l],
                  w_down[l], b_down[l])
        x2d = _final(x1, y4, top_g, ln2_g[l].reshape(1, D), ln2_b[l].reshape(1, D))
    return x2d.reshape(B, S, D)
```

```python
import functools
import math

import numpy as np
import jax
import jax.numpy as jnp
from jax import lax
from jax.experimental import pallas as pl
from jax.experimental.pallas import tpu as pltpu

D_MODEL = 2048
N_HEADS = 16
N_KV_GROUPS = 2
HEADS_PER_GROUP = N_HEADS // N_KV_GROUPS
HEAD_DIM = 128
CMP_BLOCK = 32
CMP_STRIDE = 16
SEL_BLOCK = 64
SEL_SHIFT = 6
SEL_TOP_N = 16
WINDOW = 512
Q_BLOCK = 128
N_OVERLAP = (SEL_BLOCK + CMP_BLOCK) // CMP_STRIDE - 1
FORCE_BONUS = 1.0e4
CONV_CH = D_MODEL // 2
CONV_WIDTH = 31
N_EXPERTS = 32
TOP_K = 4
D_FF = D_MODEL
SWIGLU_LIMIT = 7.0
SWIGLU_ALPHA = 1.702
LN_EPS = 1e-5
DEPTH = 1
DEEPNORM_ALPHA = (2 * DEPTH) ** 0.25
NEG_INF = -1e30
TINY = 1e-30

Q_DIM = N_HEADS * HEAD_DIM
KV_DIM = N_KV_GROUPS * HEAD_DIM
NSA_GATE_DIM = N_HEADS * 3

LANES = 128
SUBLANES = 8
VMEM_LIMIT = 56 * 1024 * 1024

COL_Q = 0
COL_GLU = COL_Q + Q_DIM
COL_MERGE = COL_GLU + 2 * CONV_CH
COL_KC = COL_MERGE + 2 * D_MODEL
COL_VC = COL_KC + KV_DIM
COL_KS = COL_VC + KV_DIM
COL_VS = COL_KS + KV_DIM
COL_KW = COL_VS + KV_DIM
COL_VW = COL_KW + KV_DIM
COL_GATE = COL_VW + KV_DIM
PROJ_W = COL_GATE + N_KV_GROUPS * LANES

BF16 = jnp.bfloat16
F32 = jnp.float32

MOE_BLOCK = 512
MOE_UP_TILE = 1024
MOE_SEGMENTS = 8

_ALIBI = np.exp2(-8.0 * np.arange(1, N_HEADS + 1, dtype=np.float32) / N_HEADS).astype(np.float32)
_ALIBI = _ALIBI.reshape(N_KV_GROUPS, HEADS_PER_GROUP)


def _cparams(sem, vmem=VMEM_LIMIT):
    return pltpu.CompilerParams(dimension_semantics=sem, vmem_limit_bytes=vmem)


def _inproj_kernel(x_ref, w_ref, o_ref, xb_ref):
    @pl.when(pl.program_id(1) == 0)
    def _():
        xb_ref[...] = x_ref[...].astype(BF16)

    o_ref[...] = jnp.dot(xb_ref[...], w_ref[...],
                         preferred_element_type=F32).astype(o_ref.dtype)


def _inproj(x2d, w_p):
    T, D = x2d.shape
    N = w_p.shape[1]
    tm = min(512, T)
    tn = N // 3
    assert T % tm == 0 and N % tn == 0 and tn % (2 * LANES) == 0
    return pl.pallas_call(
        _inproj_kernel,
        out_shape=jax.ShapeDtypeStruct((T, N), BF16),
        grid=(T // tm, N // tn),
        in_specs=[pl.BlockSpec((tm, D), lambda i, j: (i, 0)),
                  pl.BlockSpec((D, tn), lambda i, j: (0, j))],
        out_specs=pl.BlockSpec((tm, tn), lambda i, j: (i, j)),
        scratch_shapes=[pltpu.VMEM((tm, D), BF16)],
        compiler_params=_cparams(("arbitrary", "arbitrary")),
        name="inproj",
    )(x2d, w_p)


def _gelu_tanh(x):
    c = math.sqrt(2.0 / math.pi)
    return 0.5 * x * (1.0 + jnp.tanh(c * (x + 0.044715 * (x * x * x))))


def _compress_kernel(x_ref, pe_ref, w1_ref, b1_ref, w2_ref, b2_ref, o_ref, xf_ref):
    S = x_ref.shape[0]
    n16 = S // CMP_STRIDE
    xf_ref[...] = x_ref[...].astype(F32)
    top = jnp.zeros((n16, HEAD_DIM), F32)
    bot = jnp.zeros((n16, HEAD_DIM), F32)
    for j in range(CMP_STRIDE):
        xj = xf_ref[pl.ds(j, n16, stride=CMP_STRIDE), :]
        lo, hi = j * HEAD_DIM, (CMP_STRIDE + j) * HEAD_DIM
        top = top + jnp.dot((xj + pe_ref[0, :, lo:lo + HEAD_DIM]).astype(BF16),
                            w1_ref[0, lo:lo + HEAD_DIM, :], preferred_element_type=F32)
        bot = bot + jnp.dot((xj + pe_ref[0, :, hi:hi + HEAD_DIM]).astype(BF16),
                            w1_ref[0, hi:hi + HEAD_DIM, :], preferred_element_type=F32)
    pre = top + pltpu.roll(bot, n16 - 1, 0) + b1_ref[0]
    h = _gelu_tanh(pre)
    o = jnp.dot(h.astype(BF16), w2_ref[0], preferred_element_type=F32) + b2_ref[0]
    o_ref[0, 0] = o.astype(o_ref.dtype)


def _compress(proj, pe, w1, b1, w2, b2, B, S):
    n_slot = 2 * N_KV_GROUPS
    n16 = S // CMP_STRIDE
    wide = CMP_BLOCK * HEAD_DIM
    kv = lambda s, b: (s // N_KV_GROUPS, 0, 0)
    return pl.pallas_call(
        _compress_kernel,
        out_shape=jax.ShapeDtypeStruct((n_slot, B, n16, HEAD_DIM), BF16),
        grid=(n_slot, B),
        in_specs=[pl.BlockSpec((S, HEAD_DIM), lambda s, b: (b, COL_KC // HEAD_DIM + s)),
                  pl.BlockSpec((1, 1, wide), kv),
                  pl.BlockSpec((1, wide, HEAD_DIM), kv),
                  pl.BlockSpec((1, 1, HEAD_DIM), kv),
                  pl.BlockSpec((1, HEAD_DIM, HEAD_DIM), kv),
                  pl.BlockSpec((1, 1, HEAD_DIM), kv)],
        out_specs=pl.BlockSpec((1, 1, n16, HEAD_DIM), lambda s, b: (s, b, 0, 0)),
        scratch_shapes=[pltpu.VMEM((S, HEAD_DIM), F32)],
        compiler_params=_cparams(("arbitrary", "arbitrary")),
        name="compress",
    )(proj, pe, w1, b1, w2, b2)


LOG2E = 1.4426950408889634
KAUG = 2 * HEAD_DIM
ROW_ALIBI = 0
ROW_SLOT = 16
SLC_GROUP = 8
SLC_TAIL = 4
assert 2 * SLC_GROUP == 16 and SLC_TAIL == 4
GROUP_W = HEADS_PER_GROUP * Q_BLOCK
WIN_TILES = WINDOW // Q_BLOCK
WIN_KEYS = WINDOW + Q_BLOCK


def _window_pos_cols():
    rel = np.arange(WIN_KEYS) - WINDOW
    a = (rel // SEL_BLOCK) * SEL_BLOCK
    r = rel - a
    cols = np.zeros((WIN_KEYS, LANES), np.float32)
    cols[:, ROW_ALIBI:ROW_ALIBI + 3] = a[:, None]
    cols[:, ROW_ALIBI + 3:ROW_ALIBI + 6] = r[:, None]
    return jnp.asarray(cols, BF16)


def _alibi_rows():
    s2 = jnp.asarray(_ALIBI * np.float32(LOG2E), F32)
    hi = s2.astype(BF16)
    r1 = s2 - hi.astype(F32)
    mid = r1.astype(BF16)
    lo = (r1 - mid.astype(F32)).astype(BF16)
    trip = jnp.stack([hi, mid, lo, hi, mid, lo], axis=1)
    rows = jnp.pad(trip, ((0, 0), (0, 16 - 6), (0, 0)))
    return jnp.repeat(rows, Q_BLOCK, axis=2)


def _dot_ta(a, b):
    return lax.dot_general(a, b, (((0,), (0,)), ((), ())), preferred_element_type=F32)


def _fill_qaug(qaug, q_ref, alibi_ref):
    for h in range(HEADS_PER_GROUP):
        qaug[pl.ds(0, HEAD_DIM), pl.ds(h * Q_BLOCK, Q_BLOCK)] = (
            q_ref[:, h * HEAD_DIM:(h + 1) * HEAD_DIM].T)
    qaug[pl.ds(HEAD_DIM, 16), :] = alibi_ref[0]
    qaug[pl.ds(HEAD_DIM + 16, HEAD_DIM - 16), :] = jnp.zeros((HEAD_DIM - 16, GROUP_W), BF16)


def _pos_cols(a_val, r_val, lane, slot_lane=None):
    base = jnp.where((lane >= ROW_ALIBI + 3) & (lane < ROW_ALIBI + 6), r_val, 0.0)
    if slot_lane is not None:
        base = jnp.where(lane == slot_lane, 1.0, base)
    return jnp.where(lane < ROW_ALIBI + 3, a_val, base)


def _nsa_cmp_branch(c, kc_ref, vc_ref, ocmp_ref, selb_ref, jmin_ref, qaug, n_selb):
    n_vis = c * (Q_BLOCK // CMP_STRIDE) + (Q_BLOCK - CMP_BLOCK) // CMP_STRIDE + 1
    tiles = kc_ref.shape[2] // LANES
    for i in range(1, tiles + 1):
        cond = n_vis > (i - 1) * LANES
        if i < tiles:
            cond = cond & (n_vis <= i * LANES)
        pl.when(cond)(functools.partial(
            _nsa_cmp_rows, c, i * LANES, kc_ref, vc_ref, ocmp_ref, selb_ref, jmin_ref, qaug,
            n_selb))


def _nsa_cmp_rows(c, ncp, kc_ref, vc_ref, ocmp_ref, selb_ref, jmin_ref, qaug, n_selb):
    nsp = selb_ref.shape[0]

    n_i = lax.broadcasted_iota(jnp.int32, (ncp, LANES), 0)
    lane = lax.broadcasted_iota(jnp.int32, (ncp, LANES), 1)
    end_rel = n_i * CMP_STRIDE + (CMP_BLOCK - 1) - c * Q_BLOCK
    a_val = jnp.left_shift(jnp.right_shift(end_rel, SEL_SHIFT), SEL_SHIFT).astype(F32)
    r_val = (end_rel & (SEL_BLOCK - 1)).astype(F32)
    pc = _pos_cols(a_val, r_val, lane).astype(BF16)
    s = jnp.dot(jnp.concatenate([kc_ref[0, 0, pl.ds(0, ncp), :], pc], axis=1), qaug[...],
                preferred_element_type=F32)

    tq = lax.broadcasted_iota(jnp.int32, (1, Q_BLOCK), 1)
    t_row = c * Q_BLOCK + tq
    n_col = lax.broadcasted_iota(jnp.int32, (ncp, Q_BLOCK), 0)
    mask_bias = jnp.where(n_col * CMP_STRIDE + (CMP_BLOCK - 1) <= t_row, 0.0, NEG_INF)
    has_valid = t_row >= CMP_BLOCK - 1

    imp = jnp.zeros((ncp, Q_BLOCK), F32)
    es, rls = [], []
    for h in range(HEADS_PER_GROUP):
        sh = s[:, h * Q_BLOCK:(h + 1) * Q_BLOCK] + mask_bias
        m = jnp.max(sh, axis=0, keepdims=True)
        e = jnp.exp2(sh - m)
        l = jnp.sum(e, axis=0, keepdims=True)
        rl = jnp.where(has_valid, 1.0 / jnp.maximum(l, TINY), 0.0)
        imp = imp + e * rl
        es.append(e.astype(BF16))
        rls.append(rl)
    oT = _dot_ta(vc_ref[0, 0, pl.ds(0, ncp), :], jnp.concatenate(es, axis=1))
    ocmp_ref[...] = oT * jnp.concatenate(rls, axis=1)

    nb = min(nsp, ncp * CMP_STRIDE // SEL_BLOCK)
    j_i = lax.broadcasted_iota(jnp.int32, (nb, ncp), 0)
    n_j = lax.broadcasted_iota(jnp.int32, (nb, ncp), 1)
    lo = j_i * (SEL_BLOCK // CMP_STRIDE) - (CMP_BLOCK // CMP_STRIDE) + 1
    ovl = ((n_j >= lo) & (n_j < lo + N_OVERLAP)).astype(BF16)
    i_hi = imp.astype(BF16)
    r1 = imp - i_hi.astype(F32)
    i_mid = r1.astype(BF16)
    i_lo = (r1 - i_mid.astype(F32)).astype(BF16)
    imp_sel = (jnp.dot(ovl, i_hi, preferred_element_type=F32)
               + jnp.dot(ovl, i_mid, preferred_element_type=F32)
               + jnp.dot(ovl, i_lo, preferred_element_type=F32))

    blk = lax.broadcasted_iota(jnp.int32, (nb, Q_BLOCK), 0)
    cur = jnp.right_shift(t_row, SEL_SHIFT)
    valid = blk * SEL_BLOCK <= t_row
    forced = (blk == 0) | (blk == cur) | (blk == cur - 1)
    score = jnp.where(valid, imp_sel + jnp.where(forced, FORCE_BONUS, 0.0), -1.0)
    score = jnp.where(blk < n_selb, score, -3.0)
    selb = jnp.full((nb, Q_BLOCK), NEG_INF, F32)
    for _ in range(min(SEL_TOP_N, n_selb, nb)):
        mx = jnp.max(score, axis=0, keepdims=True)
        first = jnp.min(jnp.where(score == mx, blk, nb), axis=0, keepdims=True)
        pick = blk == first
        selb = jnp.where(pick, 0.0, selb)
        score = jnp.where(pick, -2.0, score)
    selb_ref[pl.ds(0, nb), :] = selb
    if nb < nsp:
        selb_ref[pl.ds(nb, nsp - nb), :] = jnp.full((nsp - nb, Q_BLOCK), NEG_INF, F32)
    blk_col = lax.broadcasted_iota(jnp.int32, (nb, 1), 0)
    any_sel = jnp.max(selb, axis=1, keepdims=True) > 0.5 * NEG_INF
    n_past = (Q_BLOCK // SEL_BLOCK) * c
    jmin_ref[0] = jnp.min(jnp.where(any_sel & (blk_col >= 1) & (blk_col < n_past),
                                    blk_col, n_past))


def _tile_scores(qaug, k_tile, pc):
    kaug = jnp.concatenate([k_tile, pc.astype(BF16)], axis=1)
    return jnp.dot(kaug, qaug[...], preferred_element_type=F32)


def _softmax_tile(st, qaug, k_tile, pc, v_tile, bias, first):
    _softmax_update(st, _tile_scores(qaug, k_tile, pc), v_tile, bias, first)


def _softmax_update(st, s, v_tile, bias, first):
    m_ref, l_ref, acc_ref = st
    if bias is not None:
        s = jnp.concatenate([s[:, h * Q_BLOCK:(h + 1) * Q_BLOCK] + bias
                             for h in range(HEADS_PER_GROUP)], axis=1)
    mx = jnp.max(s, axis=0, keepdims=True)
    if first:
        m_new = mx
    else:
        m_old = m_ref[pl.ds(0, 1), :]
        m_new = jnp.maximum(m_old, mx)
    p = jnp.exp2(s - m_new)
    ps = jnp.sum(p, axis=0, keepdims=True)
    pv = _dot_ta(v_tile, p.astype(BF16))
    if first:
        l_ref[pl.ds(0, 1), :] = ps
        acc_ref[...] = pv
    else:
        alpha = jnp.exp2(m_old - m_new)
        l_ref[pl.ds(0, 1), :] = alpha * l_ref[pl.ds(0, 1), :] + ps
        acc_ref[...] = alpha * acc_ref[...] + pv
    m_ref[pl.ds(0, 1), :] = m_new


def _bias_rows(s, bias):
    return jnp.concatenate([s[:, h * Q_BLOCK:(h + 1) * Q_BLOCK] + bias
                            for h in range(HEADS_PER_GROUP)], axis=1)


def _nsa_kernel(q_ref, kc_ref, vc_ref, ks_ref, vs_ref, kw_ref, vw_ref, gate_ref, alibi_ref,
                wpc_ref, o_ref, qaug, m_ref, l_ref, acc_ref, owin_ref, ocmp_ref, selb_ref,
                jmin_ref, *, n_selb):
    c = pl.program_id(2)
    st = (m_ref, l_ref, acc_ref)
    _fill_qaug(qaug, q_ref, alibi_ref)
    _nsa_cmp_branch(c, kc_ref, vc_ref, ocmp_ref, selb_ref, jmin_ref, qaug, n_selb)
    n_past = (Q_BLOCK // SEL_BLOCK) * c
    jmin = jmin_ref[0]
    n_act = jnp.where(c >= 1, 1, 0) + (n_past - jmin)

    lane = lax.broadcasted_iota(jnp.int32, (SEL_BLOCK, LANES), 1)
    r_val = lax.broadcasted_iota(jnp.int32, (SEL_BLOCK, LANES), 0).astype(F32)
    ki = lax.broadcasted_iota(jnp.int32, (Q_BLOCK, Q_BLOCK), 0)
    qi = lax.broadcasted_iota(jnp.int32, (Q_BLOCK, Q_BLOCK), 1)
    causal_bias = jnp.where(ki <= qi, 0.0, NEG_INF)
    tail_bias = jnp.where(ki > qi, 0.0, NEG_INF)

    def pair_cols(first_block_rel):
        return jnp.concatenate(
            [_pos_cols(float(SEL_BLOCK * (first_block_rel + i)), r_val, lane) for i in range(2)], axis=0)

    def finish():
        return acc_ref[...] / jnp.maximum(l_ref[pl.ds(0, 1), :], TINY)

    n_wt = WIN_TILES
    base = pl.multiple_of(c * Q_BLOCK, Q_BLOCK)

    @pl.when(c >= n_wt)
    def _():
        start = pl.multiple_of((c - n_wt) * Q_BLOCK, Q_BLOCK)
        kaug = jnp.concatenate([kw_ref[pl.ds(start, WIN_KEYS), :], wpc_ref[...]], axis=1)
        s = jnp.dot(kaug, qaug[...], preferred_element_type=F32)
        s_diag = _tile_scores(qaug, ks_ref[pl.ds(base, Q_BLOCK), :], pair_cols(0))
        s = jnp.concatenate([_bias_rows(s[:Q_BLOCK], tail_bias), s[Q_BLOCK:WINDOW],
                             _bias_rows(s[WINDOW:], causal_bias)], axis=0)
        p = jnp.exp2(s - jnp.max(s, axis=0, keepdims=True))
        l = jnp.sum(p, axis=0, keepdims=True)
        pv = _dot_ta(vw_ref[pl.ds(start, WIN_KEYS), :], p.astype(BF16))
        owin_ref[...] = pv / jnp.maximum(l, TINY)
        _softmax_update(st, s_diag, vs_ref[pl.ds(base, Q_BLOCK), :], causal_bias, True)

    @pl.when(c < n_wt)
    def _():
        _softmax_tile(st, qaug, kw_ref[pl.ds(base, Q_BLOCK), :], pair_cols(0),
                      vw_ref[pl.ds(base, Q_BLOCK), :], causal_bias, True)
        for w in range(1, n_wt):
            @pl.when(c - n_wt + w >= 0)
            def _(w=w):
                start = pl.multiple_of((c - n_wt + w) * Q_BLOCK, Q_BLOCK)
                _softmax_tile(st, qaug, kw_ref[pl.ds(start, Q_BLOCK), :],
                              pair_cols(2 * (w - n_wt)), vw_ref[pl.ds(start, Q_BLOCK), :],
                              None, False)
        owin_ref[...] = finish()
        _softmax_tile(st, qaug, ks_ref[pl.ds(base, Q_BLOCK), :], pair_cols(0),
                      vs_ref[pl.ds(base, Q_BLOCK), :], causal_bias, True)

    sub = lax.broadcasted_iota(jnp.int32, (16, GROUP_W), 0)

    def slc_tiles(pos0, n_tiles, per_tile):
        slot_rows = jnp.zeros((16, GROUP_W), F32)
        tiles = []
        for ti in range(n_tiles):
            ks_t, vs_t, pcs = [], [], []
            for k in range(per_tile):
                slot = ti * per_tile + k
                pos = pos0 + slot
                live = pos < n_act
                j = jnp.where(live & (pos > 0), jmin + pos - 1, 0)
                off = pl.multiple_of(j * SEL_BLOCK, SEL_BLOCK)
                ks_t.append(ks_ref[pl.ds(off, SEL_BLOCK), :])
                vs_t.append(vs_ref[pl.ds(off, SEL_BLOCK), :])
                a_val = ((j - 2 * c) * SEL_BLOCK).astype(F32)
                pcs.append(_pos_cols(a_val, r_val, lane, slot_lane=ROW_SLOT + slot))
                rowk = selb_ref[pl.ds(j, 1), :]
                rowk = jnp.where(live, rowk, NEG_INF)
                slot_rows = jnp.where(sub == slot, jnp.tile(rowk, (1, HEADS_PER_GROUP)), slot_rows)
            tiles.append((jnp.concatenate(ks_t, axis=0), jnp.concatenate(pcs, axis=0),
                          jnp.concatenate(vs_t, axis=0)))
        qaug[pl.ds(HEAD_DIM + ROW_SLOT, 16), :] = slot_rows.astype(BF16)
        scores = [_tile_scores(qaug, k_tile, pc) for k_tile, pc, _ in tiles]
        for s, (_, _, v_tile) in zip(scores, tiles):
            _softmax_update(st, s, v_tile, None, False)

    pair = 2 * SLC_GROUP
    rem = n_act & (pair - 1)
    n_pairs = jnp.right_shift(n_act, 4) + jnp.where(rem > SLC_GROUP, 1, 0)
    n_tail = jnp.where(rem > SLC_GROUP, 0, jnp.right_shift(rem + SLC_TAIL - 1, 2))

    def pair_body(i, carry):
        slc_tiles(i * pair, 2, SLC_GROUP)
        return carry

    def tail_body(i, carry):
        slc_tiles(n_pairs * pair + i * SLC_TAIL, 1, SLC_TAIL)
        return carry

    lax.fori_loop(0, n_pairs, pair_body, 0)
    lax.fori_loop(0, n_tail, tail_body, 0)
    o_slcT = finish()

    gT = jax.nn.sigmoid(gate_ref[...].astype(F32).T)
    o_cmpT = ocmp_ref[...]
    o_winT = owin_ref[...]
    outs = []
    for h in range(HEADS_PER_GROUP):
        hs = slice(h * Q_BLOCK, (h + 1) * Q_BLOCK)
        oT = (gT[3 * h:3 * h + 1, :] * o_cmpT[:, hs] + gT[3 * h + 1:3 * h + 2, :] * o_slcT[:, hs]
              + gT[3 * h + 2:3 * h + 3, :] * o_winT[:, hs])
        outs.append(oT.T)
    o_ref[...] = jnp.concatenate(outs, axis=1).astype(o_ref.dtype)


def _nsa(proj, kcvc, alibi, B, S):
    T = B * S
    C = S // Q_BLOCK
    G = N_KV_GROUPS
    ncp = kcvc.shape[2]
    n_selb = S // SEL_BLOCK
    nsp = max(LANES, n_selb)
    gw = HEADS_PER_GROUP * HEAD_DIM
    kvspec = lambda col: pl.BlockSpec((S, HEAD_DIM), lambda b, g, c: (b, col // HEAD_DIM + g))
    return pl.pallas_call(
        functools.partial(_nsa_kernel, n_selb=n_selb),
        out_shape=jax.ShapeDtypeStruct((T, Q_DIM), BF16),
        grid=(B, G, C),
        in_specs=[pl.BlockSpec((Q_BLOCK, gw), lambda b, g, c: (b * C + c, COL_Q // gw + g)),
                  pl.BlockSpec((1, 1, ncp, HEAD_DIM), lambda b, g, c: (g, b, 0, 0)),
                  pl.BlockSpec((1, 1, ncp, HEAD_DIM), lambda b, g, c: (G + g, b, 0, 0)),
                  kvspec(COL_KS), kvspec(COL_VS), kvspec(COL_KW), kvspec(COL_VW),
                  pl.BlockSpec((Q_BLOCK, LANES), lambda b, g, c: (b * C + c, COL_GATE // LANES + g)),
                  pl.BlockSpec((1, 16, GROUP_W), lambda b, g, c: (g, 0, 0)),
                  pl.BlockSpec((WIN_KEYS, LANES), lambda b, g, c: (0, 0))],
        out_specs=pl.BlockSpec((Q_BLOCK, gw), lambda b, g, c: (b * C + c, g)),
        scratch_shapes=[pltpu.VMEM((KAUG, GROUP_W), BF16),
                        pltpu.VMEM((SUBLANES, GROUP_W), F32),
                        pltpu.VMEM((SUBLANES, GROUP_W), F32),
                        pltpu.VMEM((HEAD_DIM, GROUP_W), F32),
                        pltpu.VMEM((HEAD_DIM, GROUP_W), F32),
                        pltpu.VMEM((HEAD_DIM, GROUP_W), F32),
                        pltpu.VMEM((nsp, Q_BLOCK), F32),
                        pltpu.SMEM((1,), jnp.int32)],
        compiler_params=_cparams(("arbitrary", "arbitrary", "arbitrary")),
        name="nsa",
    )(proj, kcvc, kcvc, proj, proj, proj, proj, proj, alibi, _window_pos_cols())


CONV_TS = 512
CONV_HALO = 32
CONV_RC = 64
CONV_CC = 256


def _conv_kernel(a_ref, b_ref, w_ref, cb_ref, g_ref, beta_ref, o_ref, hbuf, ybuf, wrep):
    si = pl.program_id(1)

    @pl.when(si == 0)
    def _():
        hbuf[pl.ds(0, CONV_HALO), :] = jnp.zeros((CONV_HALO, CONV_CH), F32)

    @pl.when(si > 0)
    def _():
        hbuf[pl.ds(0, CONV_HALO), :] = hbuf[pl.ds(CONV_TS, CONV_HALO), :]

    a = a_ref[...].astype(F32)
    bb = b_ref[...].astype(F32)
    hbuf[pl.ds(CONV_HALO, CONV_TS), :] = a * jax.nn.sigmoid(bb)

    for k in range(CONV_WIDTH):
        wrep[pl.ds(k * SUBLANES, SUBLANES), :] = jnp.broadcast_to(w_ref[k:k + 1, :],
                                                                  (SUBLANES, CONV_CH))

    off = CONV_HALO - (CONV_WIDTH - 1)
    groups = CONV_RC // SUBLANES
    for cc in range(CONV_CH // CONV_CC):
        cols = slice(cc * CONV_CC, (cc + 1) * CONV_CC)

        def row_body(r, carry, cols=cols):
            r0 = pl.multiple_of(r * CONV_RC, CONV_RC)
            win = hbuf[pl.ds(r0, CONV_RC + CONV_HALO), cols]
            acc = jnp.zeros((groups, SUBLANES, CONV_CC), F32)
            for res in range(SUBLANES):
                taps = [k for k in range(CONV_WIDTH) if (off + k) % SUBLANES == res]
                span = max(off + k for k in taps) - res + CONV_RC
                shifted = win[res:res + span, :]
                for k in taps:
                    a0 = off + k - res
                    rows = shifted[a0:a0 + CONV_RC, :].reshape(groups, SUBLANES, CONV_CC)
                    acc = acc + rows * wrep[pl.ds(k * SUBLANES, SUBLANES), cols][None]
            ybuf[pl.ds(r0, CONV_RC), cols] = acc.reshape(CONV_RC, CONV_CC)
            return carry

        lax.fori_loop(0, CONV_TS // CONV_RC, row_body, 0)

    y = ybuf[...] + cb_ref[...]
    mu = jnp.mean(y, axis=-1, keepdims=True)
    yc = y - mu
    var = jnp.mean(yc * yc, axis=-1, keepdims=True)
    z = yc * lax.rsqrt(var + LN_EPS) * g_ref[...] + beta_ref[...]
    o_ref[...] = (z * jax.nn.sigmoid(z)).astype(o_ref.dtype)


def _conv(proj, conv_w, conv_b, ln_g, ln_b, B, S):
    T = B * S
    ts = CONV_TS
    assert S % ts == 0
    nS = S // ts
    ca = COL_GLU // CONV_CH
    vec = lambda: pl.BlockSpec((1, CONV_CH), lambda b, s: (0, 0))
    return pl.pallas_call(
        _conv_kernel,
        out_shape=jax.ShapeDtypeStruct((T, CONV_CH), BF16),
        grid=(B, nS),
        in_specs=[pl.BlockSpec((ts, CONV_CH), lambda b, s: (b * nS + s, ca)),
                  pl.BlockSpec((ts, CONV_CH), lambda b, s: (b * nS + s, ca + 1)),
                  pl.BlockSpec((CONV_WIDTH, CONV_CH), lambda b, s: (0, 0)),
                  vec(), vec(), vec()],
        out_specs=pl.BlockSpec((ts, CONV_CH), lambda b, s: (b * nS + s, 0)),
        scratch_shapes=[pltpu.VMEM((CONV_HALO + ts, CONV_CH), F32),
                        pltpu.VMEM((ts, CONV_CH), F32),
                        pltpu.VMEM((CONV_WIDTH * SUBLANES, CONV_CH), F32)],
        compiler_params=_cparams(("arbitrary", "arbitrary")),
        name="conformer_conv",
    )(proj, proj, conv_w, conv_b.reshape(1, -1), ln_g.reshape(1, -1), ln_b.reshape(1, -1))


def _merge_kernel(o_ref, h_ref, wa_ref, wb_ref, ga_ref, gb_ref, out_ref):
    ya = jnp.dot(o_ref[...], wa_ref[...], preferred_element_type=F32)
    yb = jnp.dot(h_ref[...], wb_ref[...], preferred_element_type=F32)
    ga = jax.nn.sigmoid(ga_ref[...].astype(F32))
    gb = jax.nn.sigmoid(gb_ref[...].astype(F32))
    out_ref[...] = (ga * ya + gb * yb).astype(out_ref.dtype)


def _merge(o_nsa, h_conv, wa, wb, proj):
    T = o_nsa.shape[0]
    tm = min(512, T)
    tn = D_MODEL
    nN = D_MODEL // tn
    ga0 = COL_MERGE // tn
    return pl.pallas_call(
        _merge_kernel,
        out_shape=jax.ShapeDtypeStruct((T, D_MODEL), BF16),
        grid=(T // tm, nN),
        in_specs=[pl.BlockSpec((tm, Q_DIM), lambda i, j: (i, 0)),
                  pl.BlockSpec((tm, CONV_CH), lambda i, j: (i, 0)),
                  pl.BlockSpec((Q_DIM, tn), lambda i, j: (0, j)),
                  pl.BlockSpec((CONV_CH, tn), lambda i, j: (0, j)),
                  pl.BlockSpec((tm, tn), lambda i, j: (i, ga0 + j)),
                  pl.BlockSpec((tm, tn), lambda i, j: (i, ga0 + nN + j))],
        out_specs=pl.BlockSpec((tm, tn), lambda i, j: (i, j)),
        compiler_params=_cparams(("arbitrary", "arbitrary")),
        name="merge",
    )(o_nsa, h_conv, wa, wb, proj, proj)


def _layer_norm_rows(y, g, b):
    mu = jnp.mean(y, axis=-1, keepdims=True)
    yc = y - mu
    var = jnp.mean(yc * yc, axis=-1, keepdims=True)
    return yc * lax.rsqrt(var + LN_EPS) * g + b


def _split2(v):
    hi = v.astype(BF16)
    return hi, (v - hi.astype(F32)).astype(BF16)


def _outproj_kernel(mix_ref, w_ref, x_ref, g_ref, b_ref, rw_hi_ref, rw_lo_ref, rb_ref,
                    x1_ref, x1b_ref, ti_ref, tg_ref, rk_ref, cnt_ref):
    m = jnp.dot(mix_ref[...], w_ref[...], preferred_element_type=F32)
    x1 = _layer_norm_rows(DEEPNORM_ALPHA * x_ref[...] + m, g_ref[...], b_ref[...])
    x1_ref[...] = x1
    x1b_ref[...] = x1.astype(BF16)
    x_hi, x_lo = _split2(x1)
    logits = (jnp.dot(x_hi, rw_hi_ref[...], preferred_element_type=F32)
              + jnp.dot(x_hi, rw_lo_ref[...], preferred_element_type=F32)
              + jnp.dot(x_lo, rw_hi_ref[...], preferred_element_type=F32)) + rb_ref[...]
    lane = lax.broadcasted_iota(jnp.int32, logits.shape, 1)
    logits = jnp.where(lane < N_EXPERTS, logits, -jnp.inf)
    ti = jnp.zeros(logits.shape, jnp.int32)
    tv = jnp.zeros(logits.shape, F32)
    top0 = None
    den = jnp.zeros((logits.shape[0], 1), F32)
    picks = []
    for k in range(TOP_K):
        mx = jnp.max(logits, axis=-1, keepdims=True)
        idx = jnp.min(jnp.where(logits == mx, lane, LANES), axis=-1, keepdims=True)
        if top0 is None:
            top0 = mx
        e = jnp.exp(mx - top0)
        den = den + e
        ti = jnp.where(lane == k, idx, ti)
        tv = jnp.where(lane == k, e, tv)
        picks.append(lane == idx)
        logits = jnp.where(picks[-1], -jnp.inf, logits)
    ti_ref[...] = ti
    tg_ref[...] = tv / den

    @pl.when(pl.program_id(0) == 0)
    def _():
        cnt_ref[...] = jnp.zeros(cnt_ref.shape, F32)

    tm = logits.shape[0]
    chosen = functools.reduce(jnp.logical_or, picks).astype(BF16)
    tri = (lax.broadcasted_iota(jnp.int32, (tm, tm), 1)
           <= lax.broadcasted_iota(jnp.int32, (tm, tm), 0)).astype(BF16)
    upto = jnp.dot(tri, chosen, preferred_element_type=F32) + cnt_ref[pl.ds(0, 1), :]
    rk = jnp.zeros(logits.shape, jnp.int32)
    for k in range(TOP_K):
        r_k = jnp.sum(jnp.where(picks[k], upto, 0.0), axis=-1, keepdims=True) - 1.0
        rk = jnp.where(lane == k, r_k.astype(jnp.int32), rk)
    rk_ref[...] = rk
    cnt_ref[...] = jnp.broadcast_to(upto[tm - 1:tm, :], cnt_ref.shape)


def _outproj(mix, w_out, x2d, g, b, rw_hi, rw_lo, rb):
    T = mix.shape[0]
    tm = min(512, T)
    full = lambda shape: pl.BlockSpec(shape, lambda i: (0, 0))
    rowb = lambda w: pl.BlockSpec((tm, w), lambda i: (i, 0))
    return pl.pallas_call(
        _outproj_kernel,
        out_shape=(jax.ShapeDtypeStruct((T, D_MODEL), F32),
                   jax.ShapeDtypeStruct((T, D_MODEL), BF16),
                   jax.ShapeDtypeStruct((T, LANES), jnp.int32),
                   jax.ShapeDtypeStruct((T, LANES), F32),
                   jax.ShapeDtypeStruct((T, LANES), jnp.int32),
                   jax.ShapeDtypeStruct((SUBLANES, LANES), F32)),
        grid=(T // tm,),
        in_specs=[rowb(D_MODEL), full((D_MODEL, D_MODEL)), rowb(D_MODEL),
                  full((1, D_MODEL)), full((1, D_MODEL)),
                  full((D_MODEL, LANES)), full((D_MODEL, LANES)), full((1, LANES))],
        out_specs=(rowb(D_MODEL), rowb(D_MODEL), rowb(LANES), rowb(LANES), rowb(LANES),
                   full((SUBLANES, LANES))),
        compiler_params=_cparams(("arbitrary",)),
        name="outproj_ln_router",
    )(mix, w_out, x2d, g, b, rw_hi, rw_lo, rb)


def _moe_up_kernel(be_ref, nused_ref, x_ref, wg_ref, wu_ref, bg_ref, bu_ref, *rest, base):
    h_ref, wgb_ref, wub_ref = rest[-3:]
    local = pl.program_id(1)
    r = base + local
    prev = be_ref[jnp.maximum(r - 1, 0)]
    fresh = (local == 0) | (be_ref[r] != prev)

    @pl.when(fresh)
    def _():
        wgb_ref[...] = wg_ref[0].astype(BF16)
        wub_ref[...] = wu_ref[0].astype(BF16)

    @pl.when(r < nused_ref[0])
    def _():
        x = x_ref[...]
        gt = jnp.dot(x, wgb_ref[...], preferred_element_type=F32) + bg_ref[0]
        up = jnp.dot(x, wub_ref[...], preferred_element_type=F32) + bu_ref[0]
        gt = jnp.minimum(gt, SWIGLU_LIMIT)
        up = jnp.clip(up, -SWIGLU_LIMIT, SWIGLU_LIMIT)
        h = gt * jax.nn.sigmoid(SWIGLU_ALPHA * gt) * (up + 1.0)
        h_ref[...] = h.astype(h_ref.dtype)


def _moe_row_maps(nblk):
    rd = lambda r, nu: jnp.minimum(r, nu[0] - 1)
    wr = lambda r, nu: jnp.where(r < nu[0], r, nblk)
    return rd, wr


def _moe_up(blk_e, n_used, xs_seg, seg, nblk, hbuf, w_gate, w_up, b_gate, b_up):
    bm, tf = MOE_BLOCK, MOE_UP_TILE
    seg_blk = xs_seg.shape[0] // bm
    base = seg * seg_blk
    rd = lambda r, nu: jnp.clip(nu[0] - 1 - base, 0, r)
    wr = lambda r, nu: jnp.where(base + r < nu[0], base + r, nblk)
    wspec = pl.BlockSpec((1, D_MODEL, tf), lambda f, r, be, nu: (be[base + r], 0, f))
    bspec = pl.BlockSpec((1, 1, tf), lambda f, r, be, nu: (be[base + r], 0, f))
    in_specs = [pl.BlockSpec((bm, D_MODEL), lambda f, r, be, nu: (rd(r, nu), 0)),
                wspec, wspec, bspec, bspec]
    args = [blk_e, n_used, xs_seg, w_gate, w_up, b_gate.reshape(N_EXPERTS, 1, D_FF),
            b_up.reshape(N_EXPERTS, 1, D_FF)]
    aliases = {}
    if hbuf is not None:
        in_specs.append(pl.BlockSpec(memory_space=pl.ANY))
        aliases = {len(args): 0}
        args.append(hbuf)
    grid_spec = pltpu.PrefetchScalarGridSpec(
        num_scalar_prefetch=2,
        grid=(D_FF // tf, seg_blk),
        in_specs=in_specs,
        out_specs=pl.BlockSpec((bm, tf), lambda f, r, be, nu: (wr(r, nu), f)),
        scratch_shapes=[pltpu.VMEM((D_MODEL, tf), BF16), pltpu.VMEM((D_MODEL, tf), BF16)],
    )
    return pl.pallas_call(
        functools.partial(_moe_up_kernel, base=base),
        out_shape=jax.ShapeDtypeStruct(((nblk + 1) * bm, D_FF), BF16),
        grid_spec=grid_spec,
        input_output_aliases=aliases,
        compiler_params=_cparams(("arbitrary", "arbitrary")),
        name="moe_up",
    )(*args)


def _moe_down_kernel(be_ref, nused_ref, h_ref, wd_ref, bd_ref, o_ref, wdb_ref):
    r = pl.program_id(0)
    prev = be_ref[jnp.maximum(r - 1, 0)]
    fresh = (r == 0) | (be_ref[r] != prev)

    @pl.when(fresh)
    def _():
        wdb_ref[...] = wd_ref[0].astype(BF16)

    @pl.when(r < nused_ref[0])
    def _():
        y = jnp.dot(h_ref[...], wdb_ref[...], preferred_element_type=F32) + bd_ref[0]
        o_ref[...] = y.astype(o_ref.dtype)


def _moe_down(blk_e, n_used, h, w_down, b_down):
    bm = MOE_BLOCK
    nblk = h.shape[0] // bm - 1
    rd, wr = _moe_row_maps(nblk)
    grid_spec = pltpu.PrefetchScalarGridSpec(
        num_scalar_prefetch=2,
        grid=(nblk,),
        in_specs=[pl.BlockSpec((bm, D_FF), lambda r, be, nu: (rd(r, nu), 0)),
                  pl.BlockSpec((1, D_FF, D_MODEL), lambda r, be, nu: (be[r], 0, 0)),
                  pl.BlockSpec((1, 1, D_MODEL), lambda r, be, nu: (be[r], 0, 0))],
        out_specs=pl.BlockSpec((bm, D_MODEL), lambda r, be, nu: (wr(r, nu), 0)),
        scratch_shapes=[pltpu.VMEM((D_FF, D_MODEL), BF16)],
    )
    return pl.pallas_call(
        _moe_down_kernel,
        out_shape=jax.ShapeDtypeStruct(((nblk + 1) * bm, D_MODEL), BF16),
        grid_spec=grid_spec,
        compiler_params=_cparams(("arbitrary",)),
        name="moe_down",
    )(blk_e, n_used, h, w_down, b_down.reshape(N_EXPERTS, 1, D_MODEL))


def _final_kernel(x1_ref, *refs):
    y_refs, (tg_ref, g_ref, b_ref, o_ref) = refs[:TOP_K], refs[TOP_K:]
    tg = tg_ref[...]
    f = tg[:, 0:1] * y_refs[0][...].astype(F32)
    for k in range(1, TOP_K):
        f = f + tg[:, k:k + 1] * y_refs[k][...].astype(F32)
    o_ref[...] = _layer_norm_rows(DEEPNORM_ALPHA * x1_ref[...] + f, g_ref[...], b_ref[...])


def _final(x1, ys, tg, g, b):
    T = x1.shape[0]
    tm = min(512, T)
    rows = pl.BlockSpec((tm, D_MODEL), lambda i: (i, 0))
    vec = pl.BlockSpec((1, D_MODEL), lambda i: (0, 0))
    return pl.pallas_call(
        _final_kernel,
        out_shape=jax.ShapeDtypeStruct((T, D_MODEL), F32),
        grid=(T // tm,),
        in_specs=[rows] + [rows] * TOP_K + [pl.BlockSpec((tm, LANES), lambda i: (i, 0)), vec, vec],
        out_specs=rows,
        compiler_params=_cparams(("arbitrary",)),
        name="combine_ln",
    )(x1, *ys, tg, g, b)


_W_IN_SPLITS = np.cumsum([Q_DIM] + [KV_DIM] * 6 + [NSA_GATE_DIM, 2 * CONV_CH, 2 * D_MODEL])
IN_DIM = int(_W_IN_SPLITS[-1])
WPREP_ROWS = 256


def _wprep_kernel(w_ref, o_ref):
    sp = _W_IN_SPLITS
    o_ref[:, COL_Q:COL_Q + Q_DIM] = (
        w_ref[0, :, 0:sp[0]] * (LOG2E / math.sqrt(HEAD_DIM))).astype(BF16)
    o_ref[:, COL_GLU:COL_GLU + 2 * CONV_CH] = w_ref[0, :, sp[7]:sp[8]].astype(BF16)
    o_ref[:, COL_MERGE:COL_MERGE + 2 * D_MODEL] = w_ref[0, :, sp[8]:sp[9]].astype(BF16)
    o_ref[:, COL_KC:COL_KC + 6 * KV_DIM] = w_ref[0, :, sp[0]:sp[6]].astype(BF16)
    per_g = HEADS_PER_GROUP * 3
    zeros = jnp.zeros((w_ref.shape[1], LANES - per_g), BF16)
    for g in range(N_KV_GROUPS):
        gate = w_ref[0, :, sp[6] + g * per_g:sp[6] + (g + 1) * per_g].astype(BF16)
        o_ref[:, COL_GATE + g * LANES:COL_GATE + (g + 1) * LANES] = jnp.concatenate(
            [gate, zeros], axis=1)


def _prep_w_in(w_in3):
    D = w_in3.shape[1]
    rows = min(WPREP_ROWS, D)
    return pl.pallas_call(
        _wprep_kernel,
        out_shape=jax.ShapeDtypeStruct((D, PROJ_W), BF16),
        grid=(D // rows,),
        in_specs=[pl.BlockSpec((1, rows, IN_DIM), lambda i: (0, i, 0))],
        out_specs=pl.BlockSpec((rows, PROJ_W), lambda i: (i, 0)),
        compiler_params=_cparams(("arbitrary",)),
        name="w_in_layout",
    )(w_in3)


def _mixer(x2d, B, S, w_in, cmp_pe, cmp_w1, cmp_b1, cmp_w2, cmp_b2, w_nsa_proj,
           conv_w, conv_b, conv_ln_g, conv_ln_b, w_conv_proj):
    T = B * S
    G = N_KV_GROUPS
    proj = _inproj(x2d, _prep_w_in(w_in))

    kcvc = _compress(proj, cmp_pe.reshape(2, 1, CMP_BLOCK * HEAD_DIM),
                     cmp_w1.astype(BF16), cmp_b1.reshape(2, 1, HEAD_DIM),
                     cmp_w2.astype(BF16), cmp_b2.reshape(2, 1, HEAD_DIM), B, S)

    o_nsa = _nsa(proj, kcvc, _alibi_rows().astype(BF16), B, S)

    h_conv = _conv(proj, conv_w, conv_b, conv_ln_g, conv_ln_b, B, S)
    return _merge(o_nsa, h_conv, w_nsa_proj.astype(BF16), w_conv_proj.astype(BF16), proj)


def _moe(x1b, top_i, top_rank, expert_counts, w_gate, b_gate, w_up, b_up, w_down, b_down):
    T = x1b.shape[0]
    A = T * TOP_K
    bm = MOE_BLOCK
    flat_e = top_i[:, :TOP_K].reshape(A)
    rank = top_rank[:, :TOP_K].reshape(A)
    counts = expert_counts[0, :N_EXPERTS].astype(jnp.int32)
    padded = (counts + bm - 1) // bm * bm
    pad_end = jnp.cumsum(padded)
    pad_start = pad_end - padded
    dest = pad_start[flat_e] + rank
    P = -(-A // bm) * bm + N_EXPERTS * bm
    nblk = P // bm
    flat_tok = jnp.arange(A, dtype=jnp.int32) // TOP_K
    buf_tok = (jnp.arange(P, dtype=jnp.int32) % T).at[dest].set(flat_tok)
    blk_start = jnp.arange(nblk, dtype=jnp.int32) * bm
    blk_e = jnp.minimum(jnp.sum(blk_start[:, None] >= pad_end[None, :], axis=1),
                        N_EXPERTS - 1).astype(jnp.int32)
    n_used = (pad_end[-1] // bm).astype(jnp.int32).reshape(1)
    seg_rows = P // MOE_SEGMENTS
    assert seg_rows % bm == 0
    h = None
    for seg in range(MOE_SEGMENTS):
        xs_seg = x1b[buf_tok[seg * seg_rows:(seg + 1) * seg_rows]]
        h = _moe_up(blk_e, n_used, xs_seg, seg, nblk, h, w_gate, w_up, b_gate, b_up)
    out = _moe_down(blk_e, n_used, h, w_down, b_down)
    dest_k = dest.reshape(T, TOP_K)
    return [out[dest_k[:, k]] for k in range(TOP_K)]


def kernel(x, w_in, cmp_pe, cmp_w1, cmp_b1, cmp_w2, cmp_b2, w_nsa_proj, conv_w, conv_b, conv_ln_g, conv_ln_b, w_conv_proj, w_out, ln1_g, ln1_b, router_w, router_b, w_gate, b_gate, w_up, b_up, w_down, b_down, ln2_g, ln2_b):
    B, S, D = x.shape
    T = B * S
    x2d = x.reshape(T, D)
    for l in range(DEPTH):
        mix = _mixer(x2d, B, S, w_in[l:l + 1], cmp_pe[l], cmp_w1[l], cmp_b1[l], cmp_w2[l], cmp_b2[l],
                     w_nsa_proj[l], conv_w[l], conv_b[l], conv_ln_g[l], conv_ln_b[l],
                     w_conv_proj[l])
        rw = jnp.pad(router_w[l], ((0, 0), (0, LANES - N_EXPERTS)))
        rw_hi = rw.astype(BF16)
        rw_lo = (rw - rw_hi.astype(F32)).astype(BF16)
        rb = jnp.pad(router_b[l], (0, LANES - N_EXPERTS)).reshape(1, LANES)
        x1, x1b, top_i, top_g, top_rank, expert_counts = _outproj(
            mix, w_out[l].astype(BF16), x2d, ln1_g[l].reshape(1, D), ln1_b[l].reshape(1, D),
            rw_hi, rw_lo, rb)
        y4 = _moe(x1b, top_i, top_rank, expert_counts, w_gate[l], b_gate[l], w_up[l], b_up[l],
                  w_down[l], b_down[l])
        x2d = _final(x1, y4, top_g, ln2_g[l].reshape(1, D), ln2_b[l].reshape(1, D))
    return x2d.reshape(B, S, D)
```
